```python
import math
import jax, jax.numpy as jnp
from jax import lax
import numpy as np

D_MODEL = 1024
BATCH = 8
SEQ = 2048
DEPTH = 2
DEC_BATCH = 128
DEC_SEQ = 4
PAST_LEN = 16384
PAGE_SIZE = 128

GDN_HEADS = 4
GDN_DK = 128
GDN_DV = 128
GDN_K = GDN_HEADS * GDN_DK
GDN_V = GDN_HEADS * GDN_DV
CONV_W = 4
GDN_CONV_DIM = 2 * GDN_K + GDN_V
ML_HEADS = 4
ML_DK = 64
ML_DV = 128
ML_K = ML_HEADS * ML_DK
ML_V = ML_HEADS * ML_DV
D_FF = 4 * D_MODEL
CHUNK = 64
EPS = 1e-6
IN_SPLITS = (GDN_K, GDN_K, GDN_V, GDN_V, GDN_HEADS, GDN_HEADS,
             ML_K, ML_K, ML_V, ML_V, ML_HEADS, ML_HEADS, D_MODEL, D_MODEL)
IN_DIM = sum(IN_SPLITS)

kernel_name = 'hybrid_gdn_mlstm_gated_merge_step'


def _split_cols(a, sizes):
    offs = [int(o) for o in np.cumsum(sizes)[:-1]]
    return jnp.split(a, offs, axis=-1)


def _rmsnorm(x, w):
    xf = x.astype(jnp.float32)
    y = xf * lax.rsqrt(jnp.mean(xf * xf, axis=-1, keepdims=True) + EPS) * w.astype(jnp.float32)
    return y.astype(x.dtype)


def _head_rmsnorm(x, w):
    D = x.shape[-1]
    y = x * lax.rsqrt(jnp.mean(x * x, axis=-1, keepdims=True) + EPS)
    return y * w.astype(jnp.float32).reshape(-1, D)


def _l2norm(x):
    return x * lax.rsqrt(jnp.sum(x * x, axis=-1, keepdims=True) + EPS)


def _chunk_len(T):
    return CHUNK if T % CHUNK == 0 else T


def _to_chunks(a, L):
    B, T, H = a.shape[:3]
    a = a.reshape((B, T // L, L, H) + a.shape[3:])
    return jnp.moveaxis(a, (1, 3), (0, 2))


def _from_chunks(o):
    o = jnp.moveaxis(o, (0, 2), (1, 3))
    B, N, L, H, D = o.shape
    return o.reshape(B, N * L, H, D)


def _causal_conv_silu(x, buf, w):
    T = x.shape[1]
    xp = jnp.concatenate([buf.astype(x.dtype), x], axis=1)
    y = xp[:, 0:T] * w[0]
    for j in range(1, CONV_W):
        y = y + xp[:, j:j + T] * w[j]
    return jax.nn.silu(y), xp[:, -(CONV_W - 1):]


def _gated_delta_chunked(q, k, v, g, beta, S0):
    L = _chunk_len(q.shape[1])
    q, k, v, g, beta = (_to_chunks(a, L) for a in (q, k, v, g, beta))
    incl = jnp.tril(jnp.ones((L, L), bool))
    strict = jnp.tril(jnp.ones((L, L), bool), -1)
    gc = jnp.cumsum(g, axis=-1)
    decay = jnp.exp(jnp.where(incl, gc[..., :, None] - gc[..., None, :], -jnp.inf))
    kb = k * beta[..., None]
    A = jnp.where(strict, jnp.einsum('nbhid,nbhjd->nbhij', kb, k) * decay, 0.0)
    eye = jnp.broadcast_to(jnp.eye(L, dtype=A.dtype), A.shape)
    Tinv = lax.linalg.triangular_solve(A, eye, left_side=True, lower=True, unit_diagonal=True)
    w = jnp.einsum('nbhij,nbhjd->nbhid', Tinv, kb * jnp.exp(gc)[..., None])
    u = jnp.einsum('nbhij,nbhje->nbhie', Tinv, v * beta[..., None])
    qk = jnp.where(incl, jnp.einsum('nbhid,nbhjd->nbhij', q, k) * decay, 0.0)
    qd = q * jnp.exp(gc)[..., None]
    kd = k * jnp.exp(gc[..., -1:] - gc)[..., None]
    dl = jnp.exp(gc[..., -1])

    def step(S, xs):
        qd_c, qk_c, u_c, w_c, kd_c, dl_c = xs
        v_new = u_c - jnp.einsum('bhld,bhde->bhle', w_c, S)
        o = jnp.einsum('bhld,bhde->bhle', qd_c, S) + jnp.einsum('bhij,bhje->bhie', qk_c, v_new)
        S = S * dl_c[..., None, None] + jnp.einsum('bhld,bhle->bhde', kd_c, v_new)
        return S, o

    S, o = lax.scan(step, S0, (qd, qk, u, w, kd, dl))
    return _from_chunks(o), S


def _mlstm_chunked(q, k, v, log_i, log_f, C0, n0, m0):
    L = _chunk_len(q.shape[1])
    q, k, v, log_i, log_f = (_to_chunks(a, L) for a in (q, k, v, log_i, log_f))
    incl = jnp.tril(jnp.ones((L, L), bool))
    F = jnp.cumsum(log_f, axis=-1)
    D = jnp.where(incl, F[..., :, None] - F[..., None, :] + log_i[..., None, :], -jnp.inf)
    D_max = jnp.max(D, axis=-1)
    qk = jnp.einsum('nbhid,nbhjd->nbhij', q, k)
    G = log_i + F[..., -1:] - F
    G_max = jnp.max(G, axis=-1)

    def step(carry, xs):
        C, n, m = carry
        q_c, k_c, v_c, F_c, D_c, Dm_c, qk_c, G_c, Gm_c = xs
        inter = m[..., None] + F_c
        m_row = jnp.maximum(Dm_c, inter)
        Sm = jnp.exp(D_c - m_row[..., None]) * qk_c
        a = jnp.exp(inter - m_row)
        num = a[..., None] * jnp.einsum('bhld,bhde->bhle', q_c, C) + jnp.einsum('bhij,bhje->bhie', Sm, v_c)
        den = a * jnp.einsum('bhld,bhd->bhl', q_c, n) + jnp.sum(Sm, axis=-1)
        h = num / jnp.maximum(jnp.abs(den), jnp.exp(-m_row))[..., None]
        Fl = F_c[..., -1]
        m_new = jnp.maximum(m + Fl, Gm_c)
        dec = jnp.exp(m + Fl - m_new)
        wC = jnp.exp(G_c - m_new[..., None])
        C = dec[..., None, None] * C + jnp.einsum('bhl,bhld,bhle->bhde', wC, k_c, v_c)
        n = dec[..., None] * n + jnp.einsum('bhl,bhld->bhd', wC, k_c)
        return (C, n, m_new), h

    (C, n, m), h = lax.scan(step, (C0, n0, m0), (q, k, v, F, D, D_max, qk, G, G_max))
    return _from_chunks(h), C, n, m


def _mixer(h, conv_buf, S0, C0, n0, m0, w_in, gdn_conv_w, gdn_A_log, gdn_dt_bias, gdn_norm,
           ml_i_bias, ml_f_bias, ml_norm, w_branch_gdn, w_branch_ml, w_out):
    B, T, _ = h.shape
    f32 = jnp.float32
    (qg, kg, vg, zg, bg, ag, qm, km, vm, om, im, fm, gate_g, gate_m) = _split_cols(h @ w_in, IN_SPLITS)
    qkv, new_buf = _causal_conv_silu(jnp.concatenate([qg, kg, vg], axis=-1), conv_buf, gdn_conv_w)
    qg, kg, vg = _split_cols(qkv.astype(f32), (GDN_K, GDN_K, GDN_V))
    qg = _l2norm(qg.reshape(B, T, GDN_HEADS, GDN_DK)) * GDN_DK ** -0.5
    kg = _l2norm(kg.reshape(B, T, GDN_HEADS, GDN_DK))
    vg = vg.reshape(B, T, GDN_HEADS, GDN_DV)
    beta = jax.nn.sigmoid(bg.astype(f32))
    g = -jnp.exp(gdn_A_log.astype(f32)) * jax.nn.softplus(ag.astype(f32) + gdn_dt_bias.astype(f32))
    og, S_new = _gated_delta_chunked(qg, kg, vg, g, beta, S0.astype(f32))
    og = _head_rmsnorm(og, gdn_norm) * jax.nn.silu(zg.astype(f32)).reshape(B, T, GDN_HEADS, GDN_DV)
    br_g = og.reshape(B, T, GDN_V).astype(h.dtype) @ w_branch_gdn
    qm = qm.astype(f32).reshape(B, T, ML_HEADS, ML_DK) * ML_DK ** -0.5
    km = km.astype(f32).reshape(B, T, ML_HEADS, ML_DK)
    vm = vm.astype(f32).reshape(B, T, ML_HEADS, ML_DV)
    log_i = im.astype(f32) + ml_i_bias.astype(f32)
    log_f = jax.nn.log_sigmoid(fm.astype(f32) + ml_f_bias.astype(f32))
    hm, C_new, n_new, m_new = _mlstm_chunked(qm, km, vm, log_i, log_f,
                                            C0.astype(f32), n0.astype(f32), m0.astype(f32))
    hm = _head_rmsnorm(hm, ml_norm) * jax.nn.sigmoid(om.astype(f32)).reshape(B, T, ML_HEADS, ML_DV)
    br_m = hm.reshape(B, T, ML_V).astype(h.dtype) @ w_branch_ml
    merged = jax.nn.sigmoid(gate_g) * br_g + jax.nn.sigmoid(gate_m) * br_m
    dt = h.dtype
    return merged @ w_out, (new_buf.astype(dt), S_new.astype(dt), C_new.astype(dt), n_new.astype(dt), m_new.astype(dt))


def _trunk(x, conv_buf, gdn_S, ml_C, ml_n, ml_m, norm_mix, w_in, gdn_conv_w, gdn_A_log, gdn_dt_bias,
           gdn_norm, ml_i_bias, ml_f_bias, ml_norm, w_branch_gdn, w_branch_ml, w_out, norm_mlp,
           w_up, w_down, norm_final):
    new = ([], [], [], [], [])
    for l in range(DEPTH):
        mix, st = _mixer(_rmsnorm(x, norm_mix[l]), conv_buf[l], gdn_S[l], ml_C[l], ml_n[l], ml_m[l],
                         w_in[l], gdn_conv_w[l], gdn_A_log[l], gdn_dt_bias[l], gdn_norm[l],
                         ml_i_bias[l], ml_f_bias[l], ml_norm[l], w_branch_gdn[l], w_branch_ml[l], w_out[l])
        x = x + mix
        hmlp = _rmsnorm(x, norm_mlp[l])
        x = x + jnp.square(jax.nn.relu(hmlp @ w_up[l])) @ w_down[l]
        for lst, s in zip(new, st):
            lst.append(s)
    y = _rmsnorm(x, norm_final)
    conv_n, S_n, C_n, n_n, m_n = (jnp.stack(lst) for lst in new)
    return y, conv_n, S_n, C_n, n_n, m_n


def setup_inputs(seed: int = 0) -> dict:
    key = jax.random.key(seed)
    ks = jax.random.split(key, 24)
    f32 = jnp.float32

    def nrm(k, shape, scale):
        return jax.random.normal(k, shape, f32) * scale

    x_prompt = nrm(ks[0], (BATCH, SEQ, D_MODEL), 1.0)
    x_sample = nrm(ks[1], (DEC_BATCH, DEC_SEQ, D_MODEL), 1.0)
    state_gdn_conv = nrm(ks[2], (DEPTH, DEC_BATCH, CONV_W - 1, GDN_CONV_DIM), 1.0)
    state_gdn_S = nrm(ks[3], (DEPTH, DEC_BATCH, GDN_HEADS, GDN_DK, GDN_DV), GDN_DK ** -0.5)
    state_mlstm_C = nrm(ks[4], (DEPTH, DEC_BATCH, ML_HEADS, ML_DK, ML_DV), 0.1)
    state_mlstm_n = nrm(ks[5], (DEPTH, DEC_BATCH, ML_HEADS, ML_DK), 0.1)
    state_mlstm_m = jax.random.uniform(ks[6], (DEPTH, DEC_BATCH, ML_HEADS), f32, 0.0, 3.0)
    norm_mix = 1.0 + nrm(ks[7], (DEPTH, D_MODEL), 0.02)
    w_in = nrm(ks[8], (DEPTH, D_MODEL, IN_DIM), D_MODEL ** -0.5)
    gdn_conv_w = nrm(ks[9], (DEPTH, CONV_W, GDN_CONV_DIM), CONV_W ** -0.5)
    gdn_A_log = jnp.log(jax.random.uniform(ks[10], (DEPTH, GDN_HEADS), f32, 0.5, 4.0))
    dt0 = jnp.exp(jax.random.uniform(ks[11], (DEPTH, GDN_HEADS), f32, math.log(1e-3), math.log(1e-1)))
    gdn_dt_bias = dt0 + jnp.log(-jnp.expm1(-dt0))
    gdn_norm = 1.0 + nrm(ks[12], (DEPTH, GDN_DV), 0.02)
    ml_i_bias = nrm(ks[13], (DEPTH, ML_HEADS), 0.1) - 1.0
    ml_f_bias = jax.random.uniform(ks[14], (DEPTH, ML_HEADS), f32, 3.0, 6.0)
    ml_norm = 1.0 + nrm(ks[15], (DEPTH, ML_V), 0.02)
    w_branch_gdn = nrm(ks[16], (DEPTH, GDN_V, D_MODEL), GDN_V ** -0.5)
    w_branch_ml = nrm(ks[17], (DEPTH, ML_V, D_MODEL), ML_V ** -0.5)
    w_out = nrm(ks[18], (DEPTH, D_MODEL, D_MODEL), D_MODEL ** -0.5)
    norm_mlp = 1.0 + nrm(ks[19], (DEPTH, D_MODEL), 0.02)
    w_up = nrm(ks[20], (DEPTH, D_MODEL, D_FF), D_MODEL ** -0.5)
    w_down = nrm(ks[21], (DEPTH, D_FF, D_MODEL), D_FF ** -0.5)
    norm_final = 1.0 + nrm(ks[22], (D_MODEL,), 0.02)
    return {'x_prompt': x_prompt, 'x_sample': x_sample,
            'state_gdn_conv': state_gdn_conv, 'state_gdn_S': state_gdn_S,
            'state_mlstm_C': state_mlstm_C, 'state_mlstm_n': state_mlstm_n, 'state_mlstm_m': state_mlstm_m,
            'norm_mix': norm_mix, 'w_in': w_in, 'gdn_conv_w': gdn_conv_w, 'gdn_A_log': gdn_A_log,
            'gdn_dt_bias': gdn_dt_bias, 'gdn_norm': gdn_norm, 'ml_i_bias': ml_i_bias, 'ml_f_bias': ml_f_bias,
            'ml_norm': ml_norm, 'w_branch_gdn': w_branch_gdn, 'w_branch_ml': w_branch_ml, 'w_out': w_out,
            'norm_mlp': norm_mlp, 'w_up': w_up, 'w_down': w_down, 'norm_final': norm_final}


def reference(x_prompt, x_sample, state_gdn_conv, state_gdn_S, state_mlstm_C, state_mlstm_n, state_mlstm_m,
              norm_mix, w_in, gdn_conv_w, gdn_A_log, gdn_dt_bias, gdn_norm, ml_i_bias, ml_f_bias, ml_norm,
              w_branch_gdn, w_branch_ml, w_out, norm_mlp, w_up, w_down, norm_final):
    B = x_prompt.shape[0]
    dt = x_prompt.dtype
    z_conv = jnp.zeros((DEPTH, B, CONV_W - 1, GDN_CONV_DIM), dt)
    z_S = jnp.zeros((DEPTH, B, GDN_HEADS, GDN_DK, GDN_DV), dt)
    z_C = jnp.zeros((DEPTH, B, ML_HEADS, ML_DK, ML_DV), dt)
    z_n = jnp.zeros((DEPTH, B, ML_HEADS, ML_DK), dt)
    z_m = jnp.zeros((DEPTH, B, ML_HEADS), dt)
    y_prompt, conv_p, S_p, C_p, n_p, m_p = _trunk(
        x_prompt, z_conv, z_S, z_C, z_n, z_m, norm_mix, w_in, gdn_conv_w, gdn_A_log, gdn_dt_bias,
        gdn_norm, ml_i_bias, ml_f_bias, ml_norm, w_branch_gdn, w_branch_ml, w_out, norm_mlp,
        w_up, w_down, norm_final)
    y_sample, conv_s, S_s, C_s, n_s, m_s = _trunk(
        x_sample, state_gdn_conv, state_gdn_S, state_mlstm_C, state_mlstm_n, state_mlstm_m,
        norm_mix, w_in, gdn_conv_w, gdn_A_log, gdn_dt_bias, gdn_norm, ml_i_bias, ml_f_bias, ml_norm,
        w_branch_gdn, w_branch_ml, w_out, norm_mlp, w_up, w_down, norm_final)
    return (y_prompt, y_sample, conv_p, S_p, C_p, n_p, m_p, conv_s, S_s, C_s, n_s, m_s)
```

```python
import functools

import jax
import jax.numpy as jnp
from jax import lax
from jax.experimental import pallas as pl
from jax.experimental.pallas import tpu as pltpu

F32 = jnp.float32
BF16 = jnp.bfloat16
EPS = 1e-6
CONV_W = 4
LANES = 128
SUBLANES = 8
GATE_ROWS = 16
CHUNK = 64
NEG_BIG = -1e30
VMEM_LIMIT = 48 * 1024 * 1024
HIGHEST = lax.Precision.HIGHEST


def _sigmoid(x):
    return 1.0 / (1.0 + jnp.exp(-x))


def _softplus(x):
    return jnp.maximum(x, 0.0) + jnp.log(1.0 + jnp.exp(-jnp.abs(x)))


def _mm(a, b):
    return jnp.dot(a.astype(BF16), b.astype(BF16), preferred_element_type=F32)


def _mm_nt(a, b):
    return lax.dot_general(a.astype(BF16), b.astype(BF16), (((1,), (1,)), ((), ())), preferred_element_type=F32)


def _mm_tn(a, b):
    return lax.dot_general(a.astype(BF16), b.astype(BF16), (((0,), (0,)), ((), ())), preferred_element_type=F32)


def _split_bf16(a):
    hi = a.astype(BF16)
    return hi, (a - hi.astype(F32)).astype(BF16)


def _tile(n, cap, mult):
    best = None
    for t in range(mult, min(n, cap) + 1, mult):
        if n % t == 0:
            best = t
    return best if best is not None else n


def _tri_masks(L):
    row = lax.broadcasted_iota(jnp.int32, (L, L), 0)
    col = lax.broadcasted_iota(jnp.int32, (L, L), 1)
    return row >= col, row > col


def _cumsum_both(g_col, g_row, incl):
    L = incl.shape[0]
    row = lax.broadcasted_iota(jnp.int32, (L, L), 0)
    col = lax.broadcasted_iota(jnp.int32, (L, L), 1)
    lower = jnp.where(incl, 1.0, 0.0)
    upper = jnp.where(row <= col, 1.0, 0.0)
    c_col = jnp.dot(lower, g_col, precision=HIGHEST, preferred_element_type=F32)
    c_row = jnp.dot(g_row, upper, precision=HIGHEST, preferred_element_type=F32)
    return c_col, c_row


def _tri_inv_unit_lower(A):
    L = A.shape[0]
    row = lax.broadcasted_iota(jnp.int32, (L, L), 0)
    col = lax.broadcasted_iota(jnp.int32, (L, L), 1)
    eye = jnp.where(row == col, 1.0, 0.0)
    P = -A
    X = eye + P
    p = 2
    while p < L:
        P = _mm(P, P)
        X = X + _mm(X, P)
        p *= 2
    a_hi, a_lo = _split_bf16(A)
    x_hi, x_lo = _split_bf16(X)
    AX = (jnp.dot(a_hi, x_hi, preferred_element_type=F32)
          + (jnp.dot(a_hi, x_lo, preferred_element_type=F32) + jnp.dot(a_lo, x_hi, preferred_element_type=F32)))
    return X + _mm(X, (eye - X) - AX)


def _norm_proj_kernel(x_ref, g_ref, w_ref, o_ref, xn_ref):
    @pl.when(pl.program_id(1) == 0)
    def _():
        x = x_ref[...]
        ms = jnp.mean(x * x, axis=-1, keepdims=True)
        xn_ref[...] = (x * lax.rsqrt(ms + EPS) * g_ref[...]).astype(BF16)

    o_ref[...] = jnp.dot(xn_ref[...], w_ref[...], preferred_element_type=F32)


def _norm_proj(x, gamma, w):
    M, D = x.shape
    N = w.shape[1]
    tm = _tile(M, 1024, SUBLANES)
    tn = _tile(N, 1280, LANES)
    return pl.pallas_call(
        _norm_proj_kernel,
        grid=(M // tm, N // tn),
        in_specs=[pl.BlockSpec((tm, D), lambda i, j: (i, 0)),
                  pl.BlockSpec((1, D), lambda i, j: (0, 0)),
                  pl.BlockSpec((D, tn), lambda i, j: (0, j))],
        out_specs=pl.BlockSpec((tm, tn), lambda i, j: (i, j)),
        out_shape=jax.ShapeDtypeStruct((M, N), F32),
        scratch_shapes=[pltpu.VMEM((tm, D), BF16)],
        compiler_params=pltpu.CompilerParams(dimension_semantics=("parallel", "arbitrary"),
                                             vmem_limit_bytes=VMEM_LIMIT),
        name="norm_proj",
    )(x, gamma, w)


def _gdn_kernel(qkv_ref, z_ref, sm_ref, smT_ref, conv0_ref, S0_ref, cw_ref, prow_ref, pcol_ref, gnorm_ref,
                o_ref, convn_ref, S_ref, xp_ref, *, L, NC, T_valid, H, DK, DV):
    n = pl.program_id(1)
    K = H * DK
    pad0 = SUBLANES - (CONV_W - 1)

    @pl.when(n == 0)
    def _():
        xp_ref[pad0:SUBLANES, :] = conv0_ref[0]
        S_ref[...] = S0_ref[...]

    xp_ref[SUBLANES:SUBLANES + L, :] = qkv_ref[...]
    y = xp_ref[pad0:pad0 + L, :] * cw_ref[0:1, :]
    for j in range(1, CONV_W):
        y = y + xp_ref[pad0 + j:pad0 + j + L, :] * cw_ref[j:j + 1, :]
    qkv = y * _sigmoid(y)
    lv = L if NC > 1 else T_valid
    tail = xp_ref[SUBLANES + lv - (CONV_W - 1):SUBLANES + lv, :]
    xp_ref[pad0:SUBLANES, :] = tail

    @pl.when(n == NC - 1)
    def _():
        convn_ref[0] = tail

    sm = sm_ref[...]
    smT = smT_ref[0]
    beta_c = _sigmoid(sm)
    beta_r = _sigmoid(smT)
    g_c = -jnp.exp(prow_ref[0:1, :]) * _softplus(sm + prow_ref[1:2, :])
    g_r = -jnp.exp(pcol_ref[:, 0:1]) * _softplus(smT + pcol_ref[:, 1:2])
    if T_valid < NC * L:
        vc = lax.broadcasted_iota(jnp.int32, (L, 1), 0) < T_valid
        vr = lax.broadcasted_iota(jnp.int32, (1, L), 1) < T_valid
        beta_c = jnp.where(vc, beta_c, 0.0)
        beta_r = jnp.where(vr, beta_r, 0.0)
        g_c = jnp.where(vc, g_c, 0.0)
        g_r = jnp.where(vr, g_r, 0.0)
    incl, strict = _tri_masks(L)
    gc_c, gc_r = _cumsum_both(g_c, g_r, incl)
    del beta_r

    for h in range(H):
        qh = qkv[:, h * DK:(h + 1) * DK]
        kh = qkv[:, K + h * DK:K + (h + 1) * DK]
        vh = qkv[:, 2 * K + h * DV:2 * K + (h + 1) * DV]
        qh = qh * lax.rsqrt(jnp.sum(qh * qh, axis=-1, keepdims=True) + EPS) * (DK ** -0.5)
        kh = kh * lax.rsqrt(jnp.sum(kh * kh, axis=-1, keepdims=True) + EPS)
        b_c = beta_c[:, h:h + 1]
        gcc = gc_c[:, H + h:H + h + 1]
        gcr = gc_r[H + h:H + h + 1, :]
        gl = gcc[L - 1:L, :]
        decay = jnp.exp(jnp.where(incl, gcc - gcr, -jnp.inf))
        eg = jnp.exp(gcc)
        kb = kh * b_c
        A = jnp.where(strict, _mm_nt(kb, kh) * decay, 0.0)
        Tinv = _tri_inv_unit_lower(A)
        w = _mm(Tinv, kb * eg)
        u = _mm(Tinv, vh * b_c)
        qk = _mm_nt(qh, kh) * decay
        S = S_ref[0, h]
        v_new = u - _mm(w, S)
        o = _mm(qh * eg, S) + _mm(qk, v_new)
        kd = kh * jnp.exp(gl - gcc)
        S_ref[0, h] = S * jnp.exp(gl) + _mm_tn(kd, v_new)
        o = o * lax.rsqrt(jnp.mean(o * o, axis=-1, keepdims=True) + EPS) * gnorm_ref[:, h * DV:(h + 1) * DV]
        zh = z_ref[:, h * DV:(h + 1) * DV]
        o_ref[:, h * DV:(h + 1) * DV] = o * (zh * _sigmoid(zh))


def _gdn(proj, smT, conv0, S0, cw, prow, pcol, gnorm, *, B, NC, L, T_valid, offs):
    _, H, DK, DV = S0.shape
    CG = conv0.shape[-1]
    V = H * DV
    M = proj.shape[0]
    kern = functools.partial(_gdn_kernel, L=L, NC=NC, T_valid=T_valid, H=H, DK=DK, DV=DV)
    rowblk = lambda b, n: b * NC + n
    return pl.pallas_call(
        kern,
        grid=(B, NC),
        in_specs=[pl.BlockSpec((L, CG), lambda b, n: (rowblk(b, n), offs["qkv"] // CG)),
                  pl.BlockSpec((L, V), lambda b, n: (rowblk(b, n), offs["z"] // V)),
                  pl.BlockSpec((L, LANES), lambda b, n: (rowblk(b, n), offs["small"] // LANES)),
                  pl.BlockSpec((1, GATE_ROWS, L), lambda b, n: (rowblk(b, n), 0, 0)),
                  pl.BlockSpec((1, CONV_W - 1, CG), lambda b, n: (b, 0, 0)),
                  pl.BlockSpec((1, H, DK, DV), lambda b, n: (b, 0, 0, 0)),
                  pl.BlockSpec((CONV_W, CG), lambda b, n: (0, 0)),
                  pl.BlockSpec((2, LANES), lambda b, n: (0, 0)),
                  pl.BlockSpec((GATE_ROWS, 2), lambda b, n: (0, 0)),
                  pl.BlockSpec((1, V), lambda b, n: (0, 0))],
        out_specs=[pl.BlockSpec((L, V), lambda b, n: (rowblk(b, n), 0)),
                   pl.BlockSpec((1, CONV_W - 1, CG), lambda b, n: (b, 0, 0)),
                   pl.BlockSpec((1, H, DK, DV), lambda b, n: (b, 0, 0, 0))],
        out_shape=[jax.ShapeDtypeStruct((M, V), F32),
                   jax.ShapeDtypeStruct(conv0.shape, F32),
                   jax.ShapeDtypeStruct(S0.shape, F32)],
        scratch_shapes=[pltpu.VMEM((SUBLANES + L, CG), F32)],
        compiler_params=pltpu.CompilerParams(dimension_semantics=("parallel", "arbitrary"),
                                             vmem_limit_bytes=VMEM_LIMIT),
        name="gdn_chunk",
    )(proj, proj, proj, smT, conv0, S0, cw, prow, pcol, gnorm)


def _mlstm_kernel(qk_ref, v_ref, og_ref, sm_ref, smT_ref, C0_ref, n0_ref, m0_ref, prow_ref, pcol_ref, norm_ref,
                  h_ref, C_ref, n_ref, m_ref, *, L, NC, T_valid, H, DK, DV):
    step = pl.program_id(1)
    K = H * DK

    @pl.when(step == 0)
    def _():
        C_ref[...] = C0_ref[...]
        n_ref[...] = n0_ref[...]
        m_ref[...] = m0_ref[...]

    qk_in = qk_ref[...]
    q_all = qk_in[:, :K] * (DK ** -0.5)
    k_all = qk_in[:, K:]
    sm = sm_ref[...]
    smT = smT_ref[0]
    li_c = sm + prow_ref[0:1, :]
    li_r = smT + pcol_ref[:, 0:1]
    lf_c = -_softplus(-(sm + prow_ref[1:2, :]))
    lf_r = -_softplus(-(smT + pcol_ref[:, 1:2]))
    if T_valid < NC * L:
        vc = lax.broadcasted_iota(jnp.int32, (L, 1), 0) < T_valid
        vr = lax.broadcasted_iota(jnp.int32, (1, L), 1) < T_valid
        li_c = jnp.where(vc, li_c, NEG_BIG)
        li_r = jnp.where(vr, li_r, NEG_BIG)
        lf_c = jnp.where(vc, lf_c, 0.0)
        lf_r = jnp.where(vr, lf_r, 0.0)
    incl, _ = _tri_masks(L)
    F_c, F_r = _cumsum_both(lf_c, lf_r, incl)

    lane = lax.broadcasted_iota(jnp.int32, (1, K), 1)
    hlane = lax.broadcasted_iota(jnp.int32, (1, H), 1)
    n_row = n_ref[0]
    m_vec = m_ref[0]
    C_all = C_ref[0]
    n_new = n_row
    m_out = m_vec
    for h in range(H):
        in_head = (lane >= h * DK) & (lane < (h + 1) * DK)
        qh = jnp.where(in_head, q_all, 0.0)
        kh = jnp.where(in_head, k_all, 0.0)
        vh = v_ref[:, h * DV:(h + 1) * DV]
        Fc = F_c[:, 3 * H + h:3 * H + h + 1]
        Fr = F_r[3 * H + h:3 * H + h + 1, :]
        lic = li_c[:, 2 * H + h:2 * H + h + 1]
        lir = li_r[2 * H + h:2 * H + h + 1, :]
        Fl = Fc[L - 1:L, :]
        m = m_vec[:, h:h + 1]
        D = jnp.where(incl, Fc - Fr + lir, -jnp.inf)
        Dmax = jnp.max(D, axis=1, keepdims=True)
        inter = m + Fc
        m_row = jnp.maximum(Dmax, inter)
        Sm = jnp.exp(D - m_row) * _mm_nt(qh, k_all)
        a = jnp.exp(inter - m_row)
        num = a * _mm(qh, C_all) + _mm(Sm, vh)
        qn = jnp.sum(qh * n_row, axis=1, keepdims=True)
        den = a * qn + jnp.sum(Sm, axis=1, keepdims=True)
        hh = num / jnp.maximum(jnp.abs(den), jnp.exp(-m_row))
        Gr = lir + Fl - Fr
        Gc = lic + Fl - Fc
        Gmax = jnp.max(Gr, axis=1, keepdims=True)
        m_new = jnp.maximum(m + Fl, Gmax)
        dec = jnp.exp(m + Fl - m_new)
        kw = kh * jnp.exp(Gc - m_new)
        upd = _mm_tn(kw, vh)
        C_ref[0, h * DK:(h + 1) * DK, :] = dec * C_all[h * DK:(h + 1) * DK, :] + upd[h * DK:(h + 1) * DK, :]
        n_new = jnp.where(in_head, dec * n_row + jnp.sum(kw, axis=0, keepdims=True), n_new)
        m_out = jnp.where(hlane == h, m_new, m_out)
        hh = hh * lax.rsqrt(jnp.mean(hh * hh, axis=-1, keepdims=True) + EPS) * norm_ref[:, h * DV:(h + 1) * DV]
        oh = og_ref[:, h * DV:(h + 1) * DV]
        h_ref[:, h * DV:(h + 1) * DV] = hh * _sigmoid(oh)
    n_ref[0] = n_new
    m_ref[0] = m_out


def _mlstm(proj, smT, C0, n0, m0, prow, pcol, norm, *, B, NC, L, T_valid, offs):
    _, H, DK, DV = C0.shape
    K = H * DK
    V = H * DV
    M = proj.shape[0]
    kern = functools.partial(_mlstm_kernel, L=L, NC=NC, T_valid=T_valid, H=H, DK=DK, DV=DV)
    rowblk = lambda b, n: b * NC + n
    outs = pl.pallas_call(
        kern,
        grid=(B, NC),
        in_specs=[pl.BlockSpec((L, 2 * K), lambda b, n: (rowblk(b, n), offs["qk_m"] // (2 * K))),
                  pl.BlockSpec((L, V), lambda b, n: (rowblk(b, n), offs["v_m"] // V)),
                  pl.BlockSpec((L, V), lambda b, n: (rowblk(b, n), offs["o_m"] // V)),
                  pl.BlockSpec((L, LANES), lambda b, n: (rowblk(b, n), offs["small"] // LANES)),
                  pl.BlockSpec((1, GATE_ROWS, L), lambda b, n: (rowblk(b, n), 0, 0)),
                  pl.BlockSpec((1, K, DV), lambda b, n: (b, 0, 0)),
                  pl.BlockSpec((1, 1, K), lambda b, n: (b, 0, 0)),
                  pl.BlockSpec((1, 1, H), lambda b, n: (b, 0, 0)),
                  pl.BlockSpec((2, LANES), lambda b, n: (0, 0)),
                  pl.BlockSpec((GATE_ROWS, 2), lambda b, n: (0, 0)),
                  pl.BlockSpec((1, V), lambda b, n: (0, 0))],
        out_specs=[pl.BlockSpec((L, V), lambda b, n: (rowblk(b, n), 0)),
                   pl.BlockSpec((1, K, DV), lambda b, n: (b, 0, 0)),
                   pl.BlockSpec((1, 1, K), lambda b, n: (b, 0, 0)),
                   pl.BlockSpec((1, 1, H), lambda b, n: (b, 0, 0))],
        out_shape=[jax.ShapeDtypeStruct((M, V), F32),
                   jax.ShapeDtypeStruct((B, K, DV), F32),
                   jax.ShapeDtypeStruct((B, 1, K), F32),
                   jax.ShapeDtypeStruct((B, 1, H), F32)],
        compiler_params=pltpu.CompilerParams(dimension_semantics=("parallel", "arbitrary"),
                                             vmem_limit_bytes=VMEM_LIMIT),
        name="mlstm_chunk",
    )(proj, proj, proj, proj, smT, C0.reshape(B, K, DV), n0.reshape(B, 1, K), m0.reshape(B, 1, H),
      prow, pcol, norm)
    hm, C_new, n_new, m_new = outs
    return hm, C_new.reshape(B, H, DK, DV), n_new.reshape(B, H, DK), m_new.reshape(B, H)


def _merge_kernel(og_ref, hm_ref, gg_ref, gm_ref, x_ref, wbg_ref, wbm_ref, wout_ref, o_ref):
    br_g = jnp.dot(og_ref[...].astype(BF16), wbg_ref[...], preferred_element_type=F32)
    br_m = jnp.dot(hm_ref[...].astype(BF16), wbm_ref[...], preferred_element_type=F32)
    merged = _sigmoid(gg_ref[...]) * br_g + _sigmoid(gm_ref[...]) * br_m
    o_ref[...] = x_ref[...] + jnp.dot(merged.astype(BF16), wout_ref[...], preferred_element_type=F32)


def _merge(og, hm, proj, x, wbg, wbm, wout, *, offs):
    M, D = x.shape
    Vg = og.shape[1]
    Vm = hm.shape[1]
    tm = _tile(M, 512, SUBLANES)
    return pl.pallas_call(
        _merge_kernel,
        grid=(M // tm,),
        in_specs=[pl.BlockSpec((tm, Vg), lambda i: (i, 0)),
                  pl.BlockSpec((tm, Vm), lambda i: (i, 0)),
                  pl.BlockSpec((tm, D), lambda i: (i, offs["gate_g"] // D)),
                  pl.BlockSpec((tm, D), lambda i: (i, offs["gate_m"] // D)),
                  pl.BlockSpec((tm, D), lambda i: (i, 0)),
                  pl.BlockSpec((Vg, D), lambda i: (0, 0)),
                  pl.BlockSpec((Vm, D), lambda i: (0, 0)),
                  pl.BlockSpec((D, D), lambda i: (0, 0))],
        out_specs=pl.BlockSpec((tm, D), lambda i: (i, 0)),
        out_shape=jax.ShapeDtypeStruct((M, D), F32),
        compiler_params=pltpu.CompilerParams(dimension_semantics=("parallel",), vmem_limit_bytes=VMEM_LIMIT),
        name="merge_out",
    )(og, hm, proj, proj, x, wbg, wbm, wout)


def _mlp_kernel(x_ref, g_ref, wup_ref, wdn_ref, gf_ref, o_ref, xn_ref, acc_ref, *, final_norm):
    k = pl.program_id(1)

    @pl.when(k == 0)
    def _():
        x = x_ref[...]
        ms = jnp.mean(x * x, axis=-1, keepdims=True)
        xn_ref[...] = (x * lax.rsqrt(ms + EPS) * g_ref[...]).astype(BF16)
        acc_ref[...] = jnp.zeros_like(acc_ref)

    hcol = jnp.maximum(jnp.dot(xn_ref[...], wup_ref[...], preferred_element_type=F32), 0.0)
    acc_ref[...] += jnp.dot((hcol * hcol).astype(BF16), wdn_ref[...], preferred_element_type=F32)

    @pl.when(k == pl.num_programs(1) - 1)
    def _():
        y = x_ref[...] + acc_ref[...]
        if final_norm:
            ms = jnp.mean(y * y, axis=-1, keepdims=True)
            y = y * lax.rsqrt(ms + EPS) * gf_ref[...]
        o_ref[...] = y


def _mlp(x, gamma, wup, wdn, gamma_final, *, final_norm):
    M, D = x.shape
    FF = wup.shape[1]
    tm = _tile(M, 1024, SUBLANES)
    tf = _tile(FF, 1024, LANES)
    return pl.pallas_call(
        functools.partial(_mlp_kernel, final_norm=final_norm),
        grid=(M // tm, FF // tf),
        in_specs=[pl.BlockSpec((tm, D), lambda i, k: (i, 0)),
                  pl.BlockSpec((1, D), lambda i, k: (0, 0)),
                  pl.BlockSpec((D, tf), lambda i, k: (0, k)),
                  pl.BlockSpec((tf, D), lambda i, k: (k, 0)),
                  pl.BlockSpec((1, D), lambda i, k: (0, 0))],
        out_specs=pl.BlockSpec((tm, D), lambda i, k: (i, 0)),
        out_shape=jax.ShapeDtypeStruct((M, D), F32),
        scratch_shapes=[pltpu.VMEM((tm, D), BF16), pltpu.VMEM((tm, D), F32)],
        compiler_params=pltpu.CompilerParams(dimension_semantics=("parallel", "arbitrary"),
                                             vmem_limit_bytes=VMEM_LIMIT),
        name="mlp",
    )(x, gamma, wup, wdn, gamma_final)


def _pack_w_in(w_in, dims):
    D, Hg, DKg, DVg, Hm, DKm, DVm = dims
    Kg, Vg, Km, Vm = Hg * DKg, Hg * DVg, Hm * DKm, Hm * DVm
    sizes = (Kg, Kg, Vg, Vg, Hg, Hg, Km, Km, Vm, Vm, Hm, Hm, D, D)
    starts = [0]
    for s in sizes:
        starts.append(starts[-1] + s)
    col = lambda a, b: w_in[:, starts[a]:starts[b]]
    small = jnp.concatenate([col(4, 5), col(5, 6), col(10, 11), col(11, 12)], axis=1)
    assert 2 * Hg + 2 * Hm <= GATE_ROWS and Hg == Hm
    small = jnp.pad(small, ((0, 0), (0, LANES - small.shape[1])))
    segs = [("qkv", col(0, 3)), ("z", col(3, 4)), ("gate_g", col(12, 13)), ("gate_m", col(13, 14)),
            ("qk_m", col(6, 8)), ("v_m", col(8, 9)), ("o_m", col(9, 10)), ("small", small)]
    offs, off = {}, 0
    for name, seg in segs:
        assert off % seg.shape[1] == 0, (name, off, seg.shape)
        offs[name] = off
        off += seg.shape[1]
    return jnp.concatenate([s for _, s in segs], axis=1).astype(BF16), offs


def _gate_params(H, first, lane_first, second, lane_second):
    row = (jnp.zeros((2, LANES), F32).at[0, lane_first:lane_first + H].set(first.astype(F32))
           .at[1, lane_second:lane_second + H].set(second.astype(F32)))
    return row, row[:, :GATE_ROWS].T


def _head_norm_row(w, H, DV):
    return jnp.broadcast_to(w.astype(F32).reshape(-1, DV), (H, DV)).reshape(1, H * DV)


def _trunk(x, conv0, S0, C0, n0, m0, P, *, B, T_valid, L):
    M, D = x.shape
    NC = M // (B * L)
    depth = P["w_in"].shape[0]
    new = ([], [], [], [], [])
    for l in range(depth):
        proj = _norm_proj(x, P["norm_mix"][l], P["w_in"][l])
        smT = proj[:, P["offs"]["small"]:P["offs"]["small"] + GATE_ROWS].reshape(B * NC, L, GATE_ROWS)
        smT = jnp.swapaxes(smT, 1, 2)
        og, conv_n, S_n = _gdn(proj, smT, conv0[l], S0[l], P["conv_w"][l], P["gdn_prow"][l], P["gdn_pcol"][l],
                               P["gdn_norm"][l], B=B, NC=NC, L=L, T_valid=T_valid, offs=P["offs"])
        hm, C_n, n_n, m_n = _mlstm(proj, smT, C0[l], n0[l], m0[l], P["ml_prow"][l], P["ml_pcol"][l],
                                   P["ml_norm"][l], B=B, NC=NC, L=L, T_valid=T_valid, offs=P["offs"])
        x = _merge(og, hm, proj, x, P["w_bg"][l], P["w_bm"][l], P["w_out"][l], offs=P["offs"])
        x = _mlp(x, P["norm_mlp"][l], P["w_up"][l], P["w_down"][l], P["norm_final"],
                 final_norm=(l == depth - 1))
        for lst, s in zip(new, (conv_n, S_n, C_n, n_n, m_n)):
            lst.append(s)
    return (x,) + tuple(jnp.stack(lst) for lst in new)


def kernel(x_prompt, x_sample, state_gdn_conv, state_gdn_S, state_mlstm_C, state_mlstm_n, state_mlstm_m, norm_mix, w_in, gdn_conv_w, gdn_A_log, gdn_dt_bias, gdn_norm, ml_i_bias, ml_f_bias, ml_norm, w_branch_gdn, w_branch_ml, w_out, norm_mlp, w_up, w_down, norm_final):
    Bp, Tp, D = x_prompt.shape
    Bs, Ts, _ = x_sample.shape
    depth = w_in.shape[0]
    _, _, Hg, DKg, DVg = state_gdn_S.shape
    _, _, Hm, DKm, DVm = state_mlstm_C.shape
    CG = state_gdn_conv.shape[-1]
    dims = (D, Hg, DKg, DVg, Hm, DKm, DVm)
    assert Ts >= CONV_W - 1 and Tp >= CONV_W - 1

    packed = [_pack_w_in(w_in[l], dims) for l in range(depth)]
    P = {
        "offs": packed[0][1],
        "w_in": jnp.stack([p[0] for p in packed]),
        "norm_mix": norm_mix.reshape(depth, 1, D),
        "norm_mlp": norm_mlp.reshape(depth, 1, D),
        "norm_final": norm_final.reshape(1, D),
        "conv_w": gdn_conv_w,
        "w_bg": w_branch_gdn.astype(BF16),
        "w_bm": w_branch_ml.astype(BF16),
        "w_out": w_out.astype(BF16),
        "w_up": w_up.astype(BF16),
        "w_down": w_down.astype(BF16),
        "gdn_norm": jnp.stack([_head_norm_row(gdn_norm[l], Hg, DVg) for l in range(depth)]),
        "ml_norm": jnp.stack([_head_norm_row(ml_norm[l], Hm, DVm) for l in range(depth)]),
    }
    gp = [_gate_params(Hg, gdn_A_log[l], Hg, gdn_dt_bias[l], Hg) for l in range(depth)]
    mp = [_gate_params(Hm, ml_i_bias[l], 2 * Hg, ml_f_bias[l], 2 * Hg + Hm) for l in range(depth)]
    P["gdn_prow"] = jnp.stack([g[0] for g in gp])
    P["gdn_pcol"] = jnp.stack([g[1] for g in gp])
    P["ml_prow"] = jnp.stack([g[0] for g in mp])
    P["ml_pcol"] = jnp.stack([g[1] for g in mp])

    Lp = CHUNK if Tp % CHUNK == 0 else Tp
    assert Lp % SUBLANES == 0
    zeros = lambda *s: jnp.zeros(s, F32)
    yp, conv_p, S_p, C_p, n_p, m_p = _trunk(
        x_prompt.reshape(Bp * Tp, D), zeros(depth, Bp, CONV_W - 1, CG), zeros(depth, Bp, Hg, DKg, DVg),
        zeros(depth, Bp, Hm, DKm, DVm), zeros(depth, Bp, Hm, DKm), zeros(depth, Bp, Hm), P,
        B=Bp, T_valid=Tp, L=Lp)

    Ls = -(-Ts // SUBLANES) * SUBLANES
    xs = jnp.pad(x_sample, ((0, 0), (0, Ls - Ts), (0, 0))).reshape(Bs * Ls, D)
    ys, conv_s, S_s, C_s, n_s, m_s = _trunk(
        xs, state_gdn_conv, state_gdn_S, state_mlstm_C, state_mlstm_n, state_mlstm_m, P,
        B=Bs, T_valid=Ts, L=Ls)
    ys = ys.reshape(Bs, Ls, D)[:, :Ts]
    return (yp.reshape(Bp, Tp, D), ys, conv_p, S_p, C_p, n_p, m_p, conv_s, S_s, C_s, n_s, m_s)
```

```python
import functools

import jax
import jax.numpy as jnp
from jax import lax
from jax.experimental import pallas as pl
from jax.experimental.pallas import tpu as pltpu

F32 = jnp.float32
BF16 = jnp.bfloat16
EPS = 1e-6
CONV_W = 4
LANES = 128
SUBLANES = 8
GATE_ROWS = 16
CHUNK = 64
CHAINS = 16
NEG_BIG = -1e30
VMEM_LIMIT = 48 * 1024 * 1024
HIGHEST = lax.Precision.HIGHEST


def _sigmoid(x):
    return 1.0 / (1.0 + jnp.exp(-x))


def _softplus(x):
    return jnp.maximum(x, 0.0) + jnp.log(1.0 + jnp.exp(-jnp.abs(x)))


def _mm(a, b):
    return jnp.dot(a.astype(BF16), b.astype(BF16), preferred_element_type=F32)


def _mm_nt(a, b):
    return lax.dot_general(a.astype(BF16), b.astype(BF16), (((1,), (1,)), ((), ())), preferred_element_type=F32)


def _mm_tn(a, b):
    return lax.dot_general(a.astype(BF16), b.astype(BF16), (((0,), (0,)), ((), ())), preferred_element_type=F32)


def _split_bf16(a):
    hi = a.astype(BF16)
    return hi, (a - hi.astype(F32)).astype(BF16)


def _tile(n, cap, mult):
    best = None
    for t in range(mult, min(n, cap) + 1, mult):
        if n % t == 0:
            best = t
    return best if best is not None else n


def _tri_masks(L):
    row = lax.broadcasted_iota(jnp.int32, (L, L), 0)
    col = lax.broadcasted_iota(jnp.int32, (L, L), 1)
    return row >= col, row > col


def _cumsum_both(g_col, g_row, incl):
    L = incl.shape[0]
    row = lax.broadcasted_iota(jnp.int32, (L, L), 0)
    col = lax.broadcasted_iota(jnp.int32, (L, L), 1)
    lower = jnp.where(incl, 1.0, 0.0)
    upper = jnp.where(row <= col, 1.0, 0.0)
    c_col = jnp.dot(lower, g_col, precision=HIGHEST, preferred_element_type=F32)
    c_row = jnp.dot(g_row, upper, precision=HIGHEST, preferred_element_type=F32)
    return c_col, c_row


def _tri_inv_unit_lower(As):
    L = As[0].shape[0]
    row = lax.broadcasted_iota(jnp.int32, (L, L), 0)
    col = lax.broadcasted_iota(jnp.int32, (L, L), 1)
    eye = jnp.where(row == col, 1.0, 0.0)
    levels = max(1, (L - 1).bit_length())
    Xs = [eye - A for A in As]
    if levels > 1:
        Ps = [_mm(A, A) for A in As]
        for _ in range(1, levels - 1):
            Rs = [_mm(jnp.concatenate([P, X], axis=0), P) for P, X in zip(Ps, Xs)]
            Ps = [R[:L] for R in Rs]
            Xs = [X + R[L:] for X, R in zip(Xs, Rs)]
        Xs = [X + _mm(X, P) for X, P in zip(Xs, Ps)]
    splits = [(_split_bf16(A), _split_bf16(X)) for A, X in zip(As, Xs)]
    AX1 = [jnp.dot(jnp.concatenate([a_hi, a_lo], axis=0), x_hi, preferred_element_type=F32)
           for (a_hi, a_lo), (x_hi, _) in splits]
    AX2 = [jnp.dot(a_hi, x_lo, preferred_element_type=F32) for (a_hi, _), (_, x_lo) in splits]
    Rs = [(eye - X) - (r1[:L] + (r1[L:] + r2)) for X, r1, r2 in zip(Xs, AX1, AX2)]
    return Xs, Rs


def _norm_proj_kernel(x_ref, g_ref, w_ref, o_ref, xn_ref):
    @pl.when(pl.program_id(1) == 0)
    def _():
        x = x_ref[...]
        ms = jnp.mean(x * x, axis=-1, keepdims=True)
        xn_ref[...] = (x * lax.rsqrt(ms + EPS) * g_ref[...]).astype(BF16)

    o_ref[...] = jnp.dot(xn_ref[...], w_ref[...], preferred_element_type=F32)


def _norm_proj(x, gamma, w):
    M, D = x.shape
    N = w.shape[1]
    tm = _tile(M, 1024, SUBLANES)
    tn = _tile(N, 1280, LANES)
    return pl.pallas_call(
        _norm_proj_kernel,
        grid=(M // tm, N // tn),
        in_specs=[pl.BlockSpec((tm, D), lambda i, j: (i, 0)),
                  pl.BlockSpec((1, D), lambda i, j: (0, 0)),
                  pl.BlockSpec((D, tn), lambda i, j: (0, j))],
        out_specs=pl.BlockSpec((tm, tn), lambda i, j: (i, j)),
        out_shape=jax.ShapeDtypeStruct((M, N), F32),
        scratch_shapes=[pltpu.VMEM((tm, D), BF16)],
        compiler_params=pltpu.CompilerParams(dimension_semantics=("parallel", "arbitrary"),
                                             vmem_limit_bytes=VMEM_LIMIT),
        name="norm_proj",
    )(x, gamma, w)


def _gdn_kernel(qkv_ref, z_ref, sm_ref, smT_ref, conv0_ref, S0_ref, cw_ref, prow_ref, pcol_ref, gnorm_ref,
                o_ref, convn_ref, S_ref, xp_ref, *, nb, L, NC, T_valid, H, DK, DV):
    n = pl.program_id(1)
    K = H * DK
    pad0 = SUBLANES - (CONV_W - 1)

    @pl.when(n == 0)
    def _():
        xp_ref[:, pad0:SUBLANES, :] = conv0_ref[...]
        S_ref[...] = S0_ref[...]

    incl, strict = _tri_masks(L)
    lv = L if NC > 1 else T_valid
    chains = []
    for b in range(nb):
        xp_ref[b, SUBLANES:SUBLANES + L, :] = qkv_ref[b]
        y = xp_ref[b, pad0:pad0 + L, :] * cw_ref[0:1, :]
        for j in range(1, CONV_W):
            y = y + xp_ref[b, pad0 + j:pad0 + j + L, :] * cw_ref[j:j + 1, :]
        qkv = y * _sigmoid(y)
        tail = xp_ref[b, SUBLANES + lv - (CONV_W - 1):SUBLANES + lv, :]
        xp_ref[b, pad0:SUBLANES, :] = tail
        convn_ref[b] = tail

        sm = sm_ref[b]
        smT = smT_ref[b, 0]
        beta_c = _sigmoid(sm)
        g_c = -jnp.exp(prow_ref[0:1, :]) * _softplus(sm + prow_ref[1:2, :])
        g_r = -jnp.exp(pcol_ref[:, 0:1]) * _softplus(smT + pcol_ref[:, 1:2])
        if T_valid < NC * L:
            vc = lax.broadcasted_iota(jnp.int32, (L, 1), 0) < T_valid
            vr = lax.broadcasted_iota(jnp.int32, (1, L), 1) < T_valid
            beta_c = jnp.where(vc, beta_c, 0.0)
            g_c = jnp.where(vc, g_c, 0.0)
            g_r = jnp.where(vr, g_r, 0.0)
        gc_c, gc_r = _cumsum_both(g_c, g_r, incl)
        for h in range(H):
            qh = qkv[:, h * DK:(h + 1) * DK]
            kh = qkv[:, K + h * DK:K + (h + 1) * DK]
            vh = qkv[:, 2 * K + h * DV:2 * K + (h + 1) * DV]
            qh = qh * lax.rsqrt(jnp.sum(qh * qh, axis=-1, keepdims=True) + EPS) * (DK ** -0.5)
            kh = kh * lax.rsqrt(jnp.sum(kh * kh, axis=-1, keepdims=True) + EPS)
            b_c = beta_c[:, h:h + 1]
            gcc = gc_c[:, H + h:H + h + 1]
            gcr = gc_r[H + h:H + h + 1, :]
            gl = gcc[L - 1:L, :]
            eg = jnp.exp(gcc)
            kb = kh * b_c
            chains.append(dict(
                b=b, h=h, kh=kh.astype(BF16), kbq=jnp.concatenate([kb, qh], axis=0).astype(BF16),
                decay=jnp.exp(jnp.where(incl, gcc - gcr, -jnp.inf)),
                rhs=jnp.concatenate([kb * eg, vh * b_c], axis=1), qd=qh * eg,
                kd=(kh * jnp.exp(gl - gcc)).astype(BF16), dl=jnp.exp(gl)))

    for c in chains:
        r = _mm_nt(c["kbq"], c["kh"])
        c["A"] = jnp.where(strict, r[:L] * c["decay"], 0.0)
        c["qk"] = (r[L:] * c["decay"]).astype(BF16)
    Xs, Rs = _tri_inv_unit_lower([c["A"] for c in chains])
    corr = [_mm(R, c["rhs"]) for R, c in zip(Rs, chains)]
    wu = [_mm(X, c["rhs"] + cr) for X, c, cr in zip(Xs, chains, corr)]
    states = [S_ref[c["b"], c["h"]] for c in chains]
    wq = [_mm(jnp.concatenate([x[:, :DK], c["qd"]], axis=0), S) for x, c, S in zip(wu, chains, states)]
    v_new = [(x[:, DK:] - y[:L]).astype(BF16) for x, y in zip(wu, wq)]
    o_intra = [_mm(c["qk"], v) for c, v in zip(chains, v_new)]
    s_upd = [_mm_tn(c["kd"], v) for c, v in zip(chains, v_new)]
    for c, S, y, oi, su in zip(chains, states, wq, o_intra, s_upd):
        b, h = c["b"], c["h"]
        S_ref[b, h] = S * c["dl"] + su
        o = y[L:] + oi
        o = o * lax.rsqrt(jnp.mean(o * o, axis=-1, keepdims=True) + EPS) * gnorm_ref[:, h * DV:(h + 1) * DV]
        zh = z_ref[b, :, h * DV:(h + 1) * DV]
        o_ref[b, :, h * DV:(h + 1) * DV] = o * (zh * _sigmoid(zh))


def _gdn(proj, smT, conv0, S0, cw, prow, pcol, gnorm, *, NC, L, T_valid, offs):
    B, H, DK, DV = S0.shape
    CG = conv0.shape[-1]
    V = H * DV
    T_pad = proj.shape[1]
    nb = _tile(B, max(1, CHAINS // H), 1)
    kern = functools.partial(_gdn_kernel, nb=nb, L=L, NC=NC, T_valid=T_valid, H=H, DK=DK, DV=DV)
    return pl.pallas_call(
        kern,
        grid=(B // nb, NC),
        in_specs=[pl.BlockSpec((nb, L, CG), lambda i, n: (i, n, offs["qkv"] // CG)),
                  pl.BlockSpec((nb, L, V), lambda i, n: (i, n, offs["z"] // V)),
                  pl.BlockSpec((nb, L, LANES), lambda i, n: (i, n, offs["small"] // LANES)),
                  pl.BlockSpec((nb, 1, GATE_ROWS, L), lambda i, n: (i, n, 0, 0)),
                  pl.BlockSpec((nb, CONV_W - 1, CG), lambda i, n: (i, 0, 0)),
                  pl.BlockSpec((nb, H, DK, DV), lambda i, n: (i, 0, 0, 0)),
                  pl.BlockSpec((CONV_W, CG), lambda i, n: (0, 0)),
                  pl.BlockSpec((2, LANES), lambda i, n: (0, 0)),
                  pl.BlockSpec((GATE_ROWS, 2), lambda i, n: (0, 0)),
                  pl.BlockSpec((1, V), lambda i, n: (0, 0))],
        out_specs=[pl.BlockSpec((nb, L, V), lambda i, n: (i, n, 0)),
                   pl.BlockSpec((nb, CONV_W - 1, CG), lambda i, n: (i, 0, 0)),
                   pl.BlockSpec((nb, H, DK, DV), lambda i, n: (i, 0, 0, 0))],
        out_shape=[jax.ShapeDtypeStruct((B, T_pad, V), F32),
                   jax.ShapeDtypeStruct(conv0.shape, F32),
                   jax.ShapeDtypeStruct(S0.shape, F32)],
        scratch_shapes=[pltpu.VMEM((nb, SUBLANES + L, CG), F32)],
        compiler_params=pltpu.CompilerParams(dimension_semantics=("parallel", "arbitrary"),
                                             vmem_limit_bytes=VMEM_LIMIT),
        name="gdn_chunk",
    )(proj, proj, proj, smT, conv0, S0, cw, prow, pcol, gnorm)


def _mlstm_kernel(qk_ref, v_ref, og_ref, sm_ref, smT_ref, C0_ref, n0_ref, m0_ref, prow_ref, pcol_ref, norm_ref,
                  h_ref, C_ref, n_ref, m_ref, *, nb, L, NC, T_valid, H, DK, DV):
    step = pl.program_id(1)
    K = H * DK

    @pl.when(step == 0)
    def _():
        C_ref[...] = C0_ref[...]
        n_ref[...] = n0_ref[...]
        m_ref[...] = m0_ref[...]

    incl, _ = _tri_masks(L)
    lane = lax.broadcasted_iota(jnp.int32, (1, K), 1)
    hlane = lax.broadcasted_iota(jnp.int32, (1, H), 1)
    chains = []
    per_seq = []
    for b in range(nb):
        qk_in = qk_ref[b]
        q_all = qk_in[:, :K] * (DK ** -0.5)
        k_all = qk_in[:, K:]
        sm = sm_ref[b]
        smT = smT_ref[b, 0]
        li_c = sm + prow_ref[0:1, :]
        li_r = smT + pcol_ref[:, 0:1]
        lf_c = -_softplus(-(sm + prow_ref[1:2, :]))
        lf_r = -_softplus(-(smT + pcol_ref[:, 1:2]))
        if T_valid < NC * L:
            vc = lax.broadcasted_iota(jnp.int32, (L, 1), 0) < T_valid
            vr = lax.broadcasted_iota(jnp.int32, (1, L), 1) < T_valid
            li_c = jnp.where(vc, li_c, NEG_BIG)
            li_r = jnp.where(vr, li_r, NEG_BIG)
            lf_c = jnp.where(vc, lf_c, 0.0)
            lf_r = jnp.where(vr, lf_r, 0.0)
        F_c, F_r = _cumsum_both(lf_c, lf_r, incl)
        n_row = n_ref[b]
        m_vec = m_ref[b]
        C_all = C_ref[b]
        seq = dict(k_bf=k_all.astype(BF16), C_bf=C_all.astype(BF16), n_row=n_row, n_new=n_row, m_out=m_vec)
        per_seq.append(seq)
        for h in range(H):
            in_head = (lane >= h * DK) & (lane < (h + 1) * DK)
            qh = jnp.where(in_head, q_all, 0.0)
            kh = jnp.where(in_head, k_all, 0.0)
            Fc = F_c[:, 3 * H + h:3 * H + h + 1]
            Fr = F_r[3 * H + h:3 * H + h + 1, :]
            lic = li_c[:, 2 * H + h:2 * H + h + 1]
            lir = li_r[2 * H + h:2 * H + h + 1, :]
            Fl = Fc[L - 1:L, :]
            m = m_vec[:, h:h + 1]
            D = jnp.where(incl, Fc - Fr + lir, -jnp.inf)
            inter = m + Fc
            m_row = jnp.maximum(jnp.max(D, axis=1, keepdims=True), inter)
            m_new = jnp.maximum(m + Fl, jnp.max(lir + Fl - Fr, axis=1, keepdims=True))
            dec = jnp.exp(m + Fl - m_new)
            wC = jnp.exp(lic + Fl - Fc - m_new)
            seq["n_new"] = jnp.where(in_head, dec * n_row + jnp.sum(kh * wC, axis=0, keepdims=True),
                                     seq["n_new"])
            seq["m_out"] = jnp.where(hlane == h, m_new, seq["m_out"])
            chains.append(dict(
                b=b, h=h, seq=seq, q_bf=qh.astype(BF16), v_bf=v_ref[b, :, h * DV:(h + 1) * DV].astype(BF16),
                kw=(k_all[:, h * DK:(h + 1) * DK] * wC).astype(BF16),
                expD=jnp.exp(D - m_row), a=jnp.exp(inter - m_row), floor=jnp.exp(-m_row),
                qn=jnp.sum(qh * n_row, axis=1, keepdims=True), dec=dec,
                C_h=C_all[h * DK:(h + 1) * DK, :]))

    qk = [_mm_nt(c["q_bf"], c["seq"]["k_bf"]) for c in chains]
    Sm = [c["expD"] * x for c, x in zip(chains, qk)]
    inter_state = [_mm(c["q_bf"], c["seq"]["C_bf"]) for c in chains]
    intra = [_mm(s, c["v_bf"]) for s, c in zip(Sm, chains)]
    upd = [_mm_tn(c["kw"], c["v_bf"]) for c in chains]
    for c, s, ns, ni, u in zip(chains, Sm, inter_state, intra, upd):
        b, h = c["b"], c["h"]
        C_ref[b, h * DK:(h + 1) * DK, :] = c["dec"] * c["C_h"] + u
        den = c["a"] * c["qn"] + jnp.sum(s, axis=1, keepdims=True)
        hh = (c["a"] * ns + ni) / jnp.maximum(jnp.abs(den), c["floor"])
        hh = hh * lax.rsqrt(jnp.mean(hh * hh, axis=-1, keepdims=True) + EPS) * norm_ref[:, h * DV:(h + 1) * DV]
        oh = og_ref[b, :, h * DV:(h + 1) * DV]
        h_ref[b, :, h * DV:(h + 1) * DV] = hh * _sigmoid(oh)
    for b, seq in enumerate(per_seq):
        n_ref[b] = seq["n_new"]
        m_ref[b] = seq["m_out"]


def _mlstm(proj, smT, C0, n0, m0, prow, pcol, norm, *, NC, L, T_valid, offs):
    B, H, DK, DV = C0.shape
    K = H * DK
    V = H * DV
    T_pad = proj.shape[1]
    nb = _tile(B, max(1, CHAINS // H), 1)
    kern = functools.partial(_mlstm_kernel, nb=nb, L=L, NC=NC, T_valid=T_valid, H=H, DK=DK, DV=DV)
    outs = pl.pallas_call(
        kern,
        grid=(B // nb, NC),
        in_specs=[pl.BlockSpec((nb, L, 2 * K), lambda i, n: (i, n, offs["qk_m"] // (2 * K))),
                  pl.BlockSpec((nb, L, V), lambda i, n: (i, n, offs["v_m"] // V)),
                  pl.BlockSpec((nb, L, V), lambda i, n: (i, n, offs["o_m"] // V)),
                  pl.BlockSpec((nb, L, LANES), lambda i, n: (i, n, offs["small"] // LANES)),
                  pl.BlockSpec((nb, 1, GATE_ROWS, L), lambda i, n: (i, n, 0, 0)),
                  pl.BlockSpec((nb, K, DV), lambda i, n: (i, 0, 0)),
                  pl.BlockSpec((nb, 1, K), lambda i, n: (i, 0, 0)),
                  pl.BlockSpec((nb, 1, H), lambda i, n: (i, 0, 0)),
                  pl.BlockSpec((2, LANES), lambda i, n: (0, 0)),
                  pl.BlockSpec((GATE_ROWS, 2), lambda i, n: (0, 0)),
                  pl.BlockSpec((1, V), lambda i, n: (0, 0))],
        out_specs=[pl.BlockSpec((nb, L, V), lambda i, n: (i, n, 0)),
                   pl.BlockSpec((nb, K, DV), lambda i, n: (i, 0, 0)),
                   pl.BlockSpec((nb, 1, K), lambda i, n: (i, 0, 0)),
                   pl.BlockSpec((nb, 1, H), lambda i, n: (i, 0, 0))],
        out_shape=[jax.ShapeDtypeStruct((B, T_pad, V), F32),
                   jax.ShapeDtypeStruct((B, K, DV), F32),
                   jax.ShapeDtypeStruct((B, 1, K), F32),
                   jax.ShapeDtypeStruct((B, 1, H), F32)],
        compiler_params=pltpu.CompilerParams(dimension_semantics=("parallel", "arbitrary"),
                                             vmem_limit_bytes=VMEM_LIMIT),
        name="mlstm_chunk",
    )(proj, proj, proj, proj, smT, C0.reshape(B, K, DV), n0.reshape(B, 1, K), m0.reshape(B, 1, H),
      prow, pcol, norm)
    hm, C_new, n_new, m_new = outs
    return hm, C_new.reshape(B, H, DK, DV), n_new.reshape(B, H, DK), m_new.reshape(B, H)


def _merge_kernel(og_ref, hm_ref, gg_ref, gm_ref, x_ref, wbg_ref, wbm_ref, wout_ref, o_ref):
    br_g = jnp.dot(og_ref[...].astype(BF16), wbg_ref[...], preferred_element_type=F32)
    br_m = jnp.dot(hm_ref[...].astype(BF16), wbm_ref[...], preferred_element_type=F32)
    merged = _sigmoid(gg_ref[...]) * br_g + _sigmoid(gm_ref[...]) * br_m
    o_ref[...] = x_ref[...] + jnp.dot(merged.astype(BF16), wout_ref[...], preferred_element_type=F32)


def _merge(og, hm, proj, x, wbg, wbm, wout, *, offs):
    M, D = x.shape
    Vg = og.shape[1]
    Vm = hm.shape[1]
    tm = _tile(M, 512, SUBLANES)
    return pl.pallas_call(
        _merge_kernel,
        grid=(M // tm,),
        in_specs=[pl.BlockSpec((tm, Vg), lambda i: (i, 0)),
                  pl.BlockSpec((tm, Vm), lambda i: (i, 0)),
                  pl.BlockSpec((tm, D), lambda i: (i, offs["gate_g"] // D)),
                  pl.BlockSpec((tm, D), lambda i: (i, offs["gate_m"] // D)),
                  pl.BlockSpec((tm, D), lambda i: (i, 0)),
                  pl.BlockSpec((Vg, D), lambda i: (0, 0)),
                  pl.BlockSpec((Vm, D), lambda i: (0, 0)),
                  pl.BlockSpec((D, D), lambda i: (0, 0))],
        out_specs=pl.BlockSpec((tm, D), lambda i: (i, 0)),
        out_shape=jax.ShapeDtypeStruct((M, D), F32),
        compiler_params=pltpu.CompilerParams(dimension_semantics=("parallel",), vmem_limit_bytes=VMEM_LIMIT),
        name="merge_out",
    )(og, hm, proj, proj, x, wbg, wbm, wout)


def _mlp_kernel(x_ref, g_ref, wup_ref, wdn_ref, gf_ref, o_ref, xn_ref, acc_ref, *, final_norm):
    k = pl.program_id(1)

    @pl.when(k == 0)
    def _():
        x = x_ref[...]
        ms = jnp.mean(x * x, axis=-1, keepdims=True)
        xn_ref[...] = (x * lax.rsqrt(ms + EPS) * g_ref[...]).astype(BF16)
        acc_ref[...] = jnp.zeros_like(acc_ref)

    hcol = jnp.maximum(jnp.dot(xn_ref[...], wup_ref[...], preferred_element_type=F32), 0.0)
    acc_ref[...] += jnp.dot((hcol * hcol).astype(BF16), wdn_ref[...], preferred_element_type=F32)

    @pl.when(k == pl.num_programs(1) - 1)
    def _():
        y = x_ref[...] + acc_ref[...]
        if final_norm:
            ms = jnp.mean(y * y, axis=-1, keepdims=True)
            y = y * lax.rsqrt(ms + EPS) * gf_ref[...]
        o_ref[...] = y


def _mlp(x, gamma, wup, wdn, gamma_final, *, final_norm):
    M, D = x.shape
    FF = wup.shape[1]
    tm = _tile(M, 1024, SUBLANES)
    tf = _tile(FF, 1024, LANES)
    return pl.pallas_call(
        functools.partial(_mlp_kernel, final_norm=final_norm),
        grid=(M // tm, FF // tf),
        in_specs=[pl.BlockSpec((tm, D), lambda i, k: (i, 0)),
                  pl.BlockSpec((1, D), lambda i, k: (0, 0)),
                  pl.BlockSpec((D, tf), lambda i, k: (0, k)),
                  pl.BlockSpec((tf, D), lambda i, k: (k, 0)),
                  pl.BlockSpec((1, D), lambda i, k: (0, 0))],
        out_specs=pl.BlockSpec((tm, D), lambda i, k: (i, 0)),
        out_shape=jax.ShapeDtypeStruct((M, D), F32),
        scratch_shapes=[pltpu.VMEM((tm, D), BF16), pltpu.VMEM((tm, D), F32)],
        compiler_params=pltpu.CompilerParams(dimension_semantics=("parallel", "arbitrary"),
                                             vmem_limit_bytes=VMEM_LIMIT),
        name="mlp",
    )(x, gamma, wup, wdn, gamma_final)


def _pack_w_in(w_in, dims):
    D, Hg, DKg, DVg, Hm, DKm, DVm = dims
    Kg, Vg, Km, Vm = Hg * DKg, Hg * DVg, Hm * DKm, Hm * DVm
    sizes = (Kg, Kg, Vg, Vg, Hg, Hg, Km, Km, Vm, Vm, Hm, Hm, D, D)
    starts = [0]
    for s in sizes:
        starts.append(starts[-1] + s)
    col = lambda a, b: w_in[:, starts[a]:starts[b]]
    small = jnp.concatenate([col(4, 5), col(5, 6), col(10, 11), col(11, 12)], axis=1)
    assert 2 * Hg + 2 * Hm <= GATE_ROWS and Hg == Hm
    small = jnp.pad(small, ((0, 0), (0, LANES - small.shape[1])))
    segs = [("qkv", col(0, 3)), ("z", col(3, 4)), ("gate_g", col(12, 13)), ("gate_m", col(13, 14)),
            ("qk_m", col(6, 8)), ("v_m", col(8, 9)), ("o_m", col(9, 10)), ("small", small)]
    offs, off = {}, 0
    for name, seg in segs:
        assert off % seg.shape[1] == 0, (name, off, seg.shape)
        offs[name] = off
        off += seg.shape[1]
    return jnp.concatenate([s for _, s in segs], axis=1).astype(BF16), offs


def _gate_params(H, first, lane_first, second, lane_second):
    row = (jnp.zeros((2, LANES), F32).at[0, lane_first:lane_first + H].set(first.astype(F32))
           .at[1, lane_second:lane_second + H].set(second.astype(F32)))
    return row, row[:, :GATE_ROWS].T


def _head_norm_row(w, H, DV):
    return jnp.broadcast_to(w.astype(F32).reshape(-1, DV), (H, DV)).reshape(1, H * DV)


def _trunk(x, conv0, S0, C0, n0, m0, P, *, B, T_valid, L):
    M, D = x.shape
    T_pad = M // B
    NC = T_pad // L
    depth = P["w_in"].shape[0]
    offs = P["offs"]
    new = ([], [], [], [], [])
    for l in range(depth):
        proj = _norm_proj(x, P["norm_mix"][l], P["w_in"][l])
        proj3 = proj.reshape(B, T_pad, proj.shape[1])
        smT = proj3[:, :, offs["small"]:offs["small"] + GATE_ROWS].reshape(B, NC, L, GATE_ROWS)
        smT = jnp.swapaxes(smT, 2, 3)
        og, conv_n, S_n = _gdn(proj3, smT, conv0[l], S0[l], P["conv_w"][l], P["gdn_prow"][l], P["gdn_pcol"][l],
                               P["gdn_norm"][l], NC=NC, L=L, T_valid=T_valid, offs=offs)
        hm, C_n, n_n, m_n = _mlstm(proj3, smT, C0[l], n0[l], m0[l], P["ml_prow"][l], P["ml_pcol"][l],
                                   P["ml_norm"][l], NC=NC, L=L, T_valid=T_valid, offs=offs)
        x = _merge(og.reshape(M, -1), hm.reshape(M, -1), proj, x, P["w_bg"][l], P["w_bm"][l], P["w_out"][l],
                   offs=offs)
        x = _mlp(x, P["norm_mlp"][l], P["w_up"][l], P["w_down"][l], P["norm_final"],
                 final_norm=(l == depth - 1))
        for lst, s in zip(new, (conv_n, S_n, C_n, n_n, m_n)):
            lst.append(s)
    return (x,) + tuple(jnp.stack(lst) for lst in new)


def kernel(x_prompt, x_sample, state_gdn_conv, state_gdn_S, state_mlstm_C, state_mlstm_n, state_mlstm_m, norm_mix, w_in, gdn_conv_w, gdn_A_log, gdn_dt_bias, gdn_norm, ml_i_bias, ml_f_bias, ml_norm, w_branch_gdn, w_branch_ml, w_out, norm_mlp, w_up, w_down, norm_final):
    Bp, Tp, D = x_prompt.shape
    Bs, Ts, _ = x_sample.shape
    depth = w_in.shape[0]
    _, _, Hg, DKg, DVg = state_gdn_S.shape
    _, _, Hm, DKm, DVm = state_mlstm_C.shape
    CG = state_gdn_conv.shape[-1]
    dims = (D, Hg, DKg, DVg, Hm, DKm, DVm)
    assert Ts >= CONV_W - 1 and Tp >= CONV_W - 1

    packed = [_pack_w_in(w_in[l], dims) for l in range(depth)]
    P = {
        "offs": packed[0][1],
        "w_in": jnp.stack([p[0] for p in packed]),
        "norm_mix": norm_mix.reshape(depth, 1, D),
        "norm_mlp": norm_mlp.reshape(depth, 1, D),
        "norm_final": norm_final.reshape(1, D),
        "conv_w": gdn_conv_w,
        "w_bg": w_branch_gdn.astype(BF16),
        "w_bm": w_branch_ml.astype(BF16),
        "w_out": w_out.astype(BF16),
        "w_up": w_up.astype(BF16),
        "w_down": w_down.astype(BF16),
        "gdn_norm": jnp.stack([_head_norm_row(gdn_norm[l], Hg, DVg) for l in range(depth)]),
        "ml_norm": jnp.stack([_head_norm_row(ml_norm[l], Hm, DVm) for l in range(depth)]),
    }
    gp = [_gate_params(Hg, gdn_A_log[l], Hg, gdn_dt_bias[l], Hg) for l in range(depth)]
    mp = [_gate_params(Hm, ml_i_bias[l], 2 * Hg, ml_f_bias[l], 2 * Hg + Hm) for l in range(depth)]
    P["gdn_prow"] = jnp.stack([g[0] for g in gp])
    P["gdn_pcol"] = jnp.stack([g[1] for g in gp])
    P["ml_prow"] = jnp.stack([g[0] for g in mp])
    P["ml_pcol"] = jnp.stack([g[1] for g in mp])

    Lp = CHUNK if Tp % CHUNK == 0 else Tp
    assert Lp % SUBLANES == 0
    zeros = lambda *s: jnp.zeros(s, F32)
    yp, conv_p, S_p, C_p, n_p, m_p = _trunk(
        x_prompt.reshape(Bp * Tp, D), zeros(depth, Bp, CONV_W - 1, CG), zeros(depth, Bp, Hg, DKg, DVg),
        zeros(depth, Bp, Hm, DKm, DVm), zeros(depth, Bp, Hm, DKm), zeros(depth, Bp, Hm), P,
        B=Bp, T_valid=Tp, L=Lp)

    Ls = -(-Ts // SUBLANES) * SUBLANES
    xs = jnp.pad(x_sample, ((0, 0), (0, Ls - Ts), (0, 0))).reshape(Bs * Ls, D)
    ys, conv_s, S_s, C_s, n_s, m_s = _trunk(
        xs, state_gdn_conv, state_gdn_S, state_mlstm_C, state_mlstm_n, state_mlstm_m, P,
        B=Bs, T_valid=Ts, L=Ls)
    ys = ys.reshape(Bs, Ls, D)[:, :Ts]
    return (yp.reshape(Bp, Tp, D), ys, conv_p, S_p, C_p, n_p, m_p, conv_s, S_s, C_s, n_s, m_s)
```

```python
import functools

import jax
import jax.numpy as jnp
from jax import lax
from jax.experimental import pallas as pl
from jax.experimental.pallas import tpu as pltpu

F32 = jnp.float32
BF16 = jnp.bfloat16
EPS = 1e-6
CONV_W = 4
LANES = 128
SUBLANES = 8
GATE_ROWS = 16
CHUNK = 64
CHAINS = 16
NEG_BIG = -1e30
VMEM_LIMIT = 48 * 1024 * 1024
HIGHEST = lax.Precision.HIGHEST


def _sigmoid(x):
    return 1.0 / (1.0 + jnp.exp(-x))


def _softplus(x):
    return jnp.maximum(x, 0.0) + jnp.log(1.0 + jnp.exp(-jnp.abs(x)))


def _mm(a, b):
    return jnp.dot(a.astype(BF16), b.astype(BF16), preferred_element_type=F32)


def _mm_nt(a, b):
    return lax.dot_general(a.astype(BF16), b.astype(BF16), (((1,), (1,)), ((), ())), preferred_element_type=F32)


def _mm_tn(a, b):
    return lax.dot_general(a.astype(BF16), b.astype(BF16), (((0,), (0,)), ((), ())), preferred_element_type=F32)


def _split_bf16(a):
    hi = a.astype(BF16)
    return hi, (a - hi.astype(F32)).astype(BF16)


def _tile(n, cap, mult):
    best = None
    for t in range(mult, min(n, cap) + 1, mult):
        if n % t == 0:
            best = t
    return best if best is not None else n


def _tri_masks(L):
    row = lax.broadcasted_iota(jnp.int32, (L, L), 0)
    col = lax.broadcasted_iota(jnp.int32, (L, L), 1)
    return row >= col, row > col


def _cumsum_both(g_col, g_row, incl):
    L = incl.shape[0]
    row = lax.broadcasted_iota(jnp.int32, (L, L), 0)
    col = lax.broadcasted_iota(jnp.int32, (L, L), 1)
    lower = jnp.where(incl, 1.0, 0.0)
    upper = jnp.where(row <= col, 1.0, 0.0)
    c_col = jnp.dot(lower, g_col, precision=HIGHEST, preferred_element_type=F32)
    c_row = jnp.dot(g_row, upper, precision=HIGHEST, preferred_element_type=F32)
    return c_col, c_row


def _tri_inv_unit_lower(As):
    L = As[0].shape[0]
    row = lax.broadcasted_iota(jnp.int32, (L, L), 0)
    col = lax.broadcasted_iota(jnp.int32, (L, L), 1)
    eye = jnp.where(row == col, 1.0, 0.0)
    levels = max(1, (L - 1).bit_length())
    Xs = [eye - A for A in As]
    if levels > 1:
        Ps = [_mm(A, A) for A in As]
        for _ in range(1, levels - 1):
            Rs = [_mm(jnp.concatenate([P, X], axis=0), P) for P, X in zip(Ps, Xs)]
            Ps = [R[:L] for R in Rs]
            Xs = [X + R[L:] for X, R in zip(Xs, Rs)]
        Xs = [X + _mm(X, P) for X, P in zip(Xs, Ps)]
    splits = [(_split_bf16(A), _split_bf16(X)) for A, X in zip(As, Xs)]
    AX1 = [jnp.dot(jnp.concatenate([a_hi, a_lo], axis=0), x_hi, preferred_element_type=F32)
           for (a_hi, a_lo), (x_hi, _) in splits]
    AX2 = [jnp.dot(a_hi, x_lo, preferred_element_type=F32) for (a_hi, _), (_, x_lo) in splits]
    Rs = [(eye - X) - (r1[:L] + (r1[L:] + r2)) for X, r1, r2 in zip(Xs, AX1, AX2)]
    return Xs, Rs


def _norm_proj_kernel(x_ref, g_ref, w_ref, o_ref, xn_ref):
    @pl.when(pl.program_id(1) == 0)
    def _():
        x = x_ref[...]
        ms = jnp.mean(x * x, axis=-1, keepdims=True)
        xn_ref[...] = (x * lax.rsqrt(ms + EPS) * g_ref[...]).astype(BF16)

    o_ref[...] = jnp.dot(xn_ref[...], w_ref[...], preferred_element_type=F32)


def _norm_proj(x, gamma, w):
    M, D = x.shape
    N = w.shape[1]
    tm = _tile(M, 1024, SUBLANES)
    tn = _tile(N, 1280, LANES)
    return pl.pallas_call(
        _norm_proj_kernel,
        grid=(M // tm, N // tn),
        in_specs=[pl.BlockSpec((tm, D), lambda i, j: (i, 0)),
                  pl.BlockSpec((1, D), lambda i, j: (0, 0)),
                  pl.BlockSpec((D, tn), lambda i, j: (0, j))],
        out_specs=pl.BlockSpec((tm, tn), lambda i, j: (i, j)),
        out_shape=jax.ShapeDtypeStruct((M, N), F32),
        scratch_shapes=[pltpu.VMEM((tm, D), BF16)],
        compiler_params=pltpu.CompilerParams(dimension_semantics=("parallel", "arbitrary"),
                                             vmem_limit_bytes=VMEM_LIMIT),
        name="norm_proj",
    )(x, gamma, w)


def _gdn_kernel(qkv_ref, z_ref, sm_ref, smT_ref, conv0_ref, S0_ref, cw_ref, prow_ref, pcol_ref, gnorm_ref,
                o_ref, convn_ref, S_ref, xp_ref, *, nb, L, NC, T_valid, H, DK, DV):
    n = pl.program_id(1)
    K = H * DK
    pad0 = SUBLANES - (CONV_W - 1)

    @pl.when(n == 0)
    def _():
        xp_ref[:, pad0:SUBLANES, :] = conv0_ref[...]
        S_ref[...] = S0_ref[...]

    incl, strict = _tri_masks(L)
    lv = L if NC > 1 else T_valid
    chains = []
    for b in range(nb):
        xp_ref[b, SUBLANES:SUBLANES + L, :] = qkv_ref[b]
        y = xp_ref[b, pad0:pad0 + L, :] * cw_ref[0:1, :]
        for j in range(1, CONV_W):
            y = y + xp_ref[b, pad0 + j:pad0 + j + L, :] * cw_ref[j:j + 1, :]
        qkv = y * _sigmoid(y)
        tail = xp_ref[b, SUBLANES + lv - (CONV_W - 1):SUBLANES + lv, :]
        xp_ref[b, pad0:SUBLANES, :] = tail
        convn_ref[b] = tail

        sm = sm_ref[b]
        smT = smT_ref[b, 0]
        beta_c = _sigmoid(sm)
        g_c = -jnp.exp(prow_ref[0:1, :]) * _softplus(sm + prow_ref[1:2, :])
        g_r = -jnp.exp(pcol_ref[:, 0:1]) * _softplus(smT + pcol_ref[:, 1:2])
        if T_valid < NC * L:
            vc = lax.broadcasted_iota(jnp.int32, (L, 1), 0) < T_valid
            vr = lax.broadcasted_iota(jnp.int32, (1, L), 1) < T_valid
            beta_c = jnp.where(vc, beta_c, 0.0)
            g_c = jnp.where(vc, g_c, 0.0)
            g_r = jnp.where(vr, g_r, 0.0)
        gc_c, gc_r = _cumsum_both(g_c, g_r, incl)
        for h in range(H):
            qh = qkv[:, h * DK:(h + 1) * DK]
            kh = qkv[:, K + h * DK:K + (h + 1) * DK]
            vh = qkv[:, 2 * K + h * DV:2 * K + (h + 1) * DV]
            qh = qh * lax.rsqrt(jnp.sum(qh * qh, axis=-1, keepdims=True) + EPS) * (DK ** -0.5)
            kh = kh * lax.rsqrt(jnp.sum(kh * kh, axis=-1, keepdims=True) + EPS)
            b_c = beta_c[:, h:h + 1]
            gcc = gc_c[:, H + h:H + h + 1]
            gcr = gc_r[H + h:H + h + 1, :]
            gl = gcc[L - 1:L, :]
            eg = jnp.exp(gcc)
            kb = kh * b_c
            chains.append(dict(
                b=b, h=h, kh=kh.astype(BF16), kbq=jnp.concatenate([kb, qh], axis=0).astype(BF16),
                decay=jnp.exp(jnp.where(incl, gcc - gcr, -jnp.inf)),
                rhs=jnp.concatenate([kb * eg, vh * b_c], axis=1), qd=qh * eg,
                kd=(kh * jnp.exp(gl - gcc)).astype(BF16), dl=jnp.exp(gl)))

    for c in chains:
        r = _mm_nt(c["kbq"], c["kh"])
        c["A"] = jnp.where(strict, r[:L] * c["decay"], 0.0)
        c["qk"] = (r[L:] * c["decay"]).astype(BF16)
    Xs, Rs = _tri_inv_unit_lower([c["A"] for c in chains])
    corr = [_mm(R, c["rhs"]) for R, c in zip(Rs, chains)]
    wu = [_mm(X, c["rhs"] + cr) for X, c, cr in zip(Xs, chains, corr)]
    states = [S_ref[c["b"], c["h"]] for c in chains]
    wq = [_mm(jnp.concatenate([x[:, :DK], c["qd"]], axis=0), S) for x, c, S in zip(wu, chains, states)]
    v_new = [(x[:, DK:] - y[:L]).astype(BF16) for x, y in zip(wu, wq)]
    o_intra = [_mm(c["qk"], v) for c, v in zip(chains, v_new)]
    s_upd = [_mm_tn(c["kd"], v) for c, v in zip(chains, v_new)]
    for c, S, su in zip(chains, states, s_upd):
        S_ref[c["b"], c["h"]] = S * c["dl"] + su
    o = [y[L:] + oi for y, oi in zip(wq, o_intra)]
    rs = [lax.rsqrt(jnp.mean(x * x, axis=-1, keepdims=True) + EPS) for x in o]
    zs = [z_ref[c["b"], :, c["h"] * DV:(c["h"] + 1) * DV] for c in chains]
    gate = [z * _sigmoid(z) * gnorm_ref[:, c["h"] * DV:(c["h"] + 1) * DV] for z, c in zip(zs, chains)]
    for x, r, g, c in zip(o, rs, gate, chains):
        o_ref[c["b"], :, c["h"] * DV:(c["h"] + 1) * DV] = x * r * g


def _gdn(proj, smT, conv0, S0, cw, prow, pcol, gnorm, *, NC, L, T_valid, offs):
    B, H, DK, DV = S0.shape
    CG = conv0.shape[-1]
    V = H * DV
    T_pad = proj.shape[1]
    nb = _tile(B, max(1, CHAINS // H), 1)
    kern = functools.partial(_gdn_kernel, nb=nb, L=L, NC=NC, T_valid=T_valid, H=H, DK=DK, DV=DV)
    return pl.pallas_call(
        kern,
        grid=(B // nb, NC),
        in_specs=[pl.BlockSpec((nb, L, CG), lambda i, n: (i, n, offs["qkv"] // CG)),
                  pl.BlockSpec((nb, L, V), lambda i, n: (i, n, offs["z"] // V)),
                  pl.BlockSpec((nb, L, LANES), lambda i, n: (i, n, offs["small"] // LANES)),
                  pl.BlockSpec((nb, 1, GATE_ROWS, L), lambda i, n: (i, n, 0, 0)),
                  pl.BlockSpec((nb, CONV_W - 1, CG), lambda i, n: (i, 0, 0)),
                  pl.BlockSpec((nb, H, DK, DV), lambda i, n: (i, 0, 0, 0)),
                  pl.BlockSpec((CONV_W, CG), lambda i, n: (0, 0)),
                  pl.BlockSpec((2, LANES), lambda i, n: (0, 0)),
                  pl.BlockSpec((GATE_ROWS, 2), lambda i, n: (0, 0)),
                  pl.BlockSpec((1, V), lambda i, n: (0, 0))],
        out_specs=[pl.BlockSpec((nb, L, V), lambda i, n: (i, n, 0)),
                   pl.BlockSpec((nb, CONV_W - 1, CG), lambda i, n: (i, 0, 0)),
                   pl.BlockSpec((nb, H, DK, DV), lambda i, n: (i, 0, 0, 0))],
        out_shape=[jax.ShapeDtypeStruct((B, T_pad, V), F32),
                   jax.ShapeDtypeStruct(conv0.shape, F32),
                   jax.ShapeDtypeStruct(S0.shape, F32)],
        scratch_shapes=[pltpu.VMEM((nb, SUBLANES + L, CG), F32)],
        compiler_params=pltpu.CompilerParams(dimension_semantics=("parallel", "arbitrary"),
                                             vmem_limit_bytes=VMEM_LIMIT),
        name="gdn_chunk",
    )(proj, proj, proj, smT, conv0, S0, cw, prow, pcol, gnorm)


def _mlstm_kernel(qk_ref, v_ref, og_ref, sm_ref, smT_ref, C0_ref, n0_ref, m0_ref, prow_ref, pcol_ref, norm_ref,
                  h_ref, C_ref, n_ref, m_ref, *, nb, L, NC, T_valid, H, DK, DV):
    step = pl.program_id(1)
    K = H * DK
    g0 = 3 * H

    @pl.when(step == 0)
    def _():
        C_ref[...] = C0_ref[...]
        n_ref[...] = n0_ref[...]
        m_ref[...] = m0_ref[...]

    incl, _ = _tri_masks(L)
    lane = lax.broadcasted_iota(jnp.int32, (1, K), 1)
    lane1 = lax.broadcasted_iota(jnp.int32, (1, LANES), 1)
    rows = lax.broadcasted_iota(jnp.int32, (L, 1), 0)
    seqs = range(nb)

    sm = [sm_ref[b] for b in seqs]
    smT = [smT_ref[b, 0] for b in seqs]
    li_c = [x + prow_ref[0:1, :] for x in sm]
    li_r = [x + pcol_ref[:, 0:1] for x in smT]
    lf_c = [-_softplus(-(x + prow_ref[1:2, :])) for x in sm]
    lf_r = [-_softplus(-(x + pcol_ref[:, 1:2])) for x in smT]
    if T_valid < NC * L:
        vc = rows < T_valid
        vr = lax.broadcasted_iota(jnp.int32, (1, L), 1) < T_valid
        li_c = [jnp.where(vc, x, NEG_BIG) for x in li_c]
        li_r = [jnp.where(vr, x, NEG_BIG) for x in li_r]
        lf_c = [jnp.where(vc, x, 0.0) for x in lf_c]
        lf_r = [jnp.where(vr, x, 0.0) for x in lf_r]
    FF = [_cumsum_both(c, r, incl) for c, r in zip(lf_c, lf_r)]
    F_c = [f[0] for f in FF]
    F_r = [f[1] for f in FF]
    r_c = [pltpu.roll(x, H, axis=1) - f for x, f in zip(li_c, F_c)]
    cm = r_c
    s = 1
    while s < L:
        cm = [jnp.maximum(x, jnp.where(rows >= s, pltpu.roll(x, s, axis=0), -jnp.inf)) for x in cm]
        s *= 2
    m_old = [m_ref[b] for b in seqs]
    mx = [jnp.maximum(x, m) for x, m in zip(cm, m_old)]
    mx_last = [x[L - 1:L, :] for x in mx]
    dec = [jnp.exp(m - x) for m, x in zip(m_old, mx_last)]
    wC = [jnp.exp(r - x) for r, x in zip(r_c, mx_last)]
    a_all = [jnp.exp(m - x) for m, x in zip(m_old, mx)]
    floor = [jnp.exp(-f - x) for f, x in zip(F_c, mx)]
    for b in seqs:
        m_ref[b] = jnp.where((lane1 >= g0) & (lane1 < g0 + H), F_c[b][L - 1:L, :] + mx_last[b], 0.0)

    q_all = [qk_ref[b, :, :K] * (DK ** -0.5) for b in seqs]
    k_all = [qk_ref[b, :, K:] for b in seqs]
    k_bf = [x.astype(BF16) for x in k_all]
    C_all = [C_ref[b] for b in seqs]
    C_bf = [x.astype(BF16) for x in C_all]
    n_row = [n_ref[b] for b in seqs]

    half = lane1 < DK
    kw_cols, dec_lanes = [], []
    for b in seqs:
        cols = []
        for c in range(K // LANES):
            w_lo = wC[b][:, g0 + 2 * c:g0 + 2 * c + 1]
            w_hi = wC[b][:, g0 + 2 * c + 1:g0 + 2 * c + 2]
            cols.append(k_all[b][:, c * LANES:(c + 1) * LANES] * jnp.where(half, w_lo, w_hi))
        kw_cols.append(cols)
        d = dec[b][:, g0:g0 + 1]
        for h in range(1, H):
            d = jnp.where(lane >= h * DK, dec[b][:, g0 + h:g0 + h + 1], d)
        dec_lanes.append(d)
    for b in seqs:
        n_upd = jnp.concatenate([jnp.sum(x, axis=0, keepdims=True) for x in kw_cols[b]], axis=1)
        n_ref[b] = dec_lanes[b] * n_row[b] + n_upd

    chains = [(b, h) for b in seqs for h in range(H)]
    in_head = [(lane >= h * DK) & (lane < (h + 1) * DK) for h in range(H)]
    q_h = [jnp.where(in_head[h], q_all[b], 0.0) for b, h in chains]
    q_bf = [x.astype(BF16) for x in q_h]
    v_bf = [v_ref[b, :, h * DV:(h + 1) * DV].astype(BF16) for b, h in chains]
    kw = [kw_cols[b][h // 2][:, (h % 2) * DK:(h % 2 + 1) * DK].astype(BF16) for b, h in chains]
    expD = [jnp.exp(jnp.where(incl, (li_r[b][2 * H + h:2 * H + h + 1, :] - F_r[b][g0 + h:g0 + h + 1, :])
                              - mx[b][:, g0 + h:g0 + h + 1], -jnp.inf)) for b, h in chains]
    qn = [jnp.sum(x * n_row[b], axis=1, keepdims=True) for x, (b, h) in zip(q_h, chains)]

    qk = [_mm_nt(x, k_bf[b]) for x, (b, h) in zip(q_bf, chains)]
    Sm = [e * x for e, x in zip(expD, qk)]
    inter_state = [_mm(x, C_bf[b]) for x, (b, h) in zip(q_bf, chains)]
    intra = [_mm(x, v) for x, v in zip(Sm, v_bf)]
    upd = [_mm_tn(x, v) for x, v in zip(kw, v_bf)]
    rowsum = [jnp.sum(x, axis=1, keepdims=True) for x in Sm]
    for i, (b, h) in enumerate(chains):
        C_ref[b, h * DK:(h + 1) * DK, :] = dec[b][:, g0 + h:g0 + h + 1] * C_all[b][h * DK:(h + 1) * DK, :] + upd[i]
    a_h = [a_all[b][:, g0 + h:g0 + h + 1] for b, h in chains]
    den = [a * x + y for a, x, y in zip(a_h, qn, rowsum)]
    scale = [1.0 / jnp.maximum(jnp.abs(d), floor[b][:, g0 + h:g0 + h + 1]) for d, (b, h) in zip(den, chains)]
    hh = [(a * x + y) * sc for a, x, y, sc in zip(a_h, inter_state, intra, scale)]
    ms = [jnp.mean(x * x, axis=-1, keepdims=True) for x in hh]
    rs = [lax.rsqrt(x + EPS) for x in ms]
    gate = [_sigmoid(og_ref[b, :, h * DV:(h + 1) * DV]) * norm_ref[:, h * DV:(h + 1) * DV] for b, h in chains]
    for x, r, g, (b, h) in zip(hh, rs, gate, chains):
        h_ref[b, :, h * DV:(h + 1) * DV] = x * r * g


def _mlstm(proj, smT, C0, n0, m0, prow, pcol, norm, *, NC, L, T_valid, offs):
    B, H, DK, DV = C0.shape
    K = H * DK
    V = H * DV
    T_pad = proj.shape[1]
    nb = _tile(B, max(1, CHAINS // H), 1)
    assert 2 * DK == LANES and 4 * H <= LANES
    m_lanes = jnp.pad(m0.reshape(B, 1, H), ((0, 0), (0, 0), (3 * H, LANES - 4 * H)))
    kern = functools.partial(_mlstm_kernel, nb=nb, L=L, NC=NC, T_valid=T_valid, H=H, DK=DK, DV=DV)
    outs = pl.pallas_call(
        kern,
        grid=(B // nb, NC),
        in_specs=[pl.BlockSpec((nb, L, 2 * K), lambda i, n: (i, n, offs["qk_m"] // (2 * K))),
                  pl.BlockSpec((nb, L, V), lambda i, n: (i, n, offs["v_m"] // V)),
                  pl.BlockSpec((nb, L, V), lambda i, n: (i, n, offs["o_m"] // V)),
                  pl.BlockSpec((nb, L, LANES), lambda i, n: (i, n, offs["small"] // LANES)),
                  pl.BlockSpec((nb, 1, GATE_ROWS, L), lambda i, n: (i, n, 0, 0)),
                  pl.BlockSpec((nb, K, DV), lambda i, n: (i, 0, 0)),
                  pl.BlockSpec((nb, 1, K), lambda i, n: (i, 0, 0)),
                  pl.BlockSpec((nb, 1, LANES), lambda i, n: (i, 0, 0)),
                  pl.BlockSpec((2, LANES), lambda i, n: (0, 0)),
                  pl.BlockSpec((GATE_ROWS, 2), lambda i, n: (0, 0)),
                  pl.BlockSpec((1, V), lambda i, n: (0, 0))],
        out_specs=[pl.BlockSpec((nb, L, V), lambda i, n: (i, n, 0)),
                   pl.BlockSpec((nb, K, DV), lambda i, n: (i, 0, 0)),
                   pl.BlockSpec((nb, 1, K), lambda i, n: (i, 0, 0)),
                   pl.BlockSpec((nb, 1, LANES), lambda i, n: (i, 0, 0))],
        out_shape=[jax.ShapeDtypeStruct((B, T_pad, V), F32),
                   jax.ShapeDtypeStruct((B, K, DV), F32),
                   jax.ShapeDtypeStruct((B, 1, K), F32),
                   jax.ShapeDtypeStruct((B, 1, LANES), F32)],
        compiler_params=pltpu.CompilerParams(dimension_semantics=("parallel", "arbitrary"),
                                             vmem_limit_bytes=VMEM_LIMIT),
        name="mlstm_chunk",
    )(proj, proj, proj, proj, smT, C0.reshape(B, K, DV), n0.reshape(B, 1, K), m_lanes, prow, pcol, norm)
    hm, C_new, n_new, m_new = outs
    return hm, C_new.reshape(B, H, DK, DV), n_new.reshape(B, H, DK), m_new[:, 0, 3 * H:4 * H]


def _merge_kernel(og_ref, hm_ref, gg_ref, gm_ref, x_ref, wbg_ref, wbm_ref, wout_ref, o_ref):
    br_g = jnp.dot(og_ref[...].astype(BF16), wbg_ref[...], preferred_element_type=F32)
    br_m = jnp.dot(hm_ref[...].astype(BF16), wbm_ref[...], preferred_element_type=F32)
    merged = _sigmoid(gg_ref[...]) * br_g + _sigmoid(gm_ref[...]) * br_m
    o_ref[...] = x_ref[...] + jnp.dot(merged.astype(BF16), wout_ref[...], preferred_element_type=F32)


def _merge(og, hm, proj, x, wbg, wbm, wout, *, offs):
    M, D = x.shape
    Vg = og.shape[1]
    Vm = hm.shape[1]
    tm = _tile(M, 512, SUBLANES)
    return pl.pallas_call(
        _merge_kernel,
        grid=(M // tm,),
        in_specs=[pl.BlockSpec((tm, Vg), lambda i: (i, 0)),
                  pl.BlockSpec((tm, Vm), lambda i: (i, 0)),
                  pl.BlockSpec((tm, D), lambda i: (i, offs["gate_g"] // D)),
                  pl.BlockSpec((tm, D), lambda i: (i, offs["gate_m"] // D)),
                  pl.BlockSpec((tm, D), lambda i: (i, 0)),
                  pl.BlockSpec((Vg, D), lambda i: (0, 0)),
                  pl.BlockSpec((Vm, D), lambda i: (0, 0)),
                  pl.BlockSpec((D, D), lambda i: (0, 0))],
        out_specs=pl.BlockSpec((tm, D), lambda i: (i, 0)),
        out_shape=jax.ShapeDtypeStruct((M, D), F32),
        compiler_params=pltpu.CompilerParams(dimension_semantics=("parallel",), vmem_limit_bytes=VMEM_LIMIT),
        name="merge_out",
    )(og, hm, proj, proj, x, wbg, wbm, wout)


def _mlp_kernel(x_ref, g_ref, wup_ref, wdn_ref, gf_ref, o_ref, xn_ref, acc_ref, *, final_norm):
    k = pl.program_id(1)

    @pl.when(k == 0)
    def _():
        x = x_ref[...]
        ms = jnp.mean(x * x, axis=-1, keepdims=True)
        xn_ref[...] = (x * lax.rsqrt(ms + EPS) * g_ref[...]).astype(BF16)
        acc_ref[...] = jnp.zeros_like(acc_ref)

    hcol = jnp.maximum(jnp.dot(xn_ref[...], wup_ref[...], preferred_element_type=F32), 0.0)
    acc_ref[...] += jnp.dot((hcol * hcol).astype(BF16), wdn_ref[...], preferred_element_type=F32)

    @pl.when(k == pl.num_programs(1) - 1)
    def _():
        y = x_ref[...] + acc_ref[...]
        if final_norm:
            ms = jnp.mean(y * y, axis=-1, keepdims=True)
            y = y * lax.rsqrt(ms + EPS) * gf_ref[...]
        o_ref[...] = y


def _mlp(x, gamma, wup, wdn, gamma_final, *, final_norm):
    M, D = x.shape
    FF = wup.shape[1]
    tm = _tile(M, 1024, SUBLANES)
    tf = _tile(FF, 1024, LANES)
    return pl.pallas_call(
        functools.partial(_mlp_kernel, final_norm=final_norm),
        grid=(M // tm, FF // tf),
        in_specs=[pl.BlockSpec((tm, D), lambda i, k: (i, 0)),
                  pl.BlockSpec((1, D), lambda i, k: (0, 0)),
                  pl.BlockSpec((D, tf), lambda i, k: (0, k)),
                  pl.BlockSpec((tf, D), lambda i, k: (k, 0)),
                  pl.BlockSpec((1, D), lambda i, k: (0, 0))],
        out_specs=pl.BlockSpec((tm, D), lambda i, k: (i, 0)),
        out_shape=jax.ShapeDtypeStruct((M, D), F32),
        scratch_shapes=[pltpu.VMEM((tm, D), BF16), pltpu.VMEM((tm, D), F32)],
        compiler_params=pltpu.CompilerParams(dimension_semantics=("parallel", "arbitrary"),
                                             vmem_limit_bytes=VMEM_LIMIT),
        name="mlp",
    )(x, gamma, wup, wdn, gamma_final)


def _pack_w_in(w_in, dims):
    D, Hg, DKg, DVg, Hm, DKm, DVm = dims
    Kg, Vg, Km, Vm = Hg * DKg, Hg * DVg, Hm * DKm, Hm * DVm
    sizes = (Kg, Kg, Vg, Vg, Hg, Hg, Km, Km, Vm, Vm, Hm, Hm, D, D)
    starts = [0]
    for s in sizes:
        starts.append(starts[-1] + s)
    col = lambda a, b: w_in[:, starts[a]:starts[b]]
    small = jnp.concatenate([col(4, 5), col(5, 6), col(10, 11), col(11, 12)], axis=1)
    assert 2 * Hg + 2 * Hm <= GATE_ROWS and Hg == Hm
    small = jnp.pad(small, ((0, 0), (0, LANES - small.shape[1])))
    segs = [("qkv", col(0, 3)), ("z", col(3, 4)), ("gate_g", col(12, 13)), ("gate_m", col(13, 14)),
            ("qk_m", col(6, 8)), ("v_m", col(8, 9)), ("o_m", col(9, 10)), ("small", small)]
    offs, off = {}, 0
    for name, seg in segs:
        assert off % seg.shape[1] == 0, (name, off, seg.shape)
        offs[name] = off
        off += seg.shape[1]
    return jnp.concatenate([s for _, s in segs], axis=1).astype(BF16), offs


def _gate_params(H, first, lane_first, second, lane_second):
    row = (jnp.zeros((2, LANES), F32).at[0, lane_first:lane_first + H].set(first.astype(F32))
           .at[1, lane_second:lane_second + H].set(second.astype(F32)))
    return row, row[:, :GATE_ROWS].T


def _head_norm_row(w, H, DV):
    return jnp.broadcast_to(w.astype(F32).reshape(-1, DV), (H, DV)).reshape(1, H * DV)


def _trunk(x, conv0, S0, C0, n0, m0, P, *, B, T_valid, L):
    M, D = x.shape
    T_pad = M // B
    NC = T_pad // L
    depth = P["w_in"].shape[0]
    offs = P["offs"]
    new = ([], [], [], [], [])
    for l in range(depth):
        proj = _norm_proj(x, P["norm_mix"][l], P["w_in"][l])
        proj3 = proj.reshape(B, T_pad, proj.shape[1])
        smT = proj3[:, :, offs["small"]:offs["small"] + GATE_ROWS].reshape(B, NC, L, GATE_ROWS)
        smT = jnp.swapaxes(smT, 2, 3)
        og, conv_n, S_n = _gdn(proj3, smT, conv0[l], S0[l], P["conv_w"][l], P["gdn_prow"][l], P["gdn_pcol"][l],
                               P["gdn_norm"][l], NC=NC, L=L, T_valid=T_valid, offs=offs)
        hm, C_n, n_n, m_n = _mlstm(proj3, smT, C0[l], n0[l], m0[l], P["ml_prow"][l], P["ml_pcol"][l],
                                   P["ml_norm"][l], NC=NC, L=L, T_valid=T_valid, offs=offs)
        x = _merge(og.reshape(M, -1), hm.reshape(M, -1), proj, x, P["w_bg"][l], P["w_bm"][l], P["w_out"][l],
                   offs=offs)
        x = _mlp(x, P["norm_mlp"][l], P["w_up"][l], P["w_down"][l], P["norm_final"],
                 final_norm=(l == depth - 1))
        for lst, s in zip(new, (conv_n, S_n, C_n, n_n, m_n)):
            lst.append(s)
    return (x,) + tuple(jnp.stack(lst) for lst in new)


def kernel(x_prompt, x_sample, state_gdn_conv, state_gdn_S, state_mlstm_C, state_mlstm_n, state_mlstm_m, norm_mix, w_in, gdn_conv_w, gdn_A_log, gdn_dt_bias, gdn_norm, ml_i_bias, ml_f_bias, ml_norm, w_branch_gdn, w_branch_ml, w_out, norm_mlp, w_up, w_down, norm_final):
    Bp, Tp, D = x_prompt.shape
    Bs, Ts, _ = x_sample.shape
    depth = w_in.shape[0]
    _, _, Hg, DKg, DVg = state_gdn_S.shape
    _, _, Hm, DKm, DVm = state_mlstm_C.shape
    CG = state_gdn_conv.shape[-1]
    dims = (D, Hg, DKg, DVg, Hm, DKm, DVm)
    assert Ts >= CONV_W - 1 and Tp >= CONV_W - 1

    packed = [_pack_w_in(w_in[l], dims) for l in range(depth)]
    P = {
        "offs": packed[0][1],
        "w_in": jnp.stack([p[0] for p in packed]),
        "norm_mix": norm_mix.reshape(depth, 1, D),
        "norm_mlp": norm_mlp.reshape(depth, 1, D),
        "norm_final": norm_final.reshape(1, D),
        "conv_w": gdn_conv_w,
        "w_bg": w_branch_gdn.astype(BF16),
        "w_bm": w_branch_ml.astype(BF16),
        "w_out": w_out.astype(BF16),
        "w_up": w_up.astype(BF16),
        "w_down": w_down.astype(BF16),
        "gdn_norm": jnp.stack([_head_norm_row(gdn_norm[l], Hg, DVg) for l in range(depth)]),
        "ml_norm": jnp.stack([_head_norm_row(ml_norm[l], Hm, DVm) for l in range(depth)]),
    }
    gp = [_gate_params(Hg, gdn_A_log[l], Hg, gdn_dt_bias[l], Hg) for l in range(depth)]
    mp = [_gate_params(Hm, ml_i_bias[l], 2 * Hg, ml_f_bias[l], 2 * Hg + Hm) for l in range(depth)]
    P["gdn_prow"] = jnp.stack([g[0] for g in gp])
    P["gdn_pcol"] = jnp.stack([g[1] for g in gp])
    P["ml_prow"] = jnp.stack([g[0] for g in mp])
    P["ml_pcol"] = jnp.stack([g[1] for g in mp])

    Lp = CHUNK if Tp % CHUNK == 0 else Tp
    assert Lp % SUBLANES == 0
    zeros = lambda *s: jnp.zeros(s, F32)
    yp, conv_p, S_p, C_p, n_p, m_p = _trunk(
        x_prompt.reshape(Bp * Tp, D), zeros(depth, Bp, CONV_W - 1, CG), zeros(depth, Bp, Hg, DKg, DVg),
        zeros(depth, Bp, Hm, DKm, DVm), zeros(depth, Bp, Hm, DKm), zeros(depth, Bp, Hm), P,
        B=Bp, T_valid=Tp, L=Lp)

    Ls = -(-Ts // SUBLANES) * SUBLANES
    xs = jnp.pad(x_sample, ((0, 0), (0, Ls - Ts), (0, 0))).reshape(Bs * Ls, D)
    ys, conv_s, S_s, C_s, n_s, m_s = _trunk(
        xs, state_gdn_conv, state_gdn_S, state_mlstm_C, state_mlstm_n, state_mlstm_m, P,
        B=Bs, T_valid=Ts, L=Ls)
    ys = ys.reshape(Bs, Ls, D)[:, :Ts]
    return (yp.reshape(Bp, Tp, D), ys, conv_p, S_p, C_p, n_p, m_p, conv_s, S_s, C_s, n_s, m_s)
```

```python
import functools

import jax
import jax.numpy as jnp
from jax import lax
from jax.experimental import pallas as pl
from jax.experimental.pallas import tpu as pltpu

F32 = jnp.float32
BF16 = jnp.bfloat16
EPS = 1e-6
CONV_W = 4
LANES = 128
SUBLANES = 8
GATE_ROWS = 16
CHUNK = 64
CHAINS = 16
NEG_BIG = -1e30
VMEM_LIMIT = 48 * 1024 * 1024
HIGHEST = lax.Precision.HIGHEST


def _sigmoid(x):
    return 1.0 / (1.0 + jnp.exp(-x))


def _softplus(x):
    return jnp.maximum(x, 0.0) + jnp.log(1.0 + jnp.exp(-jnp.abs(x)))


def _mm(a, b):
    return jnp.dot(a.astype(BF16), b.astype(BF16), preferred_element_type=F32)


def _mm_nt(a, b):
    return lax.dot_general(a.astype(BF16), b.astype(BF16), (((1,), (1,)), ((), ())), preferred_element_type=F32)


def _mm_tn(a, b):
    return lax.dot_general(a.astype(BF16), b.astype(BF16), (((0,), (0,)), ((), ())), preferred_element_type=F32)


def _split_bf16(a):
    hi = a.astype(BF16)
    return hi, (a - hi.astype(F32)).astype(BF16)


def _tile(n, cap, mult):
    best = None
    for t in range(mult, min(n, cap) + 1, mult):
        if n % t == 0:
            best = t
    return best if best is not None else n


def _tri_masks(L):
    row = lax.broadcasted_iota(jnp.int32, (L, L), 0)
    col = lax.broadcasted_iota(jnp.int32, (L, L), 1)
    return row >= col, row > col


def _cumsum_both(g_col, g_row, incl):
    L = incl.shape[0]
    row = lax.broadcasted_iota(jnp.int32, (L, L), 0)
    col = lax.broadcasted_iota(jnp.int32, (L, L), 1)
    lower = jnp.where(incl, 1.0, 0.0)
    upper = jnp.where(row <= col, 1.0, 0.0)
    c_col = jnp.dot(lower, g_col, precision=HIGHEST, preferred_element_type=F32)
    c_row = jnp.dot(g_row, upper, precision=HIGHEST, preferred_element_type=F32)
    return c_col, c_row


def _tri_inv_unit_lower(As):
    L = As[0].shape[0]
    row = lax.broadcasted_iota(jnp.int32, (L, L), 0)
    col = lax.broadcasted_iota(jnp.int32, (L, L), 1)
    eye = jnp.where(row == col, 1.0, 0.0)
    levels = max(1, (L - 1).bit_length())
    Xs = [eye - A for A in As]
    if levels > 1:
        Ps = [_mm(A, A) for A in As]
        for _ in range(1, levels - 1):
            Rs = [_mm(jnp.concatenate([P, X], axis=0), P) for P, X in zip(Ps, Xs)]
            Ps = [R[:L] for R in Rs]
            Xs = [X + R[L:] for X, R in zip(Xs, Rs)]
        Xs = [X + _mm(X, P) for X, P in zip(Xs, Ps)]
    splits = [(_split_bf16(A), _split_bf16(X)) for A, X in zip(As, Xs)]
    AX1 = [jnp.dot(jnp.concatenate([a_hi, a_lo], axis=0), x_hi, preferred_element_type=F32)
           for (a_hi, a_lo), (x_hi, _) in splits]
    AX2 = [jnp.dot(a_hi, x_lo, preferred_element_type=F32) for (a_hi, _), (_, x_lo) in splits]
    Rs = [(eye - X) - (r1[:L] + (r1[L:] + r2)) for X, r1, r2 in zip(Xs, AX1, AX2)]
    return Xs, Rs


def _norm_proj_kernel(x_ref, g_ref, w_ref, o_ref):
    x = x_ref[...]
    ms = jnp.mean(x * x, axis=-1, keepdims=True)
    xn = (x * lax.rsqrt(ms + EPS) * g_ref[...]).astype(BF16)
    o_ref[...] = jnp.dot(xn, w_ref[...], preferred_element_type=F32)


def _norm_proj(x, gamma, w):
    M, D = x.shape
    N = w.shape[1]
    tm = _tile(M, 512, SUBLANES)
    return pl.pallas_call(
        _norm_proj_kernel,
        grid=(M // tm,),
        in_specs=[pl.BlockSpec((tm, D), lambda i: (i, 0)),
                  pl.BlockSpec((1, D), lambda i: (0, 0)),
                  pl.BlockSpec((D, N), lambda i: (0, 0))],
        out_specs=pl.BlockSpec((tm, N), lambda i: (i, 0)),
        out_shape=jax.ShapeDtypeStruct((M, N), F32),
        compiler_params=pltpu.CompilerParams(dimension_semantics=("parallel",), vmem_limit_bytes=VMEM_LIMIT),
        name="norm_proj",
    )(x, gamma, w)


def _gdn_kernel(qkv_ref, sm_ref, smT_ref, conv0_ref, S0_ref, cw_ref, prow_ref, pcol_ref, gnorm_ref,
                o_ref, convn_ref, S_ref, xp_ref, *, nb, L, NC, T_valid, H, DK, DV):
    n = pl.program_id(1)
    K = H * DK
    pad0 = SUBLANES - (CONV_W - 1)

    @pl.when(n == 0)
    def _():
        xp_ref[:, pad0:SUBLANES, :] = conv0_ref[...]
        S_ref[...] = S0_ref[...]

    incl, strict = _tri_masks(L)
    lv = L if NC > 1 else T_valid
    chains = []
    for b in range(nb):
        xp_ref[b, SUBLANES:SUBLANES + L, :] = qkv_ref[b]
        xfull = xp_ref[b]
        y = xfull * cw_ref[CONV_W - 1:CONV_W, :]
        for j in range(CONV_W - 1):
            y = y + pltpu.roll(xfull, CONV_W - 1 - j, axis=0) * cw_ref[j:j + 1, :]
        y = y[SUBLANES:, :]
        qkv = y * _sigmoid(y)
        tail = xp_ref[b, SUBLANES + lv - (CONV_W - 1):SUBLANES + lv, :]
        xp_ref[b, pad0:SUBLANES, :] = tail
        convn_ref[b] = tail

        sm = sm_ref[b]
        smT = smT_ref[b, 0]
        beta_c = _sigmoid(sm)
        g_c = -jnp.exp(prow_ref[0:1, :]) * _softplus(sm + prow_ref[1:2, :])
        g_r = -jnp.exp(pcol_ref[:, 0:1]) * _softplus(smT + pcol_ref[:, 1:2])
        if T_valid < NC * L:
            vc = lax.broadcasted_iota(jnp.int32, (L, 1), 0) < T_valid
            vr = lax.broadcasted_iota(jnp.int32, (1, L), 1) < T_valid
            beta_c = jnp.where(vc, beta_c, 0.0)
            g_c = jnp.where(vc, g_c, 0.0)
            g_r = jnp.where(vr, g_r, 0.0)
        gc_c, gc_r = _cumsum_both(g_c, g_r, incl)
        for h in range(H):
            qh = qkv[:, h * DK:(h + 1) * DK]
            kh = qkv[:, K + h * DK:K + (h + 1) * DK]
            vh = qkv[:, 2 * K + h * DV:2 * K + (h + 1) * DV]
            qh = qh * lax.rsqrt(jnp.sum(qh * qh, axis=-1, keepdims=True) + EPS) * (DK ** -0.5)
            kh = kh * lax.rsqrt(jnp.sum(kh * kh, axis=-1, keepdims=True) + EPS)
            b_c = beta_c[:, h:h + 1]
            gcc = gc_c[:, H + h:H + h + 1]
            gcr = gc_r[H + h:H + h + 1, :]
            gl = gcc[L - 1:L, :]
            eg = jnp.exp(gcc)
            kb = kh * b_c
            chains.append(dict(
                b=b, h=h, kh=kh.astype(BF16), kbq=jnp.concatenate([kb, qh], axis=0).astype(BF16),
                decay=jnp.exp(jnp.where(incl, gcc - gcr, -jnp.inf)),
                rhs=jnp.concatenate([kb * eg, vh * b_c], axis=1), qd=qh * eg,
                kd=(kh * jnp.exp(gl - gcc)).astype(BF16), dl=jnp.exp(gl)))

    for c in chains:
        r = _mm_nt(c["kbq"], c["kh"])
        c["A"] = jnp.where(strict, r[:L] * c["decay"], 0.0)
        c["qk"] = (r[L:] * c["decay"]).astype(BF16)
    Xs, Rs = _tri_inv_unit_lower([c["A"] for c in chains])
    corr = [_mm(R, c["rhs"]) for R, c in zip(Rs, chains)]
    wu = [_mm(X, c["rhs"] + cr) for X, c, cr in zip(Xs, chains, corr)]
    states = [S_ref[c["b"], c["h"]] for c in chains]
    wq = [_mm(jnp.concatenate([x[:, :DK], c["qd"]], axis=0), S) for x, c, S in zip(wu, chains, states)]
    v_new = [(x[:, DK:] - y[:L]).astype(BF16) for x, y in zip(wu, wq)]
    o_intra = [_mm(c["qk"], v) for c, v in zip(chains, v_new)]
    s_upd = [_mm_tn(c["kd"], v) for c, v in zip(chains, v_new)]
    for c, S, su in zip(chains, states, s_upd):
        S_ref[c["b"], c["h"]] = S * c["dl"] + su
    o = [y[L:] + oi for y, oi in zip(wq, o_intra)]
    rs = [lax.rsqrt(jnp.mean(x * x, axis=-1, keepdims=True) + EPS) for x in o]
    for x, r, c in zip(o, rs, chains):
        o_ref[c["b"], :, c["h"] * DV:(c["h"] + 1) * DV] = x * r * gnorm_ref[:, c["h"] * DV:(c["h"] + 1) * DV]


def _gdn(proj, smT, conv0, S0, cw, prow, pcol, gnorm, *, NC, L, T_valid, offs):
    B, H, DK, DV = S0.shape
    CG = conv0.shape[-1]
    V = H * DV
    T_pad = proj.shape[1]
    nb = _tile(B, max(1, CHAINS // H), 1)
    kern = functools.partial(_gdn_kernel, nb=nb, L=L, NC=NC, T_valid=T_valid, H=H, DK=DK, DV=DV)
    return pl.pallas_call(
        kern,
        grid=(B // nb, NC),
        in_specs=[pl.BlockSpec((nb, L, CG), lambda i, n: (i, n, offs["qkv"] // CG)),
                  pl.BlockSpec((nb, L, LANES), lambda i, n: (i, n, offs["small"] // LANES)),
                  pl.BlockSpec((nb, 1, GATE_ROWS, L), lambda i, n: (i, n, 0, 0)),
                  pl.BlockSpec((nb, CONV_W - 1, CG), lambda i, n: (i, 0, 0)),
                  pl.BlockSpec((nb, H, DK, DV), lambda i, n: (i, 0, 0, 0)),
                  pl.BlockSpec((CONV_W, CG), lambda i, n: (0, 0)),
                  pl.BlockSpec((2, LANES), lambda i, n: (0, 0)),
                  pl.BlockSpec((GATE_ROWS, 2), lambda i, n: (0, 0)),
                  pl.BlockSpec((1, V), lambda i, n: (0, 0))],
        out_specs=[pl.BlockSpec((nb, L, V), lambda i, n: (i, n, 0)),
                   pl.BlockSpec((nb, CONV_W - 1, CG), lambda i, n: (i, 0, 0)),
                   pl.BlockSpec((nb, H, DK, DV), lambda i, n: (i, 0, 0, 0))],
        out_shape=[jax.ShapeDtypeStruct((B, T_pad, V), F32),
                   jax.ShapeDtypeStruct(conv0.shape, F32),
                   jax.ShapeDtypeStruct(S0.shape, F32)],
        scratch_shapes=[pltpu.VMEM((nb, SUBLANES + L, CG), F32)],
        compiler_params=pltpu.CompilerParams(dimension_semantics=("parallel", "arbitrary"),
                                             vmem_limit_bytes=VMEM_LIMIT),
        name="gdn_chunk",
    )(proj, proj, smT, conv0, S0, cw, prow, pcol, gnorm)


def _mlstm_kernel(qk_ref, v_ref, sm_ref, smT_ref, C0_ref, n0_ref, m0_ref, prow_ref, pcol_ref, norm_ref,
                  h_ref, C_ref, n_ref, m_ref, *, nb, L, NC, T_valid, H, DK, DV):
    step = pl.program_id(1)
    K = H * DK
    g0 = 3 * H

    @pl.when(step == 0)
    def _():
        C_ref[...] = C0_ref[...]
        n_ref[...] = n0_ref[...]
        m_ref[...] = m0_ref[...]

    incl, _ = _tri_masks(L)
    lane = lax.broadcasted_iota(jnp.int32, (1, K), 1)
    lane1 = lax.broadcasted_iota(jnp.int32, (1, LANES), 1)
    rows = lax.broadcasted_iota(jnp.int32, (L, 1), 0)
    seqs = range(nb)

    sm = [sm_ref[b] for b in seqs]
    smT = [smT_ref[b, 0] for b in seqs]
    li_c = [x + prow_ref[0:1, :] for x in sm]
    li_r = [x + pcol_ref[:, 0:1] for x in smT]
    lf_c = [-_softplus(-(x + prow_ref[1:2, :])) for x in sm]
    lf_r = [-_softplus(-(x + pcol_ref[:, 1:2])) for x in smT]
    if T_valid < NC * L:
        vc = rows < T_valid
        vr = lax.broadcasted_iota(jnp.int32, (1, L), 1) < T_valid
        li_c = [jnp.where(vc, x, NEG_BIG) for x in li_c]
        li_r = [jnp.where(vr, x, NEG_BIG) for x in li_r]
        lf_c = [jnp.where(vc, x, 0.0) for x in lf_c]
        lf_r = [jnp.where(vr, x, 0.0) for x in lf_r]
    FF = [_cumsum_both(c, r, incl) for c, r in zip(lf_c, lf_r)]
    F_c = [f[0] for f in FF]
    F_r = [f[1] for f in FF]
    r_c = [pltpu.roll(x, H, axis=1) - f for x, f in zip(li_c, F_c)]
    cm = r_c
    s = 1
    while s < L:
        cm = [jnp.maximum(x, jnp.where(rows >= s, pltpu.roll(x, s, axis=0), -jnp.inf)) for x in cm]
        s *= 2
    m_old = [m_ref[b] for b in seqs]
    mx = [jnp.maximum(x, m) for x, m in zip(cm, m_old)]
    mx_last = [x[L - 1:L, :] for x in mx]
    dec = [jnp.exp(m - x) for m, x in zip(m_old, mx_last)]
    wC = [jnp.exp(r - x) for r, x in zip(r_c, mx_last)]
    a_all = [jnp.exp(m - x) for m, x in zip(m_old, mx)]
    floor = [jnp.exp(-f - x) for f, x in zip(F_c, mx)]
    for b in seqs:
        m_ref[b] = jnp.where((lane1 >= g0) & (lane1 < g0 + H), F_c[b][L - 1:L, :] + mx_last[b], 0.0)

    q_all = [qk_ref[b, :, :K] * (DK ** -0.5) for b in seqs]
    k_all = [qk_ref[b, :, K:] for b in seqs]
    k_bf = [x.astype(BF16) for x in k_all]
    C_all = [C_ref[b] for b in seqs]
    C_bf = [x.astype(BF16) for x in C_all]
    n_row = [n_ref[b] for b in seqs]

    half = lane1 < DK
    kw_cols, dec_lanes = [], []
    for b in seqs:
        cols = []
        for c in range(K // LANES):
            w_lo = wC[b][:, g0 + 2 * c:g0 + 2 * c + 1]
            w_hi = wC[b][:, g0 + 2 * c + 1:g0 + 2 * c + 2]
            cols.append(k_all[b][:, c * LANES:(c + 1) * LANES] * jnp.where(half, w_lo, w_hi))
        kw_cols.append(cols)
        d = dec[b][:, g0:g0 + 1]
        for h in range(1, H):
            d = jnp.where(lane >= h * DK, dec[b][:, g0 + h:g0 + h + 1], d)
        dec_lanes.append(d)
    for b in seqs:
        n_upd = jnp.concatenate([jnp.sum(x, axis=0, keepdims=True) for x in kw_cols[b]], axis=1)
        n_ref[b] = dec_lanes[b] * n_row[b] + n_upd

    chains = [(b, h) for b in seqs for h in range(H)]
    in_head = [(lane >= h * DK) & (lane < (h + 1) * DK) for h in range(H)]
    q_h = [jnp.where(in_head[h], q_all[b], 0.0) for b, h in chains]
    q_bf = [x.astype(BF16) for x in q_h]
    v_bf = [v_ref[b, :, h * DV:(h + 1) * DV].astype(BF16) for b, h in chains]
    kw = [kw_cols[b][h // 2][:, (h % 2) * DK:(h % 2 + 1) * DK].astype(BF16) for b, h in chains]
    expD = [jnp.exp(jnp.where(incl, (li_r[b][2 * H + h:2 * H + h + 1, :] - F_r[b][g0 + h:g0 + h + 1, :])
                              - mx[b][:, g0 + h:g0 + h + 1], -jnp.inf)) for b, h in chains]
    qn = [jnp.sum(x * n_row[b], axis=1, keepdims=True) for x, (b, h) in zip(q_h, chains)]

    qk = [_mm_nt(x, k_bf[b]) for x, (b, h) in zip(q_bf, chains)]
    Sm = [e * x for e, x in zip(expD, qk)]
    inter_state = [_mm(x, C_bf[b]) for x, (b, h) in zip(q_bf, chains)]
    intra = [_mm(x, v) for x, v in zip(Sm, v_bf)]
    upd = [_mm_tn(x, v) for x, v in zip(kw, v_bf)]
    rowsum = [jnp.sum(x, axis=1, keepdims=True) for x in Sm]
    for i, (b, h) in enumerate(chains):
        C_ref[b, h * DK:(h + 1) * DK, :] = dec[b][:, g0 + h:g0 + h + 1] * C_all[b][h * DK:(h + 1) * DK, :] + upd[i]
    a_h = [a_all[b][:, g0 + h:g0 + h + 1] for b, h in chains]
    den = [a * x + y for a, x, y in zip(a_h, qn, rowsum)]
    scale = [1.0 / jnp.maximum(jnp.abs(d), floor[b][:, g0 + h:g0 + h + 1]) for d, (b, h) in zip(den, chains)]
    hh = [(a * x + y) * sc for a, x, y, sc in zip(a_h, inter_state, intra, scale)]
    ms = [jnp.mean(x * x, axis=-1, keepdims=True) for x in hh]
    rs = [lax.rsqrt(x + EPS) for x in ms]
    for x, r, (b, h) in zip(hh, rs, chains):
        h_ref[b, :, h * DV:(h + 1) * DV] = x * r * norm_ref[:, h * DV:(h + 1) * DV]


def _mlstm(proj, smT, C0, n0, m0, prow, pcol, norm, *, NC, L, T_valid, offs):
    B, H, DK, DV = C0.shape
    K = H * DK
    V = H * DV
    T_pad = proj.shape[1]
    nb = _tile(B, max(1, CHAINS // H), 1)
    assert 2 * DK == LANES and 4 * H <= LANES
    m_lanes = jnp.pad(m0.reshape(B, 1, H), ((0, 0), (0, 0), (3 * H, LANES - 4 * H)))
    kern = functools.partial(_mlstm_kernel, nb=nb, L=L, NC=NC, T_valid=T_valid, H=H, DK=DK, DV=DV)
    outs = pl.pallas_call(
        kern,
        grid=(B // nb, NC),
        in_specs=[pl.BlockSpec((nb, L, 2 * K), lambda i, n: (i, n, offs["qk_m"] // (2 * K))),
                  pl.BlockSpec((nb, L, V), lambda i, n: (i, n, offs["v_m"] // V)),
                  pl.BlockSpec((nb, L, LANES), lambda i, n: (i, n, offs["small"] // LANES)),
                  pl.BlockSpec((nb, 1, GATE_ROWS, L), lambda i, n: (i, n, 0, 0)),
                  pl.BlockSpec((nb, K, DV), lambda i, n: (i, 0, 0)),
                  pl.BlockSpec((nb, 1, K), lambda i, n: (i, 0, 0)),
                  pl.BlockSpec((nb, 1, LANES), lambda i, n: (i, 0, 0)),
                  pl.BlockSpec((2, LANES), lambda i, n: (0, 0)),
                  pl.BlockSpec((GATE_ROWS, 2), lambda i, n: (0, 0)),
                  pl.BlockSpec((1, V), lambda i, n: (0, 0))],
        out_specs=[pl.BlockSpec((nb, L, V), lambda i, n: (i, n, 0)),
                   pl.BlockSpec((nb, K, DV), lambda i, n: (i, 0, 0)),
                   pl.BlockSpec((nb, 1, K), lambda i, n: (i, 0, 0)),
                   pl.BlockSpec((nb, 1, LANES), lambda i, n: (i, 0, 0))],
        out_shape=[jax.ShapeDtypeStruct((B, T_pad, V), F32),
                   jax.ShapeDtypeStruct((B, K, DV), F32),
                   jax.ShapeDtypeStruct((B, 1, K), F32),
                   jax.ShapeDtypeStruct((B, 1, LANES), F32)],
        compiler_params=pltpu.CompilerParams(dimension_semantics=("parallel", "arbitrary"),
                                             vmem_limit_bytes=VMEM_LIMIT),
        name="mlstm_chunk",
    )(proj, proj, proj, smT, C0.reshape(B, K, DV), n0.reshape(B, 1, K), m_lanes, prow, pcol, norm)
    hm, C_new, n_new, m_new = outs
    return hm, C_new.reshape(B, H, DK, DV), n_new.reshape(B, H, DK), m_new[:, 0, 3 * H:4 * H]


def _merge_kernel(og_ref, hm_ref, x_ref, g_ref, wg_ref, wbg_ref, wbm_ref, wout_ref, o_ref, *, Vg, Vm, D):
    x = x_ref[...]
    ms = jnp.mean(x * x, axis=-1, keepdims=True)
    xn = (x * lax.rsqrt(ms + EPS) * g_ref[...]).astype(BF16)
    z = jnp.dot(xn, wg_ref[:, 0:Vg], preferred_element_type=F32)
    og = (og_ref[...] * (z * _sigmoid(z))).astype(BF16)
    br_g = jnp.dot(og, wbg_ref[...], preferred_element_type=F32)
    om = jnp.dot(xn, wg_ref[:, Vg:Vg + Vm], preferred_element_type=F32)
    hm = (hm_ref[...] * _sigmoid(om)).astype(BF16)
    br_m = jnp.dot(hm, wbm_ref[...], preferred_element_type=F32)
    gg = jnp.dot(xn, wg_ref[:, Vg + Vm:Vg + Vm + D], preferred_element_type=F32)
    merged = _sigmoid(gg) * br_g
    gm = jnp.dot(xn, wg_ref[:, Vg + Vm + D:], preferred_element_type=F32)
    merged = merged + _sigmoid(gm) * br_m
    o_ref[...] = x + jnp.dot(merged.astype(BF16), wout_ref[...], preferred_element_type=F32)


def _merge(og, hm, x, gamma, wg, wbg, wbm, wout):
    M, D = x.shape
    Vg = og.shape[1]
    Vm = hm.shape[1]
    tm = _tile(M, 512, SUBLANES)
    return pl.pallas_call(
        functools.partial(_merge_kernel, Vg=Vg, Vm=Vm, D=D),
        grid=(M // tm,),
        in_specs=[pl.BlockSpec((tm, Vg), lambda i: (i, 0)),
                  pl.BlockSpec((tm, Vm), lambda i: (i, 0)),
                  pl.BlockSpec((tm, D), lambda i: (i, 0)),
                  pl.BlockSpec((1, D), lambda i: (0, 0)),
                  pl.BlockSpec((D, Vg + Vm + 2 * D), lambda i: (0, 0)),
                  pl.BlockSpec((Vg, D), lambda i: (0, 0)),
                  pl.BlockSpec((Vm, D), lambda i: (0, 0)),
                  pl.BlockSpec((D, D), lambda i: (0, 0))],
        out_specs=pl.BlockSpec((tm, D), lambda i: (i, 0)),
        out_shape=jax.ShapeDtypeStruct((M, D), F32),
        compiler_params=pltpu.CompilerParams(dimension_semantics=("parallel",), vmem_limit_bytes=VMEM_LIMIT),
        name="merge_out",
    )(og, hm, x, gamma, wg, wbg, wbm, wout)


def _mlp_kernel(x_ref, g_ref, wup_ref, wdn_ref, gf_ref, o_ref, xn_ref, acc_ref, *, final_norm):
    k = pl.program_id(1)

    @pl.when(k == 0)
    def _():
        x = x_ref[...]
        ms = jnp.mean(x * x, axis=-1, keepdims=True)
        xn_ref[...] = (x * lax.rsqrt(ms + EPS) * g_ref[...]).astype(BF16)
        acc_ref[...] = jnp.zeros_like(acc_ref)

    hcol = jnp.maximum(jnp.dot(xn_ref[...], wup_ref[...], preferred_element_type=F32), 0.0)
    acc_ref[...] += jnp.dot((hcol * hcol).astype(BF16), wdn_ref[...], preferred_element_type=F32)

    @pl.when(k == pl.num_programs(1) - 1)
    def _():
        y = x_ref[...] + acc_ref[...]
        if final_norm:
            ms = jnp.mean(y * y, axis=-1, keepdims=True)
            y = y * lax.rsqrt(ms + EPS) * gf_ref[...]
        o_ref[...] = y


def _mlp(x, gamma, wup, wdn, gamma_final, *, final_norm):
    M, D = x.shape
    FF = wup.shape[1]
    tm = _tile(M, 1024, SUBLANES)
    tf = _tile(FF, 1024, LANES)
    return pl.pallas_call(
        functools.partial(_mlp_kernel, final_norm=final_norm),
        grid=(M // tm, FF // tf),
        in_specs=[pl.BlockSpec((tm, D), lambda i, k: (i, 0)),
                  pl.BlockSpec((1, D), lambda i, k: (0, 0)),
                  pl.BlockSpec((D, tf), lambda i, k: (0, k)),
                  pl.BlockSpec((tf, D), lambda i, k: (k, 0)),
                  pl.BlockSpec((1, D), lambda i, k: (0, 0))],
        out_specs=pl.BlockSpec((tm, D), lambda i, k: (i, 0)),
        out_shape=jax.ShapeDtypeStruct((M, D), F32),
        scratch_shapes=[pltpu.VMEM((tm, D), BF16), pltpu.VMEM((tm, D), F32)],
        compiler_params=pltpu.CompilerParams(dimension_semantics=("parallel", "arbitrary"),
                                             vmem_limit_bytes=VMEM_LIMIT),
        name="mlp",
    )(x, gamma, wup, wdn, gamma_final)


def _pack_w_in(w_in, dims):
    D, Hg, DKg, DVg, Hm, DKm, DVm = dims
    Kg, Vg, Km, Vm = Hg * DKg, Hg * DVg, Hm * DKm, Hm * DVm
    sizes = (Kg, Kg, Vg, Vg, Hg, Hg, Km, Km, Vm, Vm, Hm, Hm, D, D)
    starts = [0]
    for s in sizes:
        starts.append(starts[-1] + s)
    col = lambda a, b: w_in[:, starts[a]:starts[b]]
    small = jnp.concatenate([col(4, 5), col(5, 6), col(10, 11), col(11, 12)], axis=1)
    assert 2 * Hg + 2 * Hm <= GATE_ROWS and Hg == Hm
    small = jnp.pad(small, ((0, 0), (0, LANES - small.shape[1])))
    segs = [("qkv", col(0, 3)), ("qk_m", col(6, 8)), ("v_m", col(8, 9)), ("small", small)]
    offs, off = {}, 0
    for name, seg in segs:
        assert off % seg.shape[1] == 0, (name, off, seg.shape)
        offs[name] = off
        off += seg.shape[1]
    w_rec = jnp.concatenate([s for _, s in segs], axis=1).astype(BF16)
    w_gate = jnp.concatenate([col(3, 4), col(9, 10), col(12, 14)], axis=1).astype(BF16)
    return w_rec, w_gate, offs


def _gate_params(H, first, lane_first, second, lane_second):
    row = (jnp.zeros((2, LANES), F32).at[0, lane_first:lane_first + H].set(first.astype(F32))
           .at[1, lane_second:lane_second + H].set(second.astype(F32)))
    return row, row[:, :GATE_ROWS].T


def _head_norm_row(w, H, DV):
    return jnp.broadcast_to(w.astype(F32).reshape(-1, DV), (H, DV)).reshape(1, H * DV)


def _trunk(x, conv0, S0, C0, n0, m0, P, *, B, T_valid, L):
    M, D = x.shape
    T_pad = M // B
    NC = T_pad // L
    depth = P["w_in"].shape[0]
    offs = P["offs"]
    new = ([], [], [], [], [])
    for l in range(depth):
        proj = _norm_proj(x, P["norm_mix"][l], P["w_in"][l])
        proj3 = proj.reshape(B, T_pad, proj.shape[1])
        smT = proj3[:, :, offs["small"]:offs["small"] + GATE_ROWS].reshape(B, NC, L, GATE_ROWS)
        smT = jnp.swapaxes(smT, 2, 3)
        og, conv_n, S_n = _gdn(proj3, smT, conv0[l], S0[l], P["conv_w"][l], P["gdn_prow"][l], P["gdn_pcol"][l],
                               P["gdn_norm"][l], NC=NC, L=L, T_valid=T_valid, offs=offs)
        hm, C_n, n_n, m_n = _mlstm(proj3, smT, C0[l], n0[l], m0[l], P["ml_prow"][l], P["ml_pcol"][l],
                                   P["ml_norm"][l], NC=NC, L=L, T_valid=T_valid, offs=offs)
        x = _merge(og.reshape(M, -1), hm.reshape(M, -1), x, P["norm_mix"][l], P["w_gate"][l], P["w_bg"][l],
                   P["w_bm"][l], P["w_out"][l])
        x = _mlp(x, P["norm_mlp"][l], P["w_up"][l], P["w_down"][l], P["norm_final"],
                 final_norm=(l == depth - 1))
        for lst, s in zip(new, (conv_n, S_n, C_n, n_n, m_n)):
            lst.append(s)
    return (x,) + tuple(jnp.stack(lst) for lst in new)


def kernel(x_prompt, x_sample, state_gdn_conv, state_gdn_S, state_mlstm_C, state_mlstm_n, state_mlstm_m, norm_mix, w_in, gdn_conv_w, gdn_A_log, gdn_dt_bias, gdn_norm, ml_i_bias, ml_f_bias, ml_norm, w_branch_gdn, w_branch_ml, w_out, norm_mlp, w_up, w_down, norm_final):
    Bp, Tp, D = x_prompt.shape
    Bs, Ts, _ = x_sample.shape
    depth = w_in.shape[0]
    _, _, Hg, DKg, DVg = state_gdn_S.shape
    _, _, Hm, DKm, DVm = state_mlstm_C.shape
    CG = state_gdn_conv.shape[-1]
    dims = (D, Hg, DKg, DVg, Hm, DKm, DVm)
    assert Ts >= CONV_W - 1 and Tp >= CONV_W - 1

    packed = [_pack_w_in(w_in[l], dims) for l in range(depth)]
    P = {
        "offs": packed[0][2],
        "w_in": jnp.stack([p[0] for p in packed]),
        "w_gate": jnp.stack([p[1] for p in packed]),
        "norm_mix": norm_mix.reshape(depth, 1, D),
        "norm_mlp": norm_mlp.reshape(depth, 1, D),
        "norm_final": norm_final.reshape(1, D),
        "conv_w": gdn_conv_w,
        "w_bg": w_branch_gdn.astype(BF16),
        "w_bm": w_branch_ml.astype(BF16),
        "w_out": w_out.astype(BF16),
        "w_up": w_up.astype(BF16),
        "w_down": w_down.astype(BF16),
        "gdn_norm": jnp.stack([_head_norm_row(gdn_norm[l], Hg, DVg) for l in range(depth)]),
        "ml_norm": jnp.stack([_head_norm_row(ml_norm[l], Hm, DVm) for l in range(depth)]),
    }
    gp = [_gate_params(Hg, gdn_A_log[l], Hg, gdn_dt_bias[l], Hg) for l in range(depth)]
    mp = [_gate_params(Hm, ml_i_bias[l], 2 * Hg, ml_f_bias[l], 2 * Hg + Hm) for l in range(depth)]
    P["gdn_prow"] = jnp.stack([g[0] for g in gp])
    P["gdn_pcol"] = jnp.stack([g[1] for g in gp])
    P["ml_prow"] = jnp.stack([g[0] for g in mp])
    P["ml_pcol"] = jnp.stack([g[1] for g in mp])

    Lp = CHUNK if Tp % CHUNK == 0 else Tp
    assert Lp % SUBLANES == 0
    zeros = lambda *s: jnp.zeros(s, F32)
    yp, conv_p, S_p, C_p, n_p, m_p = _trunk(
        x_prompt.reshape(Bp * Tp, D), zeros(depth, Bp, CONV_W - 1, CG), zeros(depth, Bp, Hg, DKg, DVg),
        zeros(depth, Bp, Hm, DKm, DVm), zeros(depth, Bp, Hm, DKm), zeros(depth, Bp, Hm), P,
        B=Bp, T_valid=Tp, L=Lp)

    Ls = -(-Ts // SUBLANES) * SUBLANES
    xs = jnp.pad(x_sample, ((0, 0), (0, Ls - Ts), (0, 0))).reshape(Bs * Ls, D)
    ys, conv_s, S_s, C_s, n_s, m_s = _trunk(
        xs, state_gdn_conv, state_gdn_S, state_mlstm_C, state_mlstm_n, state_mlstm_m, P,
        B=Bs, T_valid=Ts, L=Ls)
    ys = ys.reshape(Bs, Ls, D)[:, :Ts]
    return (yp.reshape(Bp, Tp, D), ys, conv_p, S_p, C_p, n_p, m_p, conv_s, S_s, C_s, n_s, m_s)
```

```python
import functools

import jax
import jax.numpy as jnp
from jax import lax
from jax.experimental import pallas as pl
from jax.experimental.pallas import tpu as pltpu

F32 = jnp.float32
BF16 = jnp.bfloat16
EPS = 1e-6
CONV_W = 4
LANES = 128
SUBLANES = 8
GATE_ROWS = 16
CHUNK = 64
CHAINS = 16
NEG_BIG = -1e30
VMEM_LIMIT = 48 * 1024 * 1024
HIGHEST = lax.Precision.HIGHEST


def _sigmoid(x):
    return 1.0 / (1.0 + jnp.exp(-x))


def _softplus(x):
    return jnp.maximum(x, 0.0) + jnp.log(1.0 + jnp.exp(-jnp.abs(x)))


def _mm(a, b):
    return jnp.dot(a.astype(BF16), b.astype(BF16), preferred_element_type=F32)


def _mm_nt(a, b):
    return lax.dot_general(a.astype(BF16), b.astype(BF16), (((1,), (1,)), ((), ())), preferred_element_type=F32)


def _mm_tn(a, b):
    return lax.dot_general(a.astype(BF16), b.astype(BF16), (((0,), (0,)), ((), ())), preferred_element_type=F32)


def _split_bf16(a):
    hi = a.astype(BF16)
    return hi, (a - hi.astype(F32)).astype(BF16)


def _tile(n, cap, mult):
    best = None
    for t in range(mult, min(n, cap) + 1, mult):
        if n % t == 0:
            best = t
    return best if best is not None else n


def _tri_masks(L):
    row = lax.broadcasted_iota(jnp.int32, (L, L), 0)
    col = lax.broadcasted_iota(jnp.int32, (L, L), 1)
    return row >= col, row > col


def _cumsum_both(g_col, g_row, incl):
    L = incl.shape[0]
    row = lax.broadcasted_iota(jnp.int32, (L, L), 0)
    col = lax.broadcasted_iota(jnp.int32, (L, L), 1)
    lower = jnp.where(incl, 1.0, 0.0)
    upper = jnp.where(row <= col, 1.0, 0.0)
    c_col = jnp.dot(lower, g_col, precision=HIGHEST, preferred_element_type=F32)
    c_row = jnp.dot(g_row, upper, precision=HIGHEST, preferred_element_type=F32)
    return c_col, c_row


def _tri_inv_unit_lower(As):
    L = As[0].shape[0]
    row = lax.broadcasted_iota(jnp.int32, (L, L), 0)
    col = lax.broadcasted_iota(jnp.int32, (L, L), 1)
    eye = jnp.where(row == col, 1.0, 0.0)
    levels = max(1, (L - 1).bit_length())
    Xs = [eye - A for A in As]
    if levels > 1:
        Ps = [_mm(A, A) for A in As]
        for _ in range(1, levels - 1):
            Rs = [_mm(jnp.concatenate([P, X], axis=0), P) for P, X in zip(Ps, Xs)]
            Ps = [R[:L] for R in Rs]
            Xs = [X + R[L:] for X, R in zip(Xs, Rs)]
        Xs = [X + _mm(X, P) for X, P in zip(Xs, Ps)]
    splits = [(_split_bf16(A), _split_bf16(X)) for A, X in zip(As, Xs)]
    AX1 = [jnp.dot(jnp.concatenate([a_hi, a_lo], axis=0), x_hi, preferred_element_type=F32)
           for (a_hi, a_lo), (x_hi, _) in splits]
    AX2 = [jnp.dot(a_hi, x_lo, preferred_element_type=F32) for (a_hi, _), (_, x_lo) in splits]
    Rs = [(eye - X) - (r1[:L] + (r1[L:] + r2)) for X, r1, r2 in zip(Xs, AX1, AX2)]
    return Xs, Rs


def _norm_proj_kernel(x_ref, g_ref, w_ref, o_ref):
    x = x_ref[...]
    ms = jnp.mean(x * x, axis=-1, keepdims=True)
    xn = (x * lax.rsqrt(ms + EPS) * g_ref[...]).astype(BF16)
    o_ref[...] = jnp.dot(xn, w_ref[...], preferred_element_type=F32)


def _norm_proj(x, gamma, w):
    M, D = x.shape
    N = w.shape[1]
    tm = _tile(M, 512, SUBLANES)
    return pl.pallas_call(
        _norm_proj_kernel,
        grid=(M // tm,),
        in_specs=[pl.BlockSpec((tm, D), lambda i: (i, 0)),
                  pl.BlockSpec((1, D), lambda i: (0, 0)),
                  pl.BlockSpec((D, N), lambda i: (0, 0))],
        out_specs=pl.BlockSpec((tm, N), lambda i: (i, 0)),
        out_shape=jax.ShapeDtypeStruct((M, N), F32),
        compiler_params=pltpu.CompilerParams(dimension_semantics=("parallel",), vmem_limit_bytes=VMEM_LIMIT),
        name="norm_proj",
    )(x, gamma, w)


def _gdn_kernel(qkv_ref, sm_ref, smT_ref, conv0_ref, S0_ref, cw_ref, prow_ref, pcol_ref, gnorm_ref,
                o_ref, convn_ref, S_ref, xp_ref, *, nb, L, NC, T_valid, H, DK, DV):
    n = pl.program_id(1)
    K = H * DK
    pad0 = SUBLANES - (CONV_W - 1)

    @pl.when(n == 0)
    def _():
        xp_ref[:, pad0:SUBLANES, :] = conv0_ref[...]
        S_ref[...] = S0_ref[...]

    incl, strict = _tri_masks(L)
    lv = L if NC > 1 else T_valid
    chains = []
    for b in range(nb):
        xp_ref[b, SUBLANES:SUBLANES + L, :] = qkv_ref[b]
        xfull = xp_ref[b]
        y = xfull * cw_ref[CONV_W - 1:CONV_W, :]
        for j in range(CONV_W - 1):
            y = y + pltpu.roll(xfull, CONV_W - 1 - j, axis=0) * cw_ref[j:j + 1, :]
        y = y[SUBLANES:, :]
        qkv = y * _sigmoid(y)
        tail = xp_ref[b, SUBLANES + lv - (CONV_W - 1):SUBLANES + lv, :]
        xp_ref[b, pad0:SUBLANES, :] = tail
        convn_ref[b] = tail

        sm = sm_ref[b]
        smT = smT_ref[b, 0]
        beta_c = _sigmoid(sm)
        g_c = -jnp.exp(prow_ref[0:1, :]) * _softplus(sm + prow_ref[1:2, :])
        g_r = -jnp.exp(pcol_ref[:, 0:1]) * _softplus(smT + pcol_ref[:, 1:2])
        if T_valid < NC * L:
            vc = lax.broadcasted_iota(jnp.int32, (L, 1), 0) < T_valid
            vr = lax.broadcasted_iota(jnp.int32, (1, L), 1) < T_valid
            beta_c = jnp.where(vc, beta_c, 0.0)
            g_c = jnp.where(vc, g_c, 0.0)
            g_r = jnp.where(vr, g_r, 0.0)
        gc_c, gc_r = _cumsum_both(g_c, g_r, incl)
        for h in range(H):
            qh = qkv[:, h * DK:(h + 1) * DK]
            kh = qkv[:, K + h * DK:K + (h + 1) * DK]
            vh = qkv[:, 2 * K + h * DV:2 * K + (h + 1) * DV]
            qh = qh * lax.rsqrt(jnp.sum(qh * qh, axis=-1, keepdims=True) + EPS) * (DK ** -0.5)
            kh = kh * lax.rsqrt(jnp.sum(kh * kh, axis=-1, keepdims=True) + EPS)
            b_c = beta_c[:, h:h + 1]
            gcc = gc_c[:, H + h:H + h + 1]
            gcr = gc_r[H + h:H + h + 1, :]
            gl = gcc[L - 1:L, :]
            eg = jnp.exp(gcc)
            kb = kh * b_c
            chains.append(dict(
                b=b, h=h, kh=kh.astype(BF16), kbq=jnp.concatenate([kb, qh], axis=0).astype(BF16),
                decay=jnp.exp(jnp.where(incl, gcc - gcr, -jnp.inf)),
                rhs=jnp.concatenate([kb * eg, vh * b_c], axis=1), qd=qh * eg,
                kd=(kh * jnp.exp(gl - gcc)).astype(BF16), dl=jnp.exp(gl)))

    for c in chains:
        r = _mm_nt(c["kbq"], c["kh"])
        c["A"] = jnp.where(strict, r[:L] * c["decay"], 0.0)
        c["qk"] = (r[L:] * c["decay"]).astype(BF16)
    Xs, Rs = _tri_inv_unit_lower([c["A"] for c in chains])
    corr = [_mm(R, c["rhs"]) for R, c in zip(Rs, chains)]
    wu = [_mm(X, c["rhs"] + cr) for X, c, cr in zip(Xs, chains, corr)]
    states = [S_ref[c["b"], c["h"]] for c in chains]
    wq = [_mm(jnp.concatenate([x[:, :DK], c["qd"]], axis=0), S) for x, c, S in zip(wu, chains, states)]
    v_new = [(x[:, DK:] - y[:L]).astype(BF16) for x, y in zip(wu, wq)]
    o_intra = [_mm(c["qk"], v) for c, v in zip(chains, v_new)]
    s_upd = [_mm_tn(c["kd"], v) for c, v in zip(chains, v_new)]
    for c, S, su in zip(chains, states, s_upd):
        S_ref[c["b"], c["h"]] = S * c["dl"] + su
    o = [y[L:] + oi for y, oi in zip(wq, o_intra)]
    rs = [lax.rsqrt(jnp.mean(x * x, axis=-1, keepdims=True) + EPS) for x in o]
    for x, r, c in zip(o, rs, chains):
        o_ref[c["b"], :, c["h"] * DV:(c["h"] + 1) * DV] = x * r * gnorm_ref[:, c["h"] * DV:(c["h"] + 1) * DV]


def _skip_ref(kernel_fn, index):
    def wrapped(*refs, **kw):
        kernel_fn(*refs[:index], *refs[index + 1:], **kw)
    return wrapped


def _gdn(proj, smT, conv0, S0_all, layer, S_prev, cw, prow, pcol, gnorm, *, NC, L, T_valid, offs):
    depth, B, H, DK, DV = S0_all.shape
    CG = conv0.shape[-1]
    V = H * DV
    T_pad = proj.shape[1]
    nb = _tile(B, max(1, CHAINS // H), 1)
    kern = functools.partial(_gdn_kernel, nb=nb, L=L, NC=NC, T_valid=T_valid, H=H, DK=DK, DV=DV)
    in_specs = [pl.BlockSpec((nb, L, CG), lambda i, n: (i, n, offs["qkv"] // CG)),
                pl.BlockSpec((nb, L, LANES), lambda i, n: (i, n, offs["small"] // LANES)),
                pl.BlockSpec((nb, 1, GATE_ROWS, L), lambda i, n: (i, n, 0, 0)),
                pl.BlockSpec((nb, CONV_W - 1, CG), lambda i, n: (i, 0, 0)),
                pl.BlockSpec((None, nb, H, DK, DV), lambda i, n: (layer, i, 0, 0, 0)),
                pl.BlockSpec((CONV_W, CG), lambda i, n: (0, 0)),
                pl.BlockSpec((2, LANES), lambda i, n: (0, 0)),
                pl.BlockSpec((GATE_ROWS, 2), lambda i, n: (0, 0)),
                pl.BlockSpec((1, V), lambda i, n: (0, 0))]
    operands = [proj, proj, smT, conv0, S0_all, cw, prow, pcol, gnorm]
    aliases = {}
    if S_prev is not None:
        kern = _skip_ref(kern, len(operands))
        aliases = {len(operands): 2}
        in_specs.append(pl.BlockSpec(memory_space=pl.ANY))
        operands.append(S_prev)
    return pl.pallas_call(
        kern,
        grid=(B // nb, NC),
        in_specs=in_specs,
        out_specs=[pl.BlockSpec((nb, L, V), lambda i, n: (i, n, 0)),
                   pl.BlockSpec((nb, CONV_W - 1, CG), lambda i, n: (i, 0, 0)),
                   pl.BlockSpec((None, nb, H, DK, DV), lambda i, n: (layer, i, 0, 0, 0))],
        out_shape=[jax.ShapeDtypeStruct((B, T_pad, V), F32),
                   jax.ShapeDtypeStruct(conv0.shape, F32),
                   jax.ShapeDtypeStruct(S0_all.shape, F32)],
        scratch_shapes=[pltpu.VMEM((nb, SUBLANES + L, CG), F32)],
        input_output_aliases=aliases,
        compiler_params=pltpu.CompilerParams(dimension_semantics=("parallel", "arbitrary"),
                                             vmem_limit_bytes=VMEM_LIMIT),
        name="gdn_chunk",
    )(*operands)


def _mlstm_kernel(qk_ref, v_ref, sm_ref, smT_ref, C0_ref, n0_ref, m0_ref, prow_ref, pcol_ref, norm_ref,
                  h_ref, C_ref, n_ref, m_ref, *, nb, L, NC, T_valid, H, DK, DV):
    step = pl.program_id(1)
    K = H * DK
    g0 = 3 * H

    @pl.when(step == 0)
    def _():
        C_ref[...] = C0_ref[...]
        n_ref[...] = n0_ref[...]
        m_ref[...] = m0_ref[...]

    incl, _ = _tri_masks(L)
    lane = lax.broadcasted_iota(jnp.int32, (1, K), 1)
    lane1 = lax.broadcasted_iota(jnp.int32, (1, LANES), 1)
    rows = lax.broadcasted_iota(jnp.int32, (L, 1), 0)
    seqs = range(nb)

    sm = [sm_ref[b] for b in seqs]
    smT = [smT_ref[b, 0] for b in seqs]
    li_c = [x + prow_ref[0:1, :] for x in sm]
    li_r = [x + pcol_ref[:, 0:1] for x in smT]
    lf_c = [-_softplus(-(x + prow_ref[1:2, :])) for x in sm]
    lf_r = [-_softplus(-(x + pcol_ref[:, 1:2])) for x in smT]
    if T_valid < NC * L:
        vc = rows < T_valid
        vr = lax.broadcasted_iota(jnp.int32, (1, L), 1) < T_valid
        li_c = [jnp.where(vc, x, NEG_BIG) for x in li_c]
        li_r = [jnp.where(vr, x, NEG_BIG) for x in li_r]
        lf_c = [jnp.where(vc, x, 0.0) for x in lf_c]
        lf_r = [jnp.where(vr, x, 0.0) for x in lf_r]
    FF = [_cumsum_both(c, r, incl) for c, r in zip(lf_c, lf_r)]
    F_c = [f[0] for f in FF]
    F_r = [f[1] for f in FF]
    r_c = [pltpu.roll(x, H, axis=1) - f for x, f in zip(li_c, F_c)]
    cm = r_c
    s = 1
    while s < L:
        cm = [jnp.maximum(x, jnp.where(rows >= s, pltpu.roll(x, s, axis=0), -jnp.inf)) for x in cm]
        s *= 2
    m_old = [m_ref[b] for b in seqs]
    mx = [jnp.maximum(x, m) for x, m in zip(cm, m_old)]
    mx_last = [x[L - 1:L, :] for x in mx]
    dec = [jnp.exp(m - x) for m, x in zip(m_old, mx_last)]
    wC = [jnp.exp(r - x) for r, x in zip(r_c, mx_last)]
    a_all = [jnp.exp(m - x) for m, x in zip(m_old, mx)]
    floor = [jnp.exp(-f - x) for f, x in zip(F_c, mx)]
    for b in seqs:
        m_ref[b] = jnp.where((lane1 >= g0) & (lane1 < g0 + H), F_c[b][L - 1:L, :] + mx_last[b], 0.0)

    q_all = [qk_ref[b, :, :K] * (DK ** -0.5) for b in seqs]
    k_all = [qk_ref[b, :, K:] for b in seqs]
    k_bf = [x.astype(BF16) for x in k_all]
    C_all = [C_ref[b] for b in seqs]
    C_bf = [x.astype(BF16) for x in C_all]
    n_row = [n_ref[b] for b in seqs]

    half = lane1 < DK
    kw_cols, dec_lanes = [], []
    for b in seqs:
        cols = []
        for c in range(K // LANES):
            w_lo = wC[b][:, g0 + 2 * c:g0 + 2 * c + 1]
            w_hi = wC[b][:, g0 + 2 * c + 1:g0 + 2 * c + 2]
            cols.append(k_all[b][:, c * LANES:(c + 1) * LANES] * jnp.where(half, w_lo, w_hi))
        kw_cols.append(cols)
        d = dec[b][:, g0:g0 + 1]
        for h in range(1, H):
            d = jnp.where(lane >= h * DK, dec[b][:, g0 + h:g0 + h + 1], d)
        dec_lanes.append(d)
    for b in seqs:
        n_upd = jnp.concatenate([jnp.sum(x, axis=0, keepdims=True) for x in kw_cols[b]], axis=1)
        n_ref[b] = dec_lanes[b] * n_row[b] + n_upd

    chains = [(b, h) for b in seqs for h in range(H)]
    in_head = [(lane >= h * DK) & (lane < (h + 1) * DK) for h in range(H)]
    q_h = [jnp.where(in_head[h], q_all[b], 0.0) for b, h in chains]
    q_bf = [x.astype(BF16) for x in q_h]
    v_bf = [v_ref[b, :, h * DV:(h + 1) * DV].astype(BF16) for b, h in chains]
    kw = [kw_cols[b][h // 2][:, (h % 2) * DK:(h % 2 + 1) * DK].astype(BF16) for b, h in chains]
    expD = [jnp.exp(jnp.where(incl, (li_r[b][2 * H + h:2 * H + h + 1, :] - F_r[b][g0 + h:g0 + h + 1, :])
                              - mx[b][:, g0 + h:g0 + h + 1], -jnp.inf)) for b, h in chains]
    qn = [jnp.sum(x * n_row[b], axis=1, keepdims=True) for x, (b, h) in zip(q_h, chains)]

    qk = [_mm_nt(x, k_bf[b]) for x, (b, h) in zip(q_bf, chains)]
    Sm = [e * x for e, x in zip(expD, qk)]
    inter_state = [_mm(x, C_bf[b]) for x, (b, h) in zip(q_bf, chains)]
    intra = [_mm(x, v) for x, v in zip(Sm, v_bf)]
    upd = [_mm_tn(x, v) for x, v in zip(kw, v_bf)]
    rowsum = [jnp.sum(x, axis=1, keepdims=True) for x in Sm]
    for i, (b, h) in enumerate(chains):
        C_ref[b, h * DK:(h + 1) * DK, :] = dec[b][:, g0 + h:g0 + h + 1] * C_all[b][h * DK:(h + 1) * DK, :] + upd[i]
    a_h = [a_all[b][:, g0 + h:g0 + h + 1] for b, h in chains]
    den = [a * x + y for a, x, y in zip(a_h, qn, rowsum)]
    scale = [1.0 / jnp.maximum(jnp.abs(d), floor[b][:, g0 + h:g0 + h + 1]) for d, (b, h) in zip(den, chains)]
    hh = [(a * x + y) * sc for a, x, y, sc in zip(a_h, inter_state, intra, scale)]
    ms = [jnp.mean(x * x, axis=-1, keepdims=True) for x in hh]
    rs = [lax.rsqrt(x + EPS) for x in ms]
    for x, r, (b, h) in zip(hh, rs, chains):
        h_ref[b, :, h * DV:(h + 1) * DV] = x * r * norm_ref[:, h * DV:(h + 1) * DV]


def _mlstm(proj, smT, C0_all, layer, C_prev, n0, m0, prow, pcol, norm, *, NC, L, T_valid, offs):
    depth, B, K, DV = C0_all.shape
    H = m0.shape[-1]
    DK = K // H
    V = H * DV
    T_pad = proj.shape[1]
    nb = _tile(B, max(1, CHAINS // H), 1)
    assert 2 * DK == LANES and 4 * H <= LANES
    m_lanes = jnp.pad(m0.reshape(B, 1, H), ((0, 0), (0, 0), (3 * H, LANES - 4 * H)))
    kern = functools.partial(_mlstm_kernel, nb=nb, L=L, NC=NC, T_valid=T_valid, H=H, DK=DK, DV=DV)
    in_specs = [pl.BlockSpec((nb, L, 2 * K), lambda i, n: (i, n, offs["qk_m"] // (2 * K))),
                pl.BlockSpec((nb, L, V), lambda i, n: (i, n, offs["v_m"] // V)),
                pl.BlockSpec((nb, L, LANES), lambda i, n: (i, n, offs["small"] // LANES)),
                pl.BlockSpec((nb, 1, GATE_ROWS, L), lambda i, n: (i, n, 0, 0)),
                pl.BlockSpec((None, nb, K, DV), lambda i, n: (layer, i, 0, 0)),
                pl.BlockSpec((nb, 1, K), lambda i, n: (i, 0, 0)),
                pl.BlockSpec((nb, 1, LANES), lambda i, n: (i, 0, 0)),
                pl.BlockSpec((2, LANES), lambda i, n: (0, 0)),
                pl.BlockSpec((GATE_ROWS, 2), lambda i, n: (0, 0)),
                pl.BlockSpec((1, V), lambda i, n: (0, 0))]
    operands = [proj, proj, proj, smT, C0_all, n0.reshape(B, 1, K), m_lanes, prow, pcol, norm]
    aliases = {}
    if C_prev is not None:
        kern = _skip_ref(kern, len(operands))
        aliases = {len(operands): 1}
        in_specs.append(pl.BlockSpec(memory_space=pl.ANY))
        operands.append(C_prev)
    outs = pl.pallas_call(
        kern,
        grid=(B // nb, NC),
        in_specs=in_specs,
        out_specs=[pl.BlockSpec((nb, L, V), lambda i, n: (i, n, 0)),
                   pl.BlockSpec((None, nb, K, DV), lambda i, n: (layer, i, 0, 0)),
                   pl.BlockSpec((nb, 1, K), lambda i, n: (i, 0, 0)),
                   pl.BlockSpec((nb, 1, LANES), lambda i, n: (i, 0, 0))],
        out_shape=[jax.ShapeDtypeStruct((B, T_pad, V), F32),
                   jax.ShapeDtypeStruct(C0_all.shape, F32),
                   jax.ShapeDtypeStruct((B, 1, K), F32),
                   jax.ShapeDtypeStruct((B, 1, LANES), F32)],
        input_output_aliases=aliases,
        compiler_params=pltpu.CompilerParams(dimension_semantics=("parallel", "arbitrary"),
                                             vmem_limit_bytes=VMEM_LIMIT),
        name="mlstm_chunk",
    )(*operands)
    hm, C_all, n_new, m_new = outs
    return hm, C_all, n_new.reshape(B, H, DK), m_new[:, 0, 3 * H:4 * H]


def _merge_kernel(og_ref, hm_ref, x_ref, g_ref, wg_ref, wbg_ref, wbm_ref, wout_ref, o_ref, *, Vg, Vm, D):
    x = x_ref[...]
    ms = jnp.mean(x * x, axis=-1, keepdims=True)
    xn = (x * lax.rsqrt(ms + EPS) * g_ref[...]).astype(BF16)
    z = jnp.dot(xn, wg_ref[:, 0:Vg], preferred_element_type=F32)
    og = (og_ref[...] * (z * _sigmoid(z))).astype(BF16)
    br_g = jnp.dot(og, wbg_ref[...], preferred_element_type=F32)
    om = jnp.dot(xn, wg_ref[:, Vg:Vg + Vm], preferred_element_type=F32)
    hm = (hm_ref[...] * _sigmoid(om)).astype(BF16)
    br_m = jnp.dot(hm, wbm_ref[...], preferred_element_type=F32)
    gg = jnp.dot(xn, wg_ref[:, Vg + Vm:Vg + Vm + D], preferred_element_type=F32)
    merged = _sigmoid(gg) * br_g
    gm = jnp.dot(xn, wg_ref[:, Vg + Vm + D:], preferred_element_type=F32)
    merged = merged + _sigmoid(gm) * br_m
    o_ref[...] = x + jnp.dot(merged.astype(BF16), wout_ref[...], preferred_element_type=F32)


def _merge(og, hm, x, gamma, wg, wbg, wbm, wout):
    M, D = x.shape
    Vg = og.shape[1]
    Vm = hm.shape[1]
    tm = _tile(M, 512, SUBLANES)
    return pl.pallas_call(
        functools.partial(_merge_kernel, Vg=Vg, Vm=Vm, D=D),
        grid=(M // tm,),
        in_specs=[pl.BlockSpec((tm, Vg), lambda i: (i, 0)),
                  pl.BlockSpec((tm, Vm), lambda i: (i, 0)),
                  pl.BlockSpec((tm, D), lambda i: (i, 0)),
                  pl.BlockSpec((1, D), lambda i: (0, 0)),
                  pl.BlockSpec((D, Vg + Vm + 2 * D), lambda i: (0, 0)),
                  pl.BlockSpec((Vg, D), lambda i: (0, 0)),
                  pl.BlockSpec((Vm, D), lambda i: (0, 0)),
                  pl.BlockSpec((D, D), lambda i: (0, 0))],
        out_specs=pl.BlockSpec((tm, D), lambda i: (i, 0)),
        out_shape=jax.ShapeDtypeStruct((M, D), F32),
        compiler_params=pltpu.CompilerParams(dimension_semantics=("parallel",), vmem_limit_bytes=VMEM_LIMIT),
        name="merge_out",
    )(og, hm, x, gamma, wg, wbg, wbm, wout)


def _mlp_kernel(x_ref, g_ref, wup_ref, wdn_ref, gf_ref, o_ref, *, final_norm, tf):
    x = x_ref[...]
    ms = jnp.mean(x * x, axis=-1, keepdims=True)
    xn = (x * lax.rsqrt(ms + EPS) * g_ref[...]).astype(BF16)
    y = x
    for c in range(wup_ref.shape[1] // tf):
        hcol = jnp.maximum(jnp.dot(xn, wup_ref[:, c * tf:(c + 1) * tf], preferred_element_type=F32), 0.0)
        y = y + jnp.dot((hcol * hcol).astype(BF16), wdn_ref[c * tf:(c + 1) * tf, :], preferred_element_type=F32)
    if final_norm:
        ms = jnp.mean(y * y, axis=-1, keepdims=True)
        y = y * lax.rsqrt(ms + EPS) * gf_ref[...]
    o_ref[...] = y


def _mlp(x, gamma, wup, wdn, gamma_final, *, final_norm):
    M, D = x.shape
    FF = wup.shape[1]
    tm = _tile(M, 512, SUBLANES)
    tf = _tile(FF, 1024, LANES)
    resident = pl.Buffered(1)
    return pl.pallas_call(
        functools.partial(_mlp_kernel, final_norm=final_norm, tf=tf),
        grid=(M // tm,),
        in_specs=[pl.BlockSpec((tm, D), lambda i: (i, 0)),
                  pl.BlockSpec((1, D), lambda i: (0, 0)),
                  pl.BlockSpec((D, FF), lambda i: (0, 0), pipeline_mode=resident),
                  pl.BlockSpec((FF, D), lambda i: (0, 0), pipeline_mode=resident),
                  pl.BlockSpec((1, D), lambda i: (0, 0))],
        out_specs=pl.BlockSpec((tm, D), lambda i: (i, 0)),
        out_shape=jax.ShapeDtypeStruct((M, D), F32),
        compiler_params=pltpu.CompilerParams(dimension_semantics=("parallel",), vmem_limit_bytes=VMEM_LIMIT),
        name="mlp",
    )(x, gamma, wup, wdn, gamma_final)


def _pack_w_in(w_in, dims):
    D, Hg, DKg, DVg, Hm, DKm, DVm = dims
    Kg, Vg, Km, Vm = Hg * DKg, Hg * DVg, Hm * DKm, Hm * DVm
    sizes = (Kg, Kg, Vg, Vg, Hg, Hg, Km, Km, Vm, Vm, Hm, Hm, D, D)
    starts = [0]
    for s in sizes:
        starts.append(starts[-1] + s)
    col = lambda a, b: w_in[:, starts[a]:starts[b]]
    small = jnp.concatenate([col(4, 5), col(5, 6), col(10, 11), col(11, 12)], axis=1)
    assert 2 * Hg + 2 * Hm <= GATE_ROWS and Hg == Hm
    small = jnp.pad(small, ((0, 0), (0, LANES - small.shape[1])))
    segs = [("qkv", col(0, 3)), ("qk_m", col(6, 8)), ("v_m", col(8, 9)), ("small", small)]
    offs, off = {}, 0
    for name, seg in segs:
        assert off % seg.shape[1] == 0, (name, off, seg.shape)
        offs[name] = off
        off += seg.shape[1]
    w_rec = jnp.concatenate([s for _, s in segs], axis=1).astype(BF16)
    w_gate = jnp.concatenate([col(3, 4), col(9, 10), col(12, 14)], axis=1).astype(BF16)
    return w_rec, w_gate, offs


def _gate_params(H, first, lane_first, second, lane_second):
    row = (jnp.zeros((2, LANES), F32).at[0, lane_first:lane_first + H].set(first.astype(F32))
           .at[1, lane_second:lane_second + H].set(second.astype(F32)))
    return row, row[:, :GATE_ROWS].T


def _head_norm_row(w, H, DV):
    return jnp.broadcast_to(w.astype(F32).reshape(-1, DV), (H, DV)).reshape(1, H * DV)


def _trunk(x, conv0, S0, C0, n0, m0, P, *, B, T_valid, L):
    M, D = x.shape
    T_pad = M // B
    NC = T_pad // L
    depth = P["w_in"].shape[0]
    offs = P["offs"]
    _, _, H, DK, DV = C0.shape
    C0_all = C0.reshape(depth, B, H * DK, DV)
    new = ([], [], [])
    S_all, C_all = None, None
    for l in range(depth):
        proj = _norm_proj(x, P["norm_mix"][l], P["w_in"][l])
        proj3 = proj.reshape(B, T_pad, proj.shape[1])
        smT = proj3[:, :, offs["small"]:offs["small"] + GATE_ROWS].reshape(B, NC, L, GATE_ROWS)
        smT = jnp.swapaxes(smT, 2, 3)
        og, conv_n, S_all = _gdn(proj3, smT, conv0[l], S0, l, S_all, P["conv_w"][l], P["gdn_prow"][l],
                                 P["gdn_pcol"][l], P["gdn_norm"][l], NC=NC, L=L, T_valid=T_valid, offs=offs)
        hm, C_all, n_n, m_n = _mlstm(proj3, smT, C0_all, l, C_all, n0[l], m0[l], P["ml_prow"][l], P["ml_pcol"][l],
                                     P["ml_norm"][l], NC=NC, L=L, T_valid=T_valid, offs=offs)
        x = _merge(og.reshape(M, -1), hm.reshape(M, -1), x, P["norm_mix"][l], P["w_gate"][l], P["w_bg"][l],
                   P["w_bm"][l], P["w_out"][l])
        x = _mlp(x, P["norm_mlp"][l], P["w_up"][l], P["w_down"][l], P["norm_final"],
                 final_norm=(l == depth - 1))
        for lst, s in zip(new, (conv_n, n_n, m_n)):
            lst.append(s)
    conv_new, n_new, m_new = (jnp.stack(lst) for lst in new)
    return x, conv_new, S_all, C_all.reshape(C0.shape), n_new, m_new


def kernel(x_prompt, x_sample, state_gdn_conv, state_gdn_S, state_mlstm_C, state_mlstm_n, state_mlstm_m, norm_mix, w_in, gdn_conv_w, gdn_A_log, gdn_dt_bias, gdn_norm, ml_i_bias, ml_f_bias, ml_norm, w_branch_gdn, w_branch_ml, w_out, norm_mlp, w_up, w_down, norm_final):
    Bp, Tp, D = x_prompt.shape
    Bs, Ts, _ = x_sample.shape
    depth = w_in.shape[0]
    _, _, Hg, DKg, DVg = state_gdn_S.shape
    _, _, Hm, DKm, DVm = state_mlstm_C.shape
    CG = state_gdn_conv.shape[-1]
    dims = (D, Hg, DKg, DVg, Hm, DKm, DVm)
    assert Ts >= CONV_W - 1 and Tp >= CONV_W - 1

    packed = [_pack_w_in(w_in[l], dims) for l in range(depth)]
    P = {
        "offs": packed[0][2],
        "w_in": jnp.stack([p[0] for p in packed]),
        "w_gate": jnp.stack([p[1] for p in packed]),
        "norm_mix": norm_mix.reshape(depth, 1, D),
        "norm_mlp": norm_mlp.reshape(depth, 1, D),
        "norm_final": norm_final.reshape(1, D),
        "conv_w": gdn_conv_w,
        "w_bg": w_branch_gdn.astype(BF16),
        "w_bm": w_branch_ml.astype(BF16),
        "w_out": w_out.astype(BF16),
        "w_up": w_up.astype(BF16),
        "w_down": w_down.astype(BF16),
        "gdn_norm": jnp.stack([_head_norm_row(gdn_norm[l], Hg, DVg) for l in range(depth)]),
        "ml_norm": jnp.stack([_head_norm_row(ml_norm[l], Hm, DVm) for l in range(depth)]),
    }
    gp = [_gate_params(Hg, gdn_A_log[l], Hg, gdn_dt_bias[l], Hg) for l in range(depth)]
    mp = [_gate_params(Hm, ml_i_bias[l], 2 * Hg, ml_f_bias[l], 2 * Hg + Hm) for l in range(depth)]
    P["gdn_prow"] = jnp.stack([g[0] for g in gp])
    P["gdn_pcol"] = jnp.stack([g[1] for g in gp])
    P["ml_prow"] = jnp.stack([g[0] for g in mp])
    P["ml_pcol"] = jnp.stack([g[1] for g in mp])

    Lp = CHUNK if Tp % CHUNK == 0 else Tp
    assert Lp % SUBLANES == 0
    zeros = lambda *s: jnp.zeros(s, F32)
    yp, conv_p, S_p, C_p, n_p, m_p = _trunk(
        x_prompt.reshape(Bp * Tp, D), zeros(depth, Bp, CONV_W - 1, CG), zeros(depth, Bp, Hg, DKg, DVg),
        zeros(depth, Bp, Hm, DKm, DVm), zeros(depth, Bp, Hm, DKm), zeros(depth, Bp, Hm), P,
        B=Bp, T_valid=Tp, L=Lp)

    Ls = -(-Ts // SUBLANES) * SUBLANES
    xs = jnp.pad(x_sample, ((0, 0), (0, Ls - Ts), (0, 0))).reshape(Bs * Ls, D)
    ys, conv_s, S_s, C_s, n_s, m_s = _trunk(
        xs, state_gdn_conv, state_gdn_S, state_mlstm_C, state_mlstm_n, state_mlstm_m, P,
        B=Bs, T_valid=Ts, L=Ls)
    ys = ys.reshape(Bs, Ls, D)[:, :Ts]
    return (yp.reshape(Bp, Tp, D), ys, conv_p, S_p, C_p, n_p, m_p, conv_s, S_s, C_s, n_s, m_s)
```

```python
import functools

import jax
import jax.numpy as jnp
from jax import lax
from jax.experimental import pallas as pl
from jax.experimental.pallas import tpu as pltpu

F32 = jnp.float32
BF16 = jnp.bfloat16
EPS = 1e-6
CONV_W = 4
LANES = 128
SUBLANES = 8
GATE_ROWS = 16
CHUNK = 64
CHAINS = 16
NEG_BIG = -1e30
VMEM_LIMIT = 48 * 1024 * 1024
HIGHEST = lax.Precision.HIGHEST


def _sigmoid(x):
    return 1.0 / (1.0 + jnp.exp(-x))


def _softplus(x):
    return jnp.maximum(x, 0.0) + jnp.log(1.0 + jnp.exp(-jnp.abs(x)))


def _mm(a, b):
    return jnp.dot(a.astype(BF16), b.astype(BF16), preferred_element_type=F32)


def _mm_nt(a, b):
    return lax.dot_general(a.astype(BF16), b.astype(BF16), (((1,), (1,)), ((), ())), preferred_element_type=F32)


def _mm_tn(a, b):
    return lax.dot_general(a.astype(BF16), b.astype(BF16), (((0,), (0,)), ((), ())), preferred_element_type=F32)


def _split_bf16(a):
    hi = a.astype(BF16)
    return hi, (a - hi.astype(F32)).astype(BF16)


def _tile(n, cap, mult):
    best = None
    for t in range(mult, min(n, cap) + 1, mult):
        if n % t == 0:
            best = t
    return best if best is not None else n


def _tri_masks(L):
    row = lax.broadcasted_iota(jnp.int32, (L, L), 0)
    col = lax.broadcasted_iota(jnp.int32, (L, L), 1)
    return row >= col, row > col


def _cumsum_both(g_col, g_row, incl):
    L = incl.shape[0]
    row = lax.broadcasted_iota(jnp.int32, (L, L), 0)
    col = lax.broadcasted_iota(jnp.int32, (L, L), 1)
    lower = jnp.where(incl, 1.0, 0.0)
    upper = jnp.where(row <= col, 1.0, 0.0)
    c_col = jnp.dot(lower, g_col, precision=HIGHEST, preferred_element_type=F32)
    c_row = jnp.dot(g_row, upper, precision=HIGHEST, preferred_element_type=F32)
    return c_col, c_row


def _tri_inv_unit_lower(As):
    L = As[0].shape[0]
    row = lax.broadcasted_iota(jnp.int32, (L, L), 0)
    col = lax.broadcasted_iota(jnp.int32, (L, L), 1)
    eye = jnp.where(row == col, 1.0, 0.0)
    levels = max(1, (L - 1).bit_length())
    Xs = [eye - A for A in As]
    if levels > 1:
        Ps = [_mm(A, A) for A in As]
        for _ in range(1, levels - 1):
            Rs = [_mm(jnp.concatenate([P, X], axis=0), P) for P, X in zip(Ps, Xs)]
            Ps = [R[:L] for R in Rs]
            Xs = [X + R[L:] for X, R in zip(Xs, Rs)]
        Xs = [X + _mm(X, P) for X, P in zip(Xs, Ps)]
    splits = [(_split_bf16(A), _split_bf16(X)) for A, X in zip(As, Xs)]
    AX1 = [jnp.dot(jnp.concatenate([a_hi, a_lo], axis=0), x_hi, preferred_element_type=F32)
           for (a_hi, a_lo), (x_hi, _) in splits]
    AX2 = [jnp.dot(a_hi, x_lo, preferred_element_type=F32) for (a_hi, _), (_, x_lo) in splits]
    Rs = [(eye - X) - (r1[:L] + (r1[L:] + r2)) for X, r1, r2 in zip(Xs, AX1, AX2)]
    return Xs, Rs


def _layer_spec(layer, *shape):
    return pl.BlockSpec((None,) + shape, lambda *_: (layer,) + (0,) * len(shape))


def _norm_proj_kernel(x_ref, g_ref, w_ref, o_ref, s_ref):
    x = x_ref[...]
    ms = jnp.mean(x * x, axis=-1, keepdims=True)
    xn = (x * lax.rsqrt(ms + EPS) * g_ref[...]).astype(BF16)
    res = jnp.dot(xn, w_ref[...], preferred_element_type=F32)
    o_ref[...] = res[:, :o_ref.shape[1]]
    s_ref[...] = res[:, o_ref.shape[1]:]


def _norm_proj(x, gamma_all, w_all, layer):
    M, D = x.shape
    N = w_all.shape[2]
    tm = _tile(M, 512, SUBLANES)
    return pl.pallas_call(
        _norm_proj_kernel,
        grid=(M // tm,),
        in_specs=[pl.BlockSpec((tm, D), lambda i: (i, 0)),
                  _layer_spec(layer, 1, D),
                  _layer_spec(layer, D, N)],
        out_specs=[pl.BlockSpec((tm, N - LANES), lambda i: (i, 0)),
                   pl.BlockSpec((tm, LANES), lambda i: (i, 0))],
        out_shape=[jax.ShapeDtypeStruct((M, N - LANES), F32),
                   jax.ShapeDtypeStruct((M, LANES), F32)],
        compiler_params=pltpu.CompilerParams(dimension_semantics=("parallel",), vmem_limit_bytes=VMEM_LIMIT),
        name="norm_proj",
    )(x, gamma_all, w_all)


def _gdn_kernel(qkv_ref, sm_ref, smT_ref, conv0_ref, S0_ref, cw_ref, prow_ref, pcol_ref, gnorm_ref,
                o_ref, convn_ref, S_ref, xp_ref, *, nb, L, NC, T_valid, H, DK, DV):
    n = pl.program_id(1)
    K = H * DK
    pad0 = SUBLANES - (CONV_W - 1)

    @pl.when(n == 0)
    def _():
        xp_ref[:, pad0:SUBLANES, :] = conv0_ref[...]
        S_ref[...] = S0_ref[...]

    incl, strict = _tri_masks(L)
    lv = L if NC > 1 else T_valid
    chains = []
    for b in range(nb):
        xp_ref[b, SUBLANES:SUBLANES + L, :] = qkv_ref[b]
        xfull = xp_ref[b]
        y = xfull * cw_ref[CONV_W - 1:CONV_W, :]
        for j in range(CONV_W - 1):
            y = y + pltpu.roll(xfull, CONV_W - 1 - j, axis=0) * cw_ref[j:j + 1, :]
        y = y[SUBLANES:, :]
        qkv = y * _sigmoid(y)
        tail = xp_ref[b, SUBLANES + lv - (CONV_W - 1):SUBLANES + lv, :]
        xp_ref[b, pad0:SUBLANES, :] = tail
        convn_ref[b] = tail

        sm = sm_ref[b]
        smT = smT_ref[b, 0]
        beta_c = _sigmoid(sm)
        g_c = -jnp.exp(prow_ref[0:1, :]) * _softplus(sm + prow_ref[1:2, :])
        g_r = -jnp.exp(pcol_ref[:, 0:1]) * _softplus(smT + pcol_ref[:, 1:2])
        if T_valid < NC * L:
            vc = lax.broadcasted_iota(jnp.int32, (L, 1), 0) < T_valid
            vr = lax.broadcasted_iota(jnp.int32, (1, L), 1) < T_valid
            beta_c = jnp.where(vc, beta_c, 0.0)
            g_c = jnp.where(vc, g_c, 0.0)
            g_r = jnp.where(vr, g_r, 0.0)
        gc_c, gc_r = _cumsum_both(g_c, g_r, incl)
        for h in range(H):
            qh = qkv[:, h * DK:(h + 1) * DK]
            kh = qkv[:, K + h * DK:K + (h + 1) * DK]
            vh = qkv[:, 2 * K + h * DV:2 * K + (h + 1) * DV]
            qh = qh * lax.rsqrt(jnp.sum(qh * qh, axis=-1, keepdims=True) + EPS) * (DK ** -0.5)
            kh = kh * lax.rsqrt(jnp.sum(kh * kh, axis=-1, keepdims=True) + EPS)
            b_c = beta_c[:, h:h + 1]
            gcc = gc_c[:, H + h:H + h + 1]
            gcr = gc_r[H + h:H + h + 1, :]
            gl = gcc[L - 1:L, :]
            eg = jnp.exp(gcc)
            kb = kh * b_c
            chains.append(dict(
                b=b, h=h, kh=kh.astype(BF16), kbq=jnp.concatenate([kb, qh], axis=0).astype(BF16),
                decay=jnp.exp(jnp.where(incl, gcc - gcr, -jnp.inf)),
                rhs=jnp.concatenate([kb * eg, vh * b_c], axis=1), qd=qh * eg,
                kd=(kh * jnp.exp(gl - gcc)).astype(BF16), dl=jnp.exp(gl)))

    for c in chains:
        r = _mm_nt(c["kbq"], c["kh"])
        c["A"] = jnp.where(strict, r[:L] * c["decay"], 0.0)
        c["qk"] = (r[L:] * c["decay"]).astype(BF16)
    Xs, Rs = _tri_inv_unit_lower([c["A"] for c in chains])
    corr = [_mm(R, c["rhs"]) for R, c in zip(Rs, chains)]
    wu = [_mm(X, c["rhs"] + cr) for X, c, cr in zip(Xs, chains, corr)]
    states = [S_ref[c["b"], c["h"]] for c in chains]
    wq = [_mm(jnp.concatenate([x[:, :DK], c["qd"]], axis=0), S) for x, c, S in zip(wu, chains, states)]
    v_new = [(x[:, DK:] - y[:L]).astype(BF16) for x, y in zip(wu, wq)]
    o_intra = [_mm(c["qk"], v) for c, v in zip(chains, v_new)]
    s_upd = [_mm_tn(c["kd"], v) for c, v in zip(chains, v_new)]
    for c, S, su in zip(chains, states, s_upd):
        S_ref[c["b"], c["h"]] = S * c["dl"] + su
    o = [y[L:] + oi for y, oi in zip(wq, o_intra)]
    rs = [lax.rsqrt(jnp.mean(x * x, axis=-1, keepdims=True) + EPS) for x in o]
    for x, r, c in zip(o, rs, chains):
        o_ref[c["b"], :, c["h"] * DV:(c["h"] + 1) * DV] = x * r * gnorm_ref[:, c["h"] * DV:(c["h"] + 1) * DV]


def _skip_ref(kernel_fn, index):
    def wrapped(*refs, **kw):
        kernel_fn(*refs[:index], *refs[index + 1:], **kw)
    return wrapped


def _gdn(proj, small, smT, conv0_all, S0_all, layer, S_prev, cw, prow, pcol, gnorm, *, NC, L, T_valid, offs):
    depth, B, H, DK, DV = S0_all.shape
    CG = conv0_all.shape[-1]
    V = H * DV
    T_pad = proj.shape[1]
    nb = _tile(B, max(1, CHAINS // H), 1)
    kern = functools.partial(_gdn_kernel, nb=nb, L=L, NC=NC, T_valid=T_valid, H=H, DK=DK, DV=DV)
    in_specs = [pl.BlockSpec((nb, L, CG), lambda i, n: (i, n, offs["qkv"] // CG)),
                pl.BlockSpec((nb, L, LANES), lambda i, n: (i, n, 0)),
                pl.BlockSpec((nb, 1, GATE_ROWS, L), lambda i, n: (i, n, 0, 0)),
                pl.BlockSpec((None, nb, CONV_W - 1, CG), lambda i, n: (layer, i, 0, 0)),
                pl.BlockSpec((None, nb, H, DK, DV), lambda i, n: (layer, i, 0, 0, 0)),
                _layer_spec(layer, CONV_W, CG),
                _layer_spec(layer, 2, LANES),
                _layer_spec(layer, GATE_ROWS, 2),
                _layer_spec(layer, 1, V)]
    operands = [proj, small, smT, conv0_all, S0_all, cw, prow, pcol, gnorm]
    aliases = {}
    if S_prev is not None:
        kern = _skip_ref(kern, len(operands))
        aliases = {len(operands): 2}
        in_specs.append(pl.BlockSpec(memory_space=pl.ANY))
        operands.append(S_prev)
    return pl.pallas_call(
        kern,
        grid=(B // nb, NC),
        in_specs=in_specs,
        out_specs=[pl.BlockSpec((nb, L, V), lambda i, n: (i, n, 0)),
                   pl.BlockSpec((nb, CONV_W - 1, CG), lambda i, n: (i, 0, 0)),
                   pl.BlockSpec((None, nb, H, DK, DV), lambda i, n: (layer, i, 0, 0, 0))],
        out_shape=[jax.ShapeDtypeStruct((B, T_pad, V), F32),
                   jax.ShapeDtypeStruct(conv0_all.shape[1:], F32),
                   jax.ShapeDtypeStruct(S0_all.shape, F32)],
        scratch_shapes=[pltpu.VMEM((nb, SUBLANES + L, CG), F32)],
        input_output_aliases=aliases,
        compiler_params=pltpu.CompilerParams(dimension_semantics=("parallel", "arbitrary"),
                                             vmem_limit_bytes=VMEM_LIMIT),
        name="gdn_chunk",
    )(*operands)


def _mlstm_kernel(qk_ref, v_ref, sm_ref, smT_ref, C0_ref, n0_ref, m0_ref, prow_ref, pcol_ref, norm_ref,
                  h_ref, C_ref, n_ref, m_ref, *, nb, L, NC, T_valid, H, DK, DV):
    step = pl.program_id(1)
    K = H * DK
    g0 = 3 * H

    @pl.when(step == 0)
    def _():
        C_ref[...] = C0_ref[...]
        n_ref[...] = n0_ref[...]
        m_ref[...] = m0_ref[...]

    incl, _ = _tri_masks(L)
    lane = lax.broadcasted_iota(jnp.int32, (1, K), 1)
    lane1 = lax.broadcasted_iota(jnp.int32, (1, LANES), 1)
    rows = lax.broadcasted_iota(jnp.int32, (L, 1), 0)
    seqs = range(nb)

    sm = [sm_ref[b] for b in seqs]
    smT = [smT_ref[b, 0] for b in seqs]
    li_c = [x + prow_ref[0:1, :] for x in sm]
    li_r = [x + pcol_ref[:, 0:1] for x in smT]
    lf_c = [-_softplus(-(x + prow_ref[1:2, :])) for x in sm]
    lf_r = [-_softplus(-(x + pcol_ref[:, 1:2])) for x in smT]
    if T_valid < NC * L:
        vc = rows < T_valid
        vr = lax.broadcasted_iota(jnp.int32, (1, L), 1) < T_valid
        li_c = [jnp.where(vc, x, NEG_BIG) for x in li_c]
        li_r = [jnp.where(vr, x, NEG_BIG) for x in li_r]
        lf_c = [jnp.where(vc, x, 0.0) for x in lf_c]
        lf_r = [jnp.where(vr, x, 0.0) for x in lf_r]
    FF = [_cumsum_both(c, r, incl) for c, r in zip(lf_c, lf_r)]
    F_c = [f[0] for f in FF]
    F_r = [f[1] for f in FF]
    r_c = [pltpu.roll(x, H, axis=1) - f for x, f in zip(li_c, F_c)]
    cm = r_c
    s = 1
    while s < L:
        cm = [jnp.maximum(x, jnp.where(rows >= s, pltpu.roll(x, s, axis=0), -jnp.inf)) for x in cm]
        s *= 2
    m_old = [m_ref[b] for b in seqs]
    mx = [jnp.maximum(x, m) for x, m in zip(cm, m_old)]
    mx_last = [x[L - 1:L, :] for x in mx]
    dec = [jnp.exp(m - x) for m, x in zip(m_old, mx_last)]
    wC = [jnp.exp(r - x) for r, x in zip(r_c, mx_last)]
    a_all = [jnp.exp(m - x) for m, x in zip(m_old, mx)]
    floor = [jnp.exp(-f - x) for f, x in zip(F_c, mx)]
    for b in seqs:
        m_ref[b] = jnp.where((lane1 >= g0) & (lane1 < g0 + H), F_c[b][L - 1:L, :] + mx_last[b], 0.0)

    q_all = [qk_ref[b, :, :K] * (DK ** -0.5) for b in seqs]
    k_all = [qk_ref[b, :, K:] for b in seqs]
    k_bf = [x.astype(BF16) for x in k_all]
    C_all = [C_ref[b] for b in seqs]
    C_bf = [x.astype(BF16) for x in C_all]
    n_row = [n_ref[b] for b in seqs]

    half = lane1 < DK
    kw_cols, dec_lanes = [], []
    for b in seqs:
        cols = []
        for c in range(K // LANES):
            w_lo = wC[b][:, g0 + 2 * c:g0 + 2 * c + 1]
            w_hi = wC[b][:, g0 + 2 * c + 1:g0 + 2 * c + 2]
            cols.append(k_all[b][:, c * LANES:(c + 1) * LANES] * jnp.where(half, w_lo, w_hi))
        kw_cols.append(cols)
        d = dec[b][:, g0:g0 + 1]
        for h in range(1, H):
            d = jnp.where(lane >= h * DK, dec[b][:, g0 + h:g0 + h + 1], d)
        dec_lanes.append(d)
    for b in seqs:
        n_upd = jnp.concatenate([jnp.sum(x, axis=0, keepdims=True) for x in kw_cols[b]], axis=1)
        n_ref[b] = dec_lanes[b] * n_row[b] + n_upd

    chains = [(b, h) for b in seqs for h in range(H)]
    in_head = [(lane >= h * DK) & (lane < (h + 1) * DK) for h in range(H)]
    q_h = [jnp.where(in_head[h], q_all[b], 0.0) for b, h in chains]
    q_bf = [x.astype(BF16) for x in q_h]
    v_bf = [v_ref[b, :, h * DV:(h + 1) * DV].astype(BF16) for b, h in chains]
    kw = [kw_cols[b][h // 2][:, (h % 2) * DK:(h % 2 + 1) * DK].astype(BF16) for b, h in chains]
    expD = [jnp.exp(jnp.where(incl, (li_r[b][2 * H + h:2 * H + h + 1, :] - F_r[b][g0 + h:g0 + h + 1, :])
                              - mx[b][:, g0 + h:g0 + h + 1], -jnp.inf)) for b, h in chains]
    qn = [jnp.sum(x * n_row[b], axis=1, keepdims=True) for x, (b, h) in zip(q_h, chains)]

    qk = [_mm_nt(x, k_bf[b]) for x, (b, h) in zip(q_bf, chains)]
    Sm = [e * x for e, x in zip(expD, qk)]
    inter_state = [_mm(x, C_bf[b]) for x, (b, h) in zip(q_bf, chains)]
    intra = [_mm(x, v) for x, v in zip(Sm, v_bf)]
    upd = [_mm_tn(x, v) for x, v in zip(kw, v_bf)]
    rowsum = [jnp.sum(x, axis=1, keepdims=True) for x in Sm]
    for i, (b, h) in enumerate(chains):
        C_ref[b, h * DK:(h + 1) * DK, :] = dec[b][:, g0 + h:g0 + h + 1] * C_all[b][h * DK:(h + 1) * DK, :] + upd[i]
    a_h = [a_all[b][:, g0 + h:g0 + h + 1] for b, h in chains]
    den = [a * x + y for a, x, y in zip(a_h, qn, rowsum)]
    scale = [1.0 / jnp.maximum(jnp.abs(d), floor[b][:, g0 + h:g0 + h + 1]) for d, (b, h) in zip(den, chains)]
    hh = [(a * x + y) * sc for a, x, y, sc in zip(a_h, inter_state, intra, scale)]
    ms = [jnp.mean(x * x, axis=-1, keepdims=True) for x in hh]
    rs = [lax.rsqrt(x + EPS) for x in ms]
    for x, r, (b, h) in zip(hh, rs, chains):
        h_ref[b, :, h * DV:(h + 1) * DV] = x * r * norm_ref[:, h * DV:(h + 1) * DV]


def _mlstm(proj, small, smT, C0_all, layer, C_prev, n0_all, m0_all, prow, pcol, norm, *, NC, L, T_valid, offs):
    depth, B, K, DV = C0_all.shape
    V = norm.shape[-1]
    H = V // DV
    DK = K // H
    T_pad = proj.shape[1]
    nb = _tile(B, max(1, CHAINS // H), 1)
    assert 2 * DK == LANES and 4 * H <= LANES
    kern = functools.partial(_mlstm_kernel, nb=nb, L=L, NC=NC, T_valid=T_valid, H=H, DK=DK, DV=DV)
    in_specs = [pl.BlockSpec((nb, L, 2 * K), lambda i, n: (i, n, offs["qk_m"] // (2 * K))),
                pl.BlockSpec((nb, L, V), lambda i, n: (i, n, offs["v_m"] // V)),
                pl.BlockSpec((nb, L, LANES), lambda i, n: (i, n, 0)),
                pl.BlockSpec((nb, 1, GATE_ROWS, L), lambda i, n: (i, n, 0, 0)),
                pl.BlockSpec((None, nb, K, DV), lambda i, n: (layer, i, 0, 0)),
                pl.BlockSpec((None, nb, 1, K), lambda i, n: (layer, i, 0, 0)),
                pl.BlockSpec((None, nb, 1, LANES), lambda i, n: (layer, i, 0, 0)),
                _layer_spec(layer, 2, LANES),
                _layer_spec(layer, GATE_ROWS, 2),
                _layer_spec(layer, 1, V)]
    operands = [proj, proj, small, smT, C0_all, n0_all, m0_all, prow, pcol, norm]
    aliases = {}
    if C_prev is not None:
        kern = _skip_ref(kern, len(operands))
        aliases = {len(operands): 1}
        in_specs.append(pl.BlockSpec(memory_space=pl.ANY))
        operands.append(C_prev)
    outs = pl.pallas_call(
        kern,
        grid=(B // nb, NC),
        in_specs=in_specs,
        out_specs=[pl.BlockSpec((nb, L, V), lambda i, n: (i, n, 0)),
                   pl.BlockSpec((None, nb, K, DV), lambda i, n: (layer, i, 0, 0)),
                   pl.BlockSpec((nb, 1, K), lambda i, n: (i, 0, 0)),
                   pl.BlockSpec((nb, 1, LANES), lambda i, n: (i, 0, 0))],
        out_shape=[jax.ShapeDtypeStruct((B, T_pad, V), F32),
                   jax.ShapeDtypeStruct(C0_all.shape, F32),
                   jax.ShapeDtypeStruct((B, 1, K), F32),
                   jax.ShapeDtypeStruct((B, 1, LANES), F32)],
        input_output_aliases=aliases,
        compiler_params=pltpu.CompilerParams(dimension_semantics=("parallel", "arbitrary"),
                                             vmem_limit_bytes=VMEM_LIMIT),
        name="mlstm_chunk",
    )(*operands)
    hm, C_all, n_new, m_new = outs
    return hm, C_all, n_new.reshape(B, H, DK), m_new[:, 0, 3 * H:4 * H]


def _merge_kernel(og_ref, hm_ref, x_ref, g_ref, wg_ref, wbg_ref, wbm_ref, wout_ref, o_ref, *, Vg, Vm, D):
    x = x_ref[...]
    ms = jnp.mean(x * x, axis=-1, keepdims=True)
    xn = (x * lax.rsqrt(ms + EPS) * g_ref[...]).astype(BF16)
    z = jnp.dot(xn, wg_ref[:, 0:Vg], preferred_element_type=F32)
    og = (og_ref[...] * (z * _sigmoid(z))).astype(BF16)
    br_g = jnp.dot(og, wbg_ref[...], preferred_element_type=F32)
    om = jnp.dot(xn, wg_ref[:, Vg:Vg + Vm], preferred_element_type=F32)
    hm = (hm_ref[...] * _sigmoid(om)).astype(BF16)
    br_m = jnp.dot(hm, wbm_ref[...], preferred_element_type=F32)
    gg = jnp.dot(xn, wg_ref[:, Vg + Vm:Vg + Vm + D], preferred_element_type=F32)
    merged = _sigmoid(gg) * br_g
    gm = jnp.dot(xn, wg_ref[:, Vg + Vm + D:], preferred_element_type=F32)
    merged = merged + _sigmoid(gm) * br_m
    o_ref[...] = x + jnp.dot(merged.astype(BF16), wout_ref[...], preferred_element_type=F32)


def _merge(og, hm, x, gamma, wg, wbg, wbm, wout, layer):
    M, D = x.shape
    Vg = og.shape[1]
    Vm = hm.shape[1]
    tm = _tile(M, 512, SUBLANES)
    return pl.pallas_call(
        functools.partial(_merge_kernel, Vg=Vg, Vm=Vm, D=D),
        grid=(M // tm,),
        in_specs=[pl.BlockSpec((tm, Vg), lambda i: (i, 0)),
                  pl.BlockSpec((tm, Vm), lambda i: (i, 0)),
                  pl.BlockSpec((tm, D), lambda i: (i, 0)),
                  _layer_spec(layer, 1, D),
                  _layer_spec(layer, D, Vg + Vm + 2 * D),
                  _layer_spec(layer, Vg, D),
                  _layer_spec(layer, Vm, D),
                  _layer_spec(layer, D, D)],
        out_specs=pl.BlockSpec((tm, D), lambda i: (i, 0)),
        out_shape=jax.ShapeDtypeStruct((M, D), F32),
        compiler_params=pltpu.CompilerParams(dimension_semantics=("parallel",), vmem_limit_bytes=VMEM_LIMIT),
        name="merge_out",
    )(og, hm, x, gamma, wg, wbg, wbm, wout)


def _mlp_kernel(x_ref, g_ref, wup_ref, wdn_ref, gf_ref, o_ref, *, final_norm, tf):
    x = x_ref[...]
    ms = jnp.mean(x * x, axis=-1, keepdims=True)
    xn = (x * lax.rsqrt(ms + EPS) * g_ref[...]).astype(BF16)
    y = x
    for c in range(wup_ref.shape[1] // tf):
        hcol = jnp.maximum(jnp.dot(xn, wup_ref[:, c * tf:(c + 1) * tf], preferred_element_type=F32), 0.0)
        y = y + jnp.dot((hcol * hcol).astype(BF16), wdn_ref[c * tf:(c + 1) * tf, :], preferred_element_type=F32)
    if final_norm:
        ms = jnp.mean(y * y, axis=-1, keepdims=True)
        y = y * lax.rsqrt(ms + EPS) * gf_ref[...]
    o_ref[...] = y


def _mlp(x, gamma, wup, wdn, gamma_final, layer, *, final_norm):
    M, D = x.shape
    FF = wup.shape[2]
    tm = _tile(M, 512, SUBLANES)
    tf = _tile(FF, 1024, LANES)
    resident = pl.Buffered(1)
    return pl.pallas_call(
        functools.partial(_mlp_kernel, final_norm=final_norm, tf=tf),
        grid=(M // tm,),
        in_specs=[pl.BlockSpec((tm, D), lambda i: (i, 0)),
                  _layer_spec(layer, 1, D),
                  pl.BlockSpec((None, D, FF), lambda i: (layer, 0, 0), pipeline_mode=resident),
                  pl.BlockSpec((None, FF, D), lambda i: (layer, 0, 0), pipeline_mode=resident),
                  pl.BlockSpec((1, D), lambda i: (0, 0))],
        out_specs=pl.BlockSpec((tm, D), lambda i: (i, 0)),
        out_shape=jax.ShapeDtypeStruct((M, D), F32),
        compiler_params=pltpu.CompilerParams(dimension_semantics=("parallel",), vmem_limit_bytes=VMEM_LIMIT),
        name="mlp",
    )(x, gamma, wup, wdn, gamma_final)


def _pack_w_in(w_in, dims):
    D, Hg, DKg, DVg, Hm, DKm, DVm = dims
    Kg, Vg, Km, Vm = Hg * DKg, Hg * DVg, Hm * DKm, Hm * DVm
    sizes = (Kg, Kg, Vg, Vg, Hg, Hg, Km, Km, Vm, Vm, Hm, Hm, D, D)
    starts = [0]
    for s in sizes:
        starts.append(starts[-1] + s)
    col = lambda a, b: w_in[:, :, starts[a]:starts[b]]
    assert 2 * Hg + 2 * Hm <= GATE_ROWS and Hg == Hm
    pad = jnp.zeros(w_in.shape[:2] + (LANES - 2 * Hg - 2 * Hm,), w_in.dtype)
    segs = [("qkv", col(0, 3)), ("qk_m", col(6, 8)), ("v_m", col(8, 9)),
            ("small", jnp.concatenate([col(4, 6), col(10, 12), pad], axis=2))]
    offs, off = {}, 0
    for name, seg in segs:
        assert off % seg.shape[2] == 0, (name, off, seg.shape)
        offs[name] = off
        off += seg.shape[2]
    w_rec = jnp.concatenate([s for _, s in segs], axis=2).astype(BF16)
    w_gate = jnp.concatenate([col(3, 4), col(9, 10), col(12, 14)], axis=2).astype(BF16)
    return w_rec, w_gate, offs


def _gate_params(first, lane_first, second, lane_second):
    depth, H = first.shape
    place = lambda v, lane: jnp.pad(v.astype(F32), ((0, 0), (lane, LANES - lane - H)))
    row = jnp.stack([place(first, lane_first), place(second, lane_second)], axis=1)
    return row, jnp.swapaxes(row[:, :, :GATE_ROWS], 1, 2)


def _head_norm_rows(w, H, DV):
    depth = w.shape[0]
    return jnp.broadcast_to(w.astype(F32).reshape(depth, -1, DV), (depth, H, DV)).reshape(depth, 1, H * DV)


def _trunk(x, conv0, S0, C0, n0, m0, P, *, B, T_valid, L):
    M, D = x.shape
    T_pad = M // B
    NC = T_pad // L
    depth = P["w_in"].shape[0]
    offs = P["offs"]
    _, _, H, DK, DV = C0.shape
    C0_all = C0.reshape(depth, B, H * DK, DV)
    n0_all = n0.reshape(depth, B, 1, H * DK)
    m0_all = jnp.pad(m0.reshape(depth, B, 1, H), ((0, 0), (0, 0), (0, 0), (3 * H, LANES - 4 * H)))
    new = ([], [], [])
    S_all, C_all = None, None
    for l in range(depth):
        proj, small = _norm_proj(x, P["norm_mix"], P["w_in"], l)
        proj3 = proj.reshape(B, T_pad, proj.shape[1])
        small3 = small.reshape(B, T_pad, LANES)
        smT = jnp.swapaxes(small[:, :GATE_ROWS].reshape(B, NC, L, GATE_ROWS), 2, 3)
        og, conv_n, S_all = _gdn(proj3, small3, smT, conv0, S0, l, S_all, P["conv_w"], P["gdn_prow"],
                                 P["gdn_pcol"], P["gdn_norm"], NC=NC, L=L, T_valid=T_valid, offs=offs)
        hm, C_all, n_n, m_n = _mlstm(proj3, small3, smT, C0_all, l, C_all, n0_all, m0_all, P["ml_prow"],
                                     P["ml_pcol"], P["ml_norm"], NC=NC, L=L, T_valid=T_valid, offs=offs)
        x = _merge(og.reshape(M, -1), hm.reshape(M, -1), x, P["norm_mix"], P["w_gate"], P["w_bg"],
                   P["w_bm"], P["w_out"], l)
        x = _mlp(x, P["norm_mlp"], P["w_up"], P["w_down"], P["norm_final"], l, final_norm=(l == depth - 1))
        for lst, s in zip(new, (conv_n, n_n, m_n)):
            lst.append(s)
    conv_new, n_new, m_new = (jnp.stack(lst) for lst in new)
    return x, conv_new, S_all, C_all.reshape(C0.shape), n_new, m_new


def kernel(x_prompt, x_sample, state_gdn_conv, state_gdn_S, state_mlstm_C, state_mlstm_n, state_mlstm_m, norm_mix, w_in, gdn_conv_w, gdn_A_log, gdn_dt_bias, gdn_norm, ml_i_bias, ml_f_bias, ml_norm, w_branch_gdn, w_branch_ml, w_out, norm_mlp, w_up, w_down, norm_final):
    Bp, Tp, D = x_prompt.shape
    Bs, Ts, _ = x_sample.shape
    depth = w_in.shape[0]
    _, _, Hg, DKg, DVg = state_gdn_S.shape
    _, _, Hm, DKm, DVm = state_mlstm_C.shape
    CG = state_gdn_conv.shape[-1]
    dims = (D, Hg, DKg, DVg, Hm, DKm, DVm)
    assert Ts >= CONV_W - 1 and Tp >= CONV_W - 1

    w_rec, w_gate, offs = _pack_w_in(w_in, dims)
    P = {
        "offs": offs,
        "w_in": w_rec,
        "w_gate": w_gate,
        "norm_mix": norm_mix.reshape(depth, 1, D),
        "norm_mlp": norm_mlp.reshape(depth, 1, D),
        "norm_final": norm_final.reshape(1, D),
        "conv_w": gdn_conv_w,
        "w_bg": w_branch_gdn.astype(BF16),
        "w_bm": w_branch_ml.astype(BF16),
        "w_out": w_out.astype(BF16),
        "w_up": w_up.astype(BF16),
        "w_down": w_down.astype(BF16),
        "gdn_norm": _head_norm_rows(gdn_norm, Hg, DVg),
        "ml_norm": _head_norm_rows(ml_norm, Hm, DVm),
    }
    P["gdn_prow"], P["gdn_pcol"] = _gate_params(gdn_A_log, Hg, gdn_dt_bias, Hg)
    P["ml_prow"], P["ml_pcol"] = _gate_params(ml_i_bias, 2 * Hg, ml_f_bias, 2 * Hg + Hm)

    Lp = CHUNK if Tp % CHUNK == 0 else Tp
    assert Lp % SUBLANES == 0
    zeros = lambda *s: jnp.zeros(s, F32)
    yp, conv_p, S_p, C_p, n_p, m_p = _trunk(
        x_prompt.reshape(Bp * Tp, D), zeros(depth, Bp, CONV_W - 1, CG), zeros(depth, Bp, Hg, DKg, DVg),
        zeros(depth, Bp, Hm, DKm, DVm), zeros(depth, Bp, Hm, DKm), zeros(depth, Bp, Hm), P,
        B=Bp, T_valid=Tp, L=Lp)

    Ls = -(-Ts // SUBLANES) * SUBLANES
    xs = jnp.pad(x_sample, ((0, 0), (0, Ls - Ts), (0, 0))).reshape(Bs * Ls, D)
    ys, conv_s, S_s, C_s, n_s, m_s = _trunk(
        xs, state_gdn_conv, state_gdn_S, state_mlstm_C, state_mlstm_n, state_mlstm_m, P,
        B=Bs, T_valid=Ts, L=Ls)
    ys = ys.reshape(Bs, Ls, D)[:, :Ts]
    return (yp.reshape(Bp, Tp, D), ys, conv_p, S_p, C_p, n_p, m_p, conv_s, S_s, C_s, n_s, m_s)
```

```python
import functools

import jax
import jax.numpy as jnp
from jax import lax
from jax.experimental import pallas as pl
from jax.experimental.pallas import tpu as pltpu

F32 = jnp.float32
BF16 = jnp.bfloat16
EPS = 1e-6
CONV_W = 4
LANES = 128
SUBLANES = 8
GATE_ROWS = 16
CHUNK = 64
CHAINS = 16
NEG_BIG = -1e30
VMEM_LIMIT = 48 * 1024 * 1024
HIGHEST = lax.Precision.HIGHEST


def _sigmoid(x):
    return 1.0 / (1.0 + jnp.exp(-x))


def _softplus(x):
    return jnp.maximum(x, 0.0) + jnp.log(1.0 + jnp.exp(-jnp.abs(x)))


def _mm(a, b):
    return jnp.dot(a.astype(BF16), b.astype(BF16), preferred_element_type=F32)


def _mm_nt(a, b):
    return lax.dot_general(a.astype(BF16), b.astype(BF16), (((1,), (1,)), ((), ())), preferred_element_type=F32)


def _mm_tn(a, b):
    return lax.dot_general(a.astype(BF16), b.astype(BF16), (((0,), (0,)), ((), ())), preferred_element_type=F32)


def _mm_tn_list(As, Bs):
    K, M = As[0].shape
    if K < 64:
        return [_mm_tn(a, b) for a, b in zip(As, Bs)]
    eye = jnp.where(lax.broadcasted_iota(jnp.int32, (M, M), 0) == lax.broadcasted_iota(jnp.int32, (M, M), 1),
                    1.0, 0.0).astype(BF16)
    a_t = [_mm_nt(eye, a) for a in As]
    return [_mm(a, b) for a, b in zip(a_t, Bs)]


def _split_bf16(a):
    hi = a.astype(BF16)
    return hi, (a - hi.astype(F32)).astype(BF16)


def _tile(n, cap, mult):
    best = None
    for t in range(mult, min(n, cap) + 1, mult):
        if n % t == 0:
            best = t
    return best if best is not None else n


def _tri_masks(L):
    row = lax.broadcasted_iota(jnp.int32, (L, L), 0)
    col = lax.broadcasted_iota(jnp.int32, (L, L), 1)
    return row >= col, row > col


def _cumsum_both(g_col, g_row, incl):
    L = incl.shape[0]
    row = lax.broadcasted_iota(jnp.int32, (L, L), 0)
    col = lax.broadcasted_iota(jnp.int32, (L, L), 1)
    lower = jnp.where(incl, 1.0, 0.0)
    upper = jnp.where(row <= col, 1.0, 0.0)
    c_col = jnp.dot(lower, g_col, precision=HIGHEST, preferred_element_type=F32)
    c_row = jnp.dot(g_row, upper, precision=HIGHEST, preferred_element_type=F32)
    return c_col, c_row


def _tri_inv_unit_lower(As):
    L = As[0].shape[0]
    row = lax.broadcasted_iota(jnp.int32, (L, L), 0)
    col = lax.broadcasted_iota(jnp.int32, (L, L), 1)
    eye = jnp.where(row == col, 1.0, 0.0)
    levels = max(1, (L - 1).bit_length())
    Xs = [eye - A for A in As]
    if levels > 1:
        Ps = [_mm(A, A) for A in As]
        for _ in range(1, levels - 1):
            Rs = [_mm(jnp.concatenate([P, X], axis=0), P) for P, X in zip(Ps, Xs)]
            Ps = [R[:L] for R in Rs]
            Xs = [X + R[L:] for X, R in zip(Xs, Rs)]
        Xs = [X + _mm(X, P) for X, P in zip(Xs, Ps)]
    splits = [(_split_bf16(A), _split_bf16(X)) for A, X in zip(As, Xs)]
    AX1 = [jnp.dot(jnp.concatenate([a_hi, a_lo], axis=0), x_hi, preferred_element_type=F32)
           for (a_hi, a_lo), (x_hi, _) in splits]
    AX2 = [jnp.dot(a_hi, x_lo, preferred_element_type=F32) for (a_hi, _), (_, x_lo) in splits]
    Rs = [(eye - X) - (r1[:L] + (r1[L:] + r2)) for X, r1, r2 in zip(Xs, AX1, AX2)]
    return Xs, Rs


def _layer_spec(layer, *shape):
    return pl.BlockSpec((None,) + shape, lambda *_: (layer,) + (0,) * len(shape))


def _norm_proj_kernel(x_ref, g_ref, w_ref, o_ref, s_ref):
    x = x_ref[...]
    ms = jnp.mean(x * x, axis=-1, keepdims=True)
    xn = (x * lax.rsqrt(ms + EPS) * g_ref[...]).astype(BF16)
    res = jnp.dot(xn, w_ref[...], preferred_element_type=F32)
    o_ref[...] = res[:, :o_ref.shape[1]]
    s_ref[...] = res[:, o_ref.shape[1]:]


def _norm_proj(x, gamma_all, w_all, layer):
    M, D = x.shape
    N = w_all.shape[2]
    tm = _tile(M, 512, SUBLANES)
    return pl.pallas_call(
        _norm_proj_kernel,
        grid=(M // tm,),
        in_specs=[pl.BlockSpec((tm, D), lambda i: (i, 0)),
                  _layer_spec(layer, 1, D),
                  _layer_spec(layer, D, N)],
        out_specs=[pl.BlockSpec((tm, N - LANES), lambda i: (i, 0)),
                   pl.BlockSpec((tm, LANES), lambda i: (i, 0))],
        out_shape=[jax.ShapeDtypeStruct((M, N - LANES), F32),
                   jax.ShapeDtypeStruct((M, LANES), F32)],
        compiler_params=pltpu.CompilerParams(dimension_semantics=("parallel",), vmem_limit_bytes=VMEM_LIMIT),
        name="norm_proj",
    )(x, gamma_all, w_all)


def _gdn_prep_seq(b, qkv_ref, sm_ref, smT_ref, xp_ref, convn_ref, cw_ref, prow_ref, pcol_ref,
                  *, L, NC, T_valid, H, DK, DV):
    K = H * DK
    pad0 = SUBLANES - (CONV_W - 1)
    incl, _ = _tri_masks(L)
    lv = L if NC > 1 else T_valid
    xp_ref[b, SUBLANES:SUBLANES + L, :] = qkv_ref[b]
    xfull = xp_ref[b]
    y = xfull * cw_ref[CONV_W - 1:CONV_W, :]
    for j in range(CONV_W - 1):
        y = y + pltpu.roll(xfull, CONV_W - 1 - j, axis=0) * cw_ref[j:j + 1, :]
    y = y[SUBLANES:, :]
    qkv = y * _sigmoid(y)
    tail = xp_ref[b, SUBLANES + lv - (CONV_W - 1):SUBLANES + lv, :]
    xp_ref[b, pad0:SUBLANES, :] = tail
    convn_ref[b] = tail

    sm = sm_ref[b]
    smT = smT_ref[b, 0]
    beta_c = _sigmoid(sm)
    g_c = -jnp.exp(prow_ref[0:1, :]) * _softplus(sm + prow_ref[1:2, :])
    g_r = -jnp.exp(pcol_ref[:, 0:1]) * _softplus(smT + pcol_ref[:, 1:2])
    if T_valid < NC * L:
        vc = lax.broadcasted_iota(jnp.int32, (L, 1), 0) < T_valid
        vr = lax.broadcasted_iota(jnp.int32, (1, L), 1) < T_valid
        beta_c = jnp.where(vc, beta_c, 0.0)
        g_c = jnp.where(vc, g_c, 0.0)
        g_r = jnp.where(vr, g_r, 0.0)
    gc_c, gc_r = _cumsum_both(g_c, g_r, incl)
    chains = []
    for h in range(H):
        qh = qkv[:, h * DK:(h + 1) * DK]
        kh = qkv[:, K + h * DK:K + (h + 1) * DK]
        vh = qkv[:, 2 * K + h * DV:2 * K + (h + 1) * DV]
        qh = qh * lax.rsqrt(jnp.sum(qh * qh, axis=-1, keepdims=True) + EPS) * (DK ** -0.5)
        kh = kh * lax.rsqrt(jnp.sum(kh * kh, axis=-1, keepdims=True) + EPS)
        b_c = beta_c[:, h:h + 1]
        gcc = gc_c[:, H + h:H + h + 1]
        gcr = gc_r[H + h:H + h + 1, :]
        gl = gcc[L - 1:L, :]
        eg = jnp.exp(gcc)
        kb = kh * b_c
        chains.append(dict(
            kh=kh.astype(BF16), kbq=jnp.concatenate([kb, qh], axis=0).astype(BF16),
            decay=jnp.exp(jnp.where(incl, gcc - gcr, -jnp.inf)),
            rhs=jnp.concatenate([kb * eg, vh * b_c], axis=1), qd=(qh * eg).astype(BF16),
            kd=(kh * jnp.exp(gl - gcc)).astype(BF16), dl=jnp.exp(gl)))
    return chains


def _gdn_chain_phase(chains, S_ref, o_ref, gnorm_ref, *, L, DK, DV):
    _, strict = _tri_masks(L)
    ops = [c[2] for c in chains]
    r1 = [_mm_nt(c["kbq"], c["kh"]) for c in ops]
    A = [jnp.where(strict, r[:L] * c["decay"], 0.0) for r, c in zip(r1, ops)]
    qk = [(r[L:] * c["decay"]).astype(BF16) for r, c in zip(r1, ops)]
    Xs, Rs = _tri_inv_unit_lower(A)
    corr = [_mm(R, c["rhs"]) for R, c in zip(Rs, ops)]
    wu = [_mm(X, c["rhs"] + cr) for X, c, cr in zip(Xs, ops, corr)]
    states = [S_ref[b, h] for b, h, _ in chains]
    wq = [_mm(jnp.concatenate([x[:, :DK].astype(BF16), c["qd"]], axis=0), S) for x, c, S in zip(wu, ops, states)]
    v_new = [(x[:, DK:] - y[:L]).astype(BF16) for x, y in zip(wu, wq)]
    o_intra = [_mm(q, v) for q, v in zip(qk, v_new)]
    s_upd = [_mm_tn(c["kd"], v) for c, v in zip(ops, v_new)]
    for (b, h, c), S, su in zip(chains, states, s_upd):
        S_ref[b, h] = S * c["dl"] + su
    o = [y[L:] + oi for y, oi in zip(wq, o_intra)]
    rs = [lax.rsqrt(jnp.mean(x * x, axis=-1, keepdims=True) + EPS) for x in o]
    for x, r, (b, h, _) in zip(o, rs, chains):
        o_ref[b, :, h * DV:(h + 1) * DV] = x * r * gnorm_ref[:, h * DV:(h + 1) * DV]


def _gdn_kernel(qkv_ref, sm_ref, smT_ref, conv0_ref, S0_ref, cw_ref, prow_ref, pcol_ref, gnorm_ref,
                o_ref, convn_ref, S_ref, xp_ref, *, nb, L, NC, T_valid, H, DK, DV):
    pad0 = SUBLANES - (CONV_W - 1)

    @pl.when(pl.program_id(1) == 0)
    def _():
        xp_ref[:, pad0:SUBLANES, :] = conv0_ref[...]
        S_ref[...] = S0_ref[...]

    chains = []
    for b in range(nb):
        ops = _gdn_prep_seq(b, qkv_ref, sm_ref, smT_ref, xp_ref, convn_ref, cw_ref, prow_ref, pcol_ref,
                            L=L, NC=NC, T_valid=T_valid, H=H, DK=DK, DV=DV)
        chains += [(b, h, c) for h, c in enumerate(ops)]
    _gdn_chain_phase(chains, S_ref, o_ref, gnorm_ref, L=L, DK=DK, DV=DV)


def _skip_ref(kernel_fn, index):
    def wrapped(*refs, **kw):
        kernel_fn(*refs[:index], *refs[index + 1:], **kw)
    return wrapped


def _gdn(proj, small, smT, conv0_all, S0_all, layer, S_prev, cw, prow, pcol, gnorm, *, NC, L, T_valid, offs):
    depth, B, H, DK, DV = S0_all.shape
    CG = conv0_all.shape[-1]
    V = H * DV
    T_pad = proj.shape[1]
    nb = _tile(B, max(1, (CHAINS if L >= CHUNK else 2 * CHAINS) // H), 1)
    kern = functools.partial(_gdn_kernel, nb=nb, L=L, NC=NC, T_valid=T_valid, H=H, DK=DK, DV=DV)
    in_specs = [pl.BlockSpec((nb, L, CG), lambda i, n: (i, n, offs["qkv"] // CG)),
                pl.BlockSpec((nb, L, LANES), lambda i, n: (i, n, 0)),
                pl.BlockSpec((nb, 1, GATE_ROWS, L), lambda i, n: (i, n, 0, 0)),
                pl.BlockSpec((None, nb, CONV_W - 1, CG), lambda i, n: (layer, i, 0, 0)),
                pl.BlockSpec((None, nb, H, DK, DV), lambda i, n: (layer, i, 0, 0, 0)),
                _layer_spec(layer, CONV_W, CG),
                _layer_spec(layer, 2, LANES),
                _layer_spec(layer, GATE_ROWS, 2),
                _layer_spec(layer, 1, V)]
    operands = [proj, small, smT, conv0_all, S0_all, cw, prow, pcol, gnorm]
    aliases = {}
    if S_prev is not None:
        kern = _skip_ref(kern, len(operands))
        aliases = {len(operands): 2}
        in_specs.append(pl.BlockSpec(memory_space=pl.ANY))
        operands.append(S_prev)
    return pl.pallas_call(
        kern,
        grid=(B // nb, NC),
        in_specs=in_specs,
        out_specs=[pl.BlockSpec((nb, L, V), lambda i, n: (i, n, 0)),
                   pl.BlockSpec((nb, CONV_W - 1, CG), lambda i, n: (i, 0, 0)),
                   pl.BlockSpec((None, nb, H, DK, DV), lambda i, n: (layer, i, 0, 0, 0))],
        out_shape=[jax.ShapeDtypeStruct((B, T_pad, V), F32),
                   jax.ShapeDtypeStruct(conv0_all.shape[1:], F32),
                   jax.ShapeDtypeStruct(S0_all.shape, F32)],
        scratch_shapes=[pltpu.VMEM((nb, SUBLANES + L, CG), F32)],
        input_output_aliases=aliases,
        compiler_params=pltpu.CompilerParams(dimension_semantics=("parallel", "arbitrary"),
                                             vmem_limit_bytes=VMEM_LIMIT),
        name="gdn_chunk",
    )(*operands)


def _mlstm_kernel(qk_ref, v_ref, sm_ref, smT_ref, C0_ref, n0_ref, m0_ref, prow_ref, pcol_ref, norm_ref,
                  h_ref, C_ref, n_ref, m_ref, *, nb, L, NC, T_valid, H, DK, DV):
    step = pl.program_id(1)
    K = H * DK
    g0 = 3 * H

    @pl.when(step == 0)
    def _():
        C_ref[...] = C0_ref[...]
        n_ref[...] = n0_ref[...]
        m_ref[...] = m0_ref[...]

    incl, _ = _tri_masks(L)
    lane = lax.broadcasted_iota(jnp.int32, (1, K), 1)
    lane1 = lax.broadcasted_iota(jnp.int32, (1, LANES), 1)
    rows = lax.broadcasted_iota(jnp.int32, (L, 1), 0)
    seqs = range(nb)

    sm = [sm_ref[b] for b in seqs]
    smT = [smT_ref[b, 0] for b in seqs]
    li_c = [x + prow_ref[0:1, :] for x in sm]
    li_r = [x + pcol_ref[:, 0:1] for x in smT]
    lf_c = [-_softplus(-(x + prow_ref[1:2, :])) for x in sm]
    lf_r = [-_softplus(-(x + pcol_ref[:, 1:2])) for x in smT]
    if T_valid < NC * L:
        vc = rows < T_valid
        vr = lax.broadcasted_iota(jnp.int32, (1, L), 1) < T_valid
        li_c = [jnp.where(vc, x, NEG_BIG) for x in li_c]
        li_r = [jnp.where(vr, x, NEG_BIG) for x in li_r]
        lf_c = [jnp.where(vc, x, 0.0) for x in lf_c]
        lf_r = [jnp.where(vr, x, 0.0) for x in lf_r]
    FF = [_cumsum_both(c, r, incl) for c, r in zip(lf_c, lf_r)]
    F_c = [f[0] for f in FF]
    F_r = [f[1] for f in FF]
    r_c = [pltpu.roll(x, H, axis=1) - f for x, f in zip(li_c, F_c)]
    cm = r_c
    s = 1
    while s < L:
        cm = [jnp.maximum(x, jnp.where(rows >= s, pltpu.roll(x, s, axis=0), -jnp.inf)) for x in cm]
        s *= 2
    m_old = [m_ref[b] for b in seqs]
    mx = [jnp.maximum(x, m) for x, m in zip(cm, m_old)]
    mx_last = [x[L - 1:L, :] for x in mx]
    dec = [jnp.exp(m - x) for m, x in zip(m_old, mx_last)]
    wC = [jnp.exp(r - x) for r, x in zip(r_c, mx_last)]
    a_all = [jnp.exp(m - x) for m, x in zip(m_old, mx)]
    floor = [jnp.exp(-f - x) for f, x in zip(F_c, mx)]
    for b in seqs:
        m_ref[b] = jnp.where((lane1 >= g0) & (lane1 < g0 + H), F_c[b][L - 1:L, :] + mx_last[b], 0.0)

    q_all = [qk_ref[b, :, :K] * (DK ** -0.5) for b in seqs]
    k_all = [qk_ref[b, :, K:] for b in seqs]
    k_bf = [x.astype(BF16) for x in k_all]
    C_all = [C_ref[b] for b in seqs]
    C_bf = [x.astype(BF16) for x in C_all]
    n_row = [n_ref[b] for b in seqs]

    half = lane1 < DK
    kw_cols, dec_lanes = [], []
    for b in seqs:
        cols = []
        for c in range(K // LANES):
            w_lo = wC[b][:, g0 + 2 * c:g0 + 2 * c + 1]
            w_hi = wC[b][:, g0 + 2 * c + 1:g0 + 2 * c + 2]
            cols.append(k_all[b][:, c * LANES:(c + 1) * LANES] * jnp.where(half, w_lo, w_hi))
        kw_cols.append(cols)
        d = dec[b][:, g0:g0 + 1]
        for h in range(1, H):
            d = jnp.where(lane >= h * DK, dec[b][:, g0 + h:g0 + h + 1], d)
        dec_lanes.append(d)
    for b in seqs:
        n_upd = jnp.concatenate([jnp.sum(x, axis=0, keepdims=True) for x in kw_cols[b]], axis=1)
        n_ref[b] = dec_lanes[b] * n_row[b] + n_upd

    chains = [(b, h) for b in seqs for h in range(H)]
    in_head = [(lane >= h * DK) & (lane < (h + 1) * DK) for h in range(H)]
    q_h = [jnp.where(in_head[h], q_all[b], 0.0) for b, h in chains]
    q_bf = [x.astype(BF16) for x in q_h]
    v_bf = [v_ref[b, :, h * DV:(h + 1) * DV].astype(BF16) for b, h in chains]
    kw = [kw_cols[b][h // 2][:, (h % 2) * DK:(h % 2 + 1) * DK].astype(BF16) for b, h in chains]
    expD = [jnp.exp(jnp.where(incl, (li_r[b][2 * H + h:2 * H + h + 1, :] - F_r[b][g0 + h:g0 + h + 1, :])
                              - mx[b][:, g0 + h:g0 + h + 1], -jnp.inf)) for b, h in chains]
    qn = [jnp.sum(x * n_row[b], axis=1, keepdims=True) for x, (b, h) in zip(q_h, chains)]

    qk = [_mm_nt(x, k_bf[b]) for x, (b, h) in zip(q_bf, chains)]
    Sm = [e * x for e, x in zip(expD, qk)]
    inter_state = [_mm(x, C_bf[b]) for x, (b, h) in zip(q_bf, chains)]
    intra = [_mm(x, v) for x, v in zip(Sm, v_bf)]
    upd = _mm_tn_list(kw, v_bf)
    rowsum = [jnp.sum(x, axis=1, keepdims=True) for x in Sm]
    for i, (b, h) in enumerate(chains):
        C_ref[b, h * DK:(h + 1) * DK, :] = dec[b][:, g0 + h:g0 + h + 1] * C_all[b][h * DK:(h + 1) * DK, :] + upd[i]
    a_h = [a_all[b][:, g0 + h:g0 + h + 1] for b, h in chains]
    den = [a * x + y for a, x, y in zip(a_h, qn, rowsum)]
    scale = [1.0 / jnp.maximum(jnp.abs(d), floor[b][:, g0 + h:g0 + h + 1]) for d, (b, h) in zip(den, chains)]
    hh = [(a * x + y) * sc for a, x, y, sc in zip(a_h, inter_state, intra, scale)]
    ms = [jnp.mean(x * x, axis=-1, keepdims=True) for x in hh]
    rs = [lax.rsqrt(x + EPS) for x in ms]
    for x, r, (b, h) in zip(hh, rs, chains):
        h_ref[b, :, h * DV:(h + 1) * DV] = x * r * norm_ref[:, h * DV:(h + 1) * DV]


def _mlstm(proj, small, smT, C0_all, layer, C_prev, n0_all, m0_all, prow, pcol, norm, *, NC, L, T_valid, offs):
    depth, B, K, DV = C0_all.shape
    V = norm.shape[-1]
    H = V // DV
    DK = K // H
    T_pad = proj.shape[1]
    nb = _tile(B, max(1, 2 * CHAINS // H), 1)
    assert 2 * DK == LANES and 4 * H <= LANES
    kern = functools.partial(_mlstm_kernel, nb=nb, L=L, NC=NC, T_valid=T_valid, H=H, DK=DK, DV=DV)
    in_specs = [pl.BlockSpec((nb, L, 2 * K), lambda i, n: (i, n, offs["qk_m"] // (2 * K))),
                pl.BlockSpec((nb, L, V), lambda i, n: (i, n, offs["v_m"] // V)),
                pl.BlockSpec((nb, L, LANES), lambda i, n: (i, n, 0)),
                pl.BlockSpec((nb, 1, GATE_ROWS, L), lambda i, n: (i, n, 0, 0)),
                pl.BlockSpec((None, nb, K, DV), lambda i, n: (layer, i, 0, 0)),
                pl.BlockSpec((None, nb, 1, K), lambda i, n: (layer, i, 0, 0)),
                pl.BlockSpec((None, nb, 1, LANES), lambda i, n: (layer, i, 0, 0)),
                _layer_spec(layer, 2, LANES),
                _layer_spec(layer, GATE_ROWS, 2),
                _layer_spec(layer, 1, V)]
    operands = [proj, proj, small, smT, C0_all, n0_all, m0_all, prow, pcol, norm]
    aliases = {}
    if C_prev is not None:
        kern = _skip_ref(kern, len(operands))
        aliases = {len(operands): 1}
        in_specs.append(pl.BlockSpec(memory_space=pl.ANY))
        operands.append(C_prev)
    outs = pl.pallas_call(
        kern,
        grid=(B // nb, NC),
        in_specs=in_specs,
        out_specs=[pl.BlockSpec((nb, L, V), lambda i, n: (i, n, 0)),
                   pl.BlockSpec((None, nb, K, DV), lambda i, n: (layer, i, 0, 0)),
                   pl.BlockSpec((nb, 1, K), lambda i, n: (i, 0, 0)),
                   pl.BlockSpec((nb, 1, LANES), lambda i, n: (i, 0, 0))],
        out_shape=[jax.ShapeDtypeStruct((B, T_pad, V), F32),
                   jax.ShapeDtypeStruct(C0_all.shape, F32),
                   jax.ShapeDtypeStruct((B, 1, K), F32),
                   jax.ShapeDtypeStruct((B, 1, LANES), F32)],
        input_output_aliases=aliases,
        compiler_params=pltpu.CompilerParams(dimension_semantics=("parallel", "arbitrary"),
                                             vmem_limit_bytes=VMEM_LIMIT),
        name="mlstm_chunk",
    )(*operands)
    hm, C_all, n_new, m_new = outs
    return hm, C_all, n_new.reshape(B, H, DK), m_new[:, 0, 3 * H:4 * H]


def _merge_kernel(og_ref, hm_ref, x_ref, g_ref, wg_ref, wbg_ref, wbm_ref, wout_ref, o_ref, *, Vg, Vm, D):
    x = x_ref[...]
    ms = jnp.mean(x * x, axis=-1, keepdims=True)
    xn = (x * lax.rsqrt(ms + EPS) * g_ref[...]).astype(BF16)
    z = jnp.dot(xn, wg_ref[:, 0:Vg], preferred_element_type=F32)
    og = (og_ref[...] * (z * _sigmoid(z))).astype(BF16)
    br_g = jnp.dot(og, wbg_ref[...], preferred_element_type=F32)
    om = jnp.dot(xn, wg_ref[:, Vg:Vg + Vm], preferred_element_type=F32)
    hm = (hm_ref[...] * _sigmoid(om)).astype(BF16)
    br_m = jnp.dot(hm, wbm_ref[...], preferred_element_type=F32)
    gg = jnp.dot(xn, wg_ref[:, Vg + Vm:Vg + Vm + D], preferred_element_type=F32)
    merged = _sigmoid(gg) * br_g
    gm = jnp.dot(xn, wg_ref[:, Vg + Vm + D:], preferred_element_type=F32)
    merged = merged + _sigmoid(gm) * br_m
    o_ref[...] = x + jnp.dot(merged.astype(BF16), wout_ref[...], preferred_element_type=F32)


def _merge(og, hm, x, gamma, wg, wbg, wbm, wout, layer):
    M, D = x.shape
    Vg = og.shape[1]
    Vm = hm.shape[1]
    tm = _tile(M, 512, SUBLANES)
    return pl.pallas_call(
        functools.partial(_merge_kernel, Vg=Vg, Vm=Vm, D=D),
        grid=(M // tm,),
        in_specs=[pl.BlockSpec((tm, Vg), lambda i: (i, 0)),
                  pl.BlockSpec((tm, Vm), lambda i: (i, 0)),
                  pl.BlockSpec((tm, D), lambda i: (i, 0)),
                  _layer_spec(layer, 1, D),
                  _layer_spec(layer, D, Vg + Vm + 2 * D),
                  _layer_spec(layer, Vg, D),
                  _layer_spec(layer, Vm, D),
                  _layer_spec(layer, D, D)],
        out_specs=pl.BlockSpec((tm, D), lambda i: (i, 0)),
        out_shape=jax.ShapeDtypeStruct((M, D), F32),
        compiler_params=pltpu.CompilerParams(dimension_semantics=("parallel",), vmem_limit_bytes=VMEM_LIMIT),
        name="merge_out",
    )(og, hm, x, gamma, wg, wbg, wbm, wout)


def _mlp_kernel(x_ref, g_ref, wup_ref, wdn_ref, gf_ref, o_ref, *, final_norm, tf):
    x = x_ref[...]
    ms = jnp.mean(x * x, axis=-1, keepdims=True)
    xn = (x * lax.rsqrt(ms + EPS) * g_ref[...]).astype(BF16)
    y = x
    for c in range(wup_ref.shape[1] // tf):
        hcol = jnp.maximum(jnp.dot(xn, wup_ref[:, c * tf:(c + 1) * tf], preferred_element_type=F32), 0.0)
        y = y + jnp.dot((hcol * hcol).astype(BF16), wdn_ref[c * tf:(c + 1) * tf, :], preferred_element_type=F32)
    if final_norm:
        ms = jnp.mean(y * y, axis=-1, keepdims=True)
        y = y * lax.rsqrt(ms + EPS) * gf_ref[...]
    o_ref[...] = y


def _mlp(x, gamma, wup, wdn, gamma_final, layer, *, final_norm):
    M, D = x.shape
    FF = wup.shape[2]
    tm = _tile(M, 512, SUBLANES)
    tf = _tile(FF, 1024, LANES)
    resident = pl.Buffered(1)
    return pl.pallas_call(
        functools.partial(_mlp_kernel, final_norm=final_norm, tf=tf),
        grid=(M // tm,),
        in_specs=[pl.BlockSpec((tm, D), lambda i: (i, 0)),
                  _layer_spec(layer, 1, D),
                  pl.BlockSpec((None, D, FF), lambda i: (layer, 0, 0), pipeline_mode=resident),
                  pl.BlockSpec((None, FF, D), lambda i: (layer, 0, 0), pipeline_mode=resident),
                  pl.BlockSpec((1, D), lambda i: (0, 0))],
        out_specs=pl.BlockSpec((tm, D), lambda i: (i, 0)),
        out_shape=jax.ShapeDtypeStruct((M, D), F32),
        compiler_params=pltpu.CompilerParams(dimension_semantics=("parallel",), vmem_limit_bytes=VMEM_LIMIT),
        name="mlp",
    )(x, gamma, wup, wdn, gamma_final)


def _pack_w_in(w_in, dims):
    D, Hg, DKg, DVg, Hm, DKm, DVm = dims
    Kg, Vg, Km, Vm = Hg * DKg, Hg * DVg, Hm * DKm, Hm * DVm
    sizes = (Kg, Kg, Vg, Vg, Hg, Hg, Km, Km, Vm, Vm, Hm, Hm, D, D)
    starts = [0]
    for s in sizes:
        starts.append(starts[-1] + s)
    col = lambda a, b: w_in[:, :, starts[a]:starts[b]]
    assert 2 * Hg + 2 * Hm <= GATE_ROWS and Hg == Hm
    pad = jnp.zeros(w_in.shape[:2] + (LANES - 2 * Hg - 2 * Hm,), w_in.dtype)
    segs = [("qkv", col(0, 3)), ("qk_m", col(6, 8)), ("v_m", col(8, 9)),
            ("small", jnp.concatenate([col(4, 6), col(10, 12), pad], axis=2))]
    offs, off = {}, 0
    for name, seg in segs:
        assert off % seg.shape[2] == 0, (name, off, seg.shape)
        offs[name] = off
        off += seg.shape[2]
    w_rec = jnp.concatenate([s for _, s in segs], axis=2).astype(BF16)
    w_gate = jnp.concatenate([col(3, 4), col(9, 10), col(12, 14)], axis=2).astype(BF16)
    return w_rec, w_gate, offs


def _gate_params(first, lane_first, second, lane_second):
    depth, H = first.shape
    place = lambda v, lane: jnp.pad(v.astype(F32), ((0, 0), (lane, LANES - lane - H)))
    row = jnp.stack([place(first, lane_first), place(second, lane_second)], axis=1)
    return row, jnp.swapaxes(row[:, :, :GATE_ROWS], 1, 2)


def _head_norm_rows(w, H, DV):
    depth = w.shape[0]
    return jnp.broadcast_to(w.astype(F32).reshape(depth, -1, DV), (depth, H, DV)).reshape(depth, 1, H * DV)


def _trunk(x, conv0, S0, C0, n0, m0, P, *, B, T_valid, L):
    M, D = x.shape
    T_pad = M // B
    NC = T_pad // L
    depth = P["w_in"].shape[0]
    offs = P["offs"]
    _, _, H, DK, DV = C0.shape
    C0_all = C0.reshape(depth, B, H * DK, DV)
    n0_all = n0.reshape(depth, B, 1, H * DK)
    m0_all = jnp.pad(m0.reshape(depth, B, 1, H), ((0, 0), (0, 0), (0, 0), (3 * H, LANES - 4 * H)))
    new = ([], [], [])
    S_all, C_all = None, None
    for l in range(depth):
        proj, small = _norm_proj(x, P["norm_mix"], P["w_in"], l)
        proj3 = proj.reshape(B, T_pad, proj.shape[1])
        small3 = small.reshape(B, T_pad, LANES)
        smT = jnp.swapaxes(small[:, :GATE_ROWS].reshape(B, NC, L, GATE_ROWS), 2, 3)
        og, conv_n, S_all = _gdn(proj3, small3, smT, conv0, S0, l, S_all, P["conv_w"], P["gdn_prow"],
                                 P["gdn_pcol"], P["gdn_norm"], NC=NC, L=L, T_valid=T_valid, offs=offs)
        hm, C_all, n_n, m_n = _mlstm(proj3, small3, smT, C0_all, l, C_all, n0_all, m0_all, P["ml_prow"],
                                     P["ml_pcol"], P["ml_norm"], NC=NC, L=L, T_valid=T_valid, offs=offs)
        x = _merge(og.reshape(M, -1), hm.reshape(M, -1), x, P["norm_mix"], P["w_gate"], P["w_bg"],
                   P["w_bm"], P["w_out"], l)
        x = _mlp(x, P["norm_mlp"], P["w_up"], P["w_down"], P["norm_final"], l, final_norm=(l == depth - 1))
        for lst, s in zip(new, (conv_n, n_n, m_n)):
            lst.append(s)
    conv_new, n_new, m_new = (jnp.stack(lst) for lst in new)
    return x, conv_new, S_all, C_all.reshape(C0.shape), n_new, m_new


def kernel(x_prompt, x_sample, state_gdn_conv, state_gdn_S, state_mlstm_C, state_mlstm_n, state_mlstm_m, norm_mix, w_in, gdn_conv_w, gdn_A_log, gdn_dt_bias, gdn_norm, ml_i_bias, ml_f_bias, ml_norm, w_branch_gdn, w_branch_ml, w_out, norm_mlp, w_up, w_down, norm_final):
    Bp, Tp, D = x_prompt.shape
    Bs, Ts, _ = x_sample.shape
    depth = w_in.shape[0]
    _, _, Hg, DKg, DVg = state_gdn_S.shape
    _, _, Hm, DKm, DVm = state_mlstm_C.shape
    CG = state_gdn_conv.shape[-1]
    dims = (D, Hg, DKg, DVg, Hm, DKm, DVm)
    assert Ts >= CONV_W - 1 and Tp >= CONV_W - 1

    w_rec, w_gate, offs = _pack_w_in(w_in, dims)
    P = {
        "offs": offs,
        "w_in": w_rec,
        "w_gate": w_gate,
        "norm_mix": norm_mix.reshape(depth, 1, D),
        "norm_mlp": norm_mlp.reshape(depth, 1, D),
        "norm_final": norm_final.reshape(1, D),
        "conv_w": gdn_conv_w,
        "w_bg": w_branch_gdn.astype(BF16),
        "w_bm": w_branch_ml.astype(BF16),
        "w_out": w_out.astype(BF16),
        "w_up": w_up.astype(BF16),
        "w_down": w_down.astype(BF16),
        "gdn_norm": _head_norm_rows(gdn_norm, Hg, DVg),
        "ml_norm": _head_norm_rows(ml_norm, Hm, DVm),
    }
    P["gdn_prow"], P["gdn_pcol"] = _gate_params(gdn_A_log, Hg, gdn_dt_bias, Hg)
    P["ml_prow"], P["ml_pcol"] = _gate_params(ml_i_bias, 2 * Hg, ml_f_bias, 2 * Hg + Hm)

    Lp = CHUNK if Tp % CHUNK == 0 else Tp
    assert Lp % SUBLANES == 0
    zeros = lambda *s: jnp.zeros(s, F32)
    yp, conv_p, S_p, C_p, n_p, m_p = _trunk(
        x_prompt.reshape(Bp * Tp, D), zeros(depth, Bp, CONV_W - 1, CG), zeros(depth, Bp, Hg, DKg, DVg),
        zeros(depth, Bp, Hm, DKm, DVm), zeros(depth, Bp, Hm, DKm), zeros(depth, Bp, Hm), P,
        B=Bp, T_valid=Tp, L=Lp)

    Ls = -(-Ts // SUBLANES) * SUBLANES
    xs = jnp.pad(x_sample, ((0, 0), (0, Ls - Ts), (0, 0))).reshape(Bs * Ls, D)
    ys, conv_s, S_s, C_s, n_s, m_s = _trunk(
        xs, state_gdn_conv, state_gdn_S, state_mlstm_C, state_mlstm_n, state_mlstm_m, P,
        B=Bs, T_valid=Ts, L=Ls)
    ys = ys.reshape(Bs, Ls, D)[:, :Ts]
    return (yp.reshape(Bp, Tp, D), ys, conv_p, S_p, C_p, n_p, m_p, conv_s, S_s, C_s, n_s, m_s)
```

```python
import functools

import jax
import jax.numpy as jnp
from jax import lax
from jax.experimental import pallas as pl
from jax.experimental.pallas import tpu as pltpu

F32 = jnp.float32
BF16 = jnp.bfloat16
EPS = 1e-6
CONV_W = 4
LANES = 128
SUBLANES = 8
GATE_ROWS = 16
CHUNK = 64
CHAINS = 16
ROW_PARTS = 2
NEG_BIG = -1e30
VMEM_LIMIT = 48 * 1024 * 1024
HIGHEST = lax.Precision.HIGHEST


def _sigmoid(x):
    return 1.0 / (1.0 + jnp.exp(-x))


def _softplus(x):
    return jnp.maximum(x, 0.0) + jnp.log(1.0 + jnp.exp(-jnp.abs(x)))


def _mm(a, b):
    return jnp.dot(a.astype(BF16), b.astype(BF16), preferred_element_type=F32)


def _mm_nt(a, b):
    return lax.dot_general(a.astype(BF16), b.astype(BF16), (((1,), (1,)), ((), ())), preferred_element_type=F32)


def _mm_tn(a, b):
    return lax.dot_general(a.astype(BF16), b.astype(BF16), (((0,), (0,)), ((), ())), preferred_element_type=F32)


def _mm_tn_list(As, Bs):
    K, M = As[0].shape
    if K < 64:
        return [_mm_tn(a, b) for a, b in zip(As, Bs)]
    eye = jnp.where(lax.broadcasted_iota(jnp.int32, (M, M), 0) == lax.broadcasted_iota(jnp.int32, (M, M), 1),
                    1.0, 0.0).astype(BF16)
    a_t = [_mm_nt(eye, a) for a in As]
    return [_mm(a, b) for a, b in zip(a_t, Bs)]


def _split_bf16(a):
    hi = a.astype(BF16)
    return hi, (a - hi.astype(F32)).astype(BF16)


def _tile(n, cap, mult):
    best = None
    for t in range(mult, min(n, cap) + 1, mult):
        if n % t == 0:
            best = t
    return best if best is not None else n


def _tri_masks(L):
    row = lax.broadcasted_iota(jnp.int32, (L, L), 0)
    col = lax.broadcasted_iota(jnp.int32, (L, L), 1)
    return row >= col, row > col


def _cumsum_both(g_col, g_row, incl):
    L = incl.shape[0]
    row = lax.broadcasted_iota(jnp.int32, (L, L), 0)
    col = lax.broadcasted_iota(jnp.int32, (L, L), 1)
    lower = jnp.where(incl, 1.0, 0.0)
    upper = jnp.where(row <= col, 1.0, 0.0)
    c_col = jnp.dot(lower, g_col, precision=HIGHEST, preferred_element_type=F32)
    c_row = jnp.dot(g_row, upper, precision=HIGHEST, preferred_element_type=F32)
    return c_col, c_row


def _tri_inv_unit_lower(As):
    L = As[0].shape[0]
    row = lax.broadcasted_iota(jnp.int32, (L, L), 0)
    col = lax.broadcasted_iota(jnp.int32, (L, L), 1)
    eye = jnp.where(row == col, 1.0, 0.0)
    levels = max(1, (L - 1).bit_length())
    Xs = [eye - A for A in As]
    if levels > 1:
        Ps = [_mm(A, A) for A in As]
        for _ in range(1, levels - 1):
            Rs = [_mm(jnp.concatenate([P, X], axis=0), P) for P, X in zip(Ps, Xs)]
            Ps = [R[:L] for R in Rs]
            Xs = [X + R[L:] for X, R in zip(Xs, Rs)]
        Xs = [X + _mm(X, P) for X, P in zip(Xs, Ps)]
    splits = [(_split_bf16(A), _split_bf16(X)) for A, X in zip(As, Xs)]
    AX1 = [jnp.dot(jnp.concatenate([a_hi, a_lo], axis=0), x_hi, preferred_element_type=F32)
           for (a_hi, a_lo), (x_hi, _) in splits]
    AX2 = [jnp.dot(a_hi, x_lo, preferred_element_type=F32) for (a_hi, _), (_, x_lo) in splits]
    Rs = [(eye - X) - (r1[:L] + (r1[L:] + r2)) for X, r1, r2 in zip(Xs, AX1, AX2)]
    return Xs, Rs


def _layer_spec(layer, *shape):
    return pl.BlockSpec((None,) + shape, lambda *_: (layer,) + (0,) * len(shape))


def _row_parts(tm):
    return [pl.ds(r * (tm // ROW_PARTS), tm // ROW_PARTS) for r in range(ROW_PARTS)]


def _norm_proj_kernel(x_ref, g_ref, w_ref, o_ref, s_ref):
    parts = _row_parts(x_ref.shape[0])
    x = [x_ref[p, :] for p in parts]
    xn = [(v * lax.rsqrt(jnp.mean(v * v, axis=-1, keepdims=True) + EPS) * g_ref[...]).astype(BF16) for v in x]
    res = [jnp.dot(v, w_ref[...], preferred_element_type=F32) for v in xn]
    for p, r in zip(parts, res):
        o_ref[p, :] = r[:, :o_ref.shape[1]]
        s_ref[p, :] = r[:, o_ref.shape[1]:]


def _norm_proj(x, gamma_all, w_all, layer):
    M, D = x.shape
    N = w_all.shape[2]
    tm = _tile(M, 512, SUBLANES * ROW_PARTS)
    return pl.pallas_call(
        _norm_proj_kernel,
        grid=(M // tm,),
        in_specs=[pl.BlockSpec((tm, D), lambda i: (i, 0)),
                  _layer_spec(layer, 1, D),
                  _layer_spec(layer, D, N)],
        out_specs=[pl.BlockSpec((tm, N - LANES), lambda i: (i, 0)),
                   pl.BlockSpec((tm, LANES), lambda i: (i, 0))],
        out_shape=[jax.ShapeDtypeStruct((M, N - LANES), F32),
                   jax.ShapeDtypeStruct((M, LANES), F32)],
        compiler_params=pltpu.CompilerParams(dimension_semantics=("parallel",), vmem_limit_bytes=VMEM_LIMIT),
        name="norm_proj",
    )(x, gamma_all, w_all)


def _gdn_prep_seq(b, qkv_ref, sm_ref, smT_ref, xp_ref, convn_ref, cw_ref, prow_ref, pcol_ref,
                  *, L, NC, T_valid, H, DK, DV):
    K = H * DK
    pad0 = SUBLANES - (CONV_W - 1)
    incl, _ = _tri_masks(L)
    lv = L if NC > 1 else T_valid
    xp_ref[b, SUBLANES:SUBLANES + L, :] = qkv_ref[b]
    xfull = xp_ref[b]
    y = xfull * cw_ref[CONV_W - 1:CONV_W, :]
    for j in range(CONV_W - 1):
        y = y + pltpu.roll(xfull, CONV_W - 1 - j, axis=0) * cw_ref[j:j + 1, :]
    y = y[SUBLANES:, :]
    qkv = y * _sigmoid(y)
    tail = xp_ref[b, SUBLANES + lv - (CONV_W - 1):SUBLANES + lv, :]
    xp_ref[b, pad0:SUBLANES, :] = tail
    convn_ref[b] = tail

    sm = sm_ref[b]
    smT = smT_ref[b, 0]
    beta_c = _sigmoid(sm)
    g_c = -jnp.exp(prow_ref[0:1, :]) * _softplus(sm + prow_ref[1:2, :])
    g_r = -jnp.exp(pcol_ref[:, 0:1]) * _softplus(smT + pcol_ref[:, 1:2])
    if T_valid < NC * L:
        vc = lax.broadcasted_iota(jnp.int32, (L, 1), 0) < T_valid
        vr = lax.broadcasted_iota(jnp.int32, (1, L), 1) < T_valid
        beta_c = jnp.where(vc, beta_c, 0.0)
        g_c = jnp.where(vc, g_c, 0.0)
        g_r = jnp.where(vr, g_r, 0.0)
    gc_c, gc_r = _cumsum_both(g_c, g_r, incl)
    chains = []
    for h in range(H):
        qh = qkv[:, h * DK:(h + 1) * DK]
        kh = qkv[:, K + h * DK:K + (h + 1) * DK]
        vh = qkv[:, 2 * K + h * DV:2 * K + (h + 1) * DV]
        qh = qh * lax.rsqrt(jnp.sum(qh * qh, axis=-1, keepdims=True) + EPS) * (DK ** -0.5)
        kh = kh * lax.rsqrt(jnp.sum(kh * kh, axis=-1, keepdims=True) + EPS)
        b_c = beta_c[:, h:h + 1]
        gcc = gc_c[:, H + h:H + h + 1]
        gcr = gc_r[H + h:H + h + 1, :]
        gl = gcc[L - 1:L, :]
        eg = jnp.exp(gcc)
        kb = kh * b_c
        chains.append(dict(
            kh=kh.astype(BF16), kbq=jnp.concatenate([kb, qh], axis=0).astype(BF16),
            decay=jnp.exp(jnp.where(incl, gcc - gcr, -jnp.inf)),
            rhs=jnp.concatenate([kb * eg, vh * b_c], axis=1), qd=(qh * eg).astype(BF16),
            kd=(kh * jnp.exp(gl - gcc)).astype(BF16), dl=jnp.exp(gl)))
    return chains


def _gdn_chain_phase(chains, S_ref, o_ref, gnorm_ref, *, L, DK, DV):
    _, strict = _tri_masks(L)
    ops = [c[2] for c in chains]
    r1 = [_mm_nt(c["kbq"], c["kh"]) for c in ops]
    A = [jnp.where(strict, r[:L] * c["decay"], 0.0) for r, c in zip(r1, ops)]
    qk = [(r[L:] * c["decay"]).astype(BF16) for r, c in zip(r1, ops)]
    Xs, Rs = _tri_inv_unit_lower(A)
    corr = [_mm(R, c["rhs"]) for R, c in zip(Rs, ops)]
    wu = [_mm(X, c["rhs"] + cr) for X, c, cr in zip(Xs, ops, corr)]
    states = [S_ref[b, h] for b, h, _ in chains]
    wq = [_mm(jnp.concatenate([x[:, :DK].astype(BF16), c["qd"]], axis=0), S) for x, c, S in zip(wu, ops, states)]
    v_new = [(x[:, DK:] - y[:L]).astype(BF16) for x, y in zip(wu, wq)]
    o_intra = [_mm(q, v) for q, v in zip(qk, v_new)]
    s_upd = [_mm_tn(c["kd"], v) for c, v in zip(ops, v_new)]
    for (b, h, c), S, su in zip(chains, states, s_upd):
        S_ref[b, h] = S * c["dl"] + su
    o = [y[L:] + oi for y, oi in zip(wq, o_intra)]
    rs = [lax.rsqrt(jnp.mean(x * x, axis=-1, keepdims=True) + EPS) for x in o]
    for x, r, (b, h, _) in zip(o, rs, chains):
        o_ref[b, :, h * DV:(h + 1) * DV] = x * r * gnorm_ref[:, h * DV:(h + 1) * DV]


def _gdn_kernel(qkv_ref, sm_ref, smT_ref, conv0_ref, S0_ref, cw_ref, prow_ref, pcol_ref, gnorm_ref,
                o_ref, convn_ref, S_ref, xp_ref, *, nb, L, NC, T_valid, H, DK, DV):
    pad0 = SUBLANES - (CONV_W - 1)

    @pl.when(pl.program_id(1) == 0)
    def _():
        xp_ref[:, pad0:SUBLANES, :] = conv0_ref[...]
        S_ref[...] = S0_ref[...]

    chains = []
    for b in range(nb):
        ops = _gdn_prep_seq(b, qkv_ref, sm_ref, smT_ref, xp_ref, convn_ref, cw_ref, prow_ref, pcol_ref,
                            L=L, NC=NC, T_valid=T_valid, H=H, DK=DK, DV=DV)
        chains += [(b, h, c) for h, c in enumerate(ops)]
    _gdn_chain_phase(chains, S_ref, o_ref, gnorm_ref, L=L, DK=DK, DV=DV)


def _skip_ref(kernel_fn, index):
    def wrapped(*refs, **kw):
        kernel_fn(*refs[:index], *refs[index + 1:], **kw)
    return wrapped


def _gdn(proj, small, smT, conv0_all, S0_all, layer, S_prev, cw, prow, pcol, gnorm, *, NC, L, T_valid, offs):
    depth, B, H, DK, DV = S0_all.shape
    CG = conv0_all.shape[-1]
    V = H * DV
    T_pad = proj.shape[1]
    nb = _tile(B, max(1, (CHAINS if L >= CHUNK else 2 * CHAINS) // H), 1)
    kern = functools.partial(_gdn_kernel, nb=nb, L=L, NC=NC, T_valid=T_valid, H=H, DK=DK, DV=DV)
    in_specs = [pl.BlockSpec((nb, L, CG), lambda i, n: (i, n, offs["qkv"] // CG)),
                pl.BlockSpec((nb, L, LANES), lambda i, n: (i, n, 0)),
                pl.BlockSpec((nb, 1, GATE_ROWS, L), lambda i, n: (i, n, 0, 0)),
                pl.BlockSpec((None, nb, CONV_W - 1, CG), lambda i, n: (layer, i, 0, 0)),
                pl.BlockSpec((None, nb, H, DK, DV), lambda i, n: (layer, i, 0, 0, 0)),
                _layer_spec(layer, CONV_W, CG),
                _layer_spec(layer, 2, LANES),
                _layer_spec(layer, GATE_ROWS, 2),
                _layer_spec(layer, 1, V)]
    operands = [proj, small, smT, conv0_all, S0_all, cw, prow, pcol, gnorm]
    aliases = {}
    if S_prev is not None:
        kern = _skip_ref(kern, len(operands))
        aliases = {len(operands): 2}
        in_specs.append(pl.BlockSpec(memory_space=pl.ANY))
        operands.append(S_prev)
    return pl.pallas_call(
        kern,
        grid=(B // nb, NC),
        in_specs=in_specs,
        out_specs=[pl.BlockSpec((nb, L, V), lambda i, n: (i, n, 0)),
                   pl.BlockSpec((nb, CONV_W - 1, CG), lambda i, n: (i, 0, 0)),
                   pl.BlockSpec((None, nb, H, DK, DV), lambda i, n: (layer, i, 0, 0, 0))],
        out_shape=[jax.ShapeDtypeStruct((B, T_pad, V), F32),
                   jax.ShapeDtypeStruct(conv0_all.shape[1:], F32),
                   jax.ShapeDtypeStruct(S0_all.shape, F32)],
        scratch_shapes=[pltpu.VMEM((nb, SUBLANES + L, CG), F32)],
        input_output_aliases=aliases,
        compiler_params=pltpu.CompilerParams(dimension_semantics=("parallel", "arbitrary"),
                                             vmem_limit_bytes=VMEM_LIMIT),
        name="gdn_chunk",
    )(*operands)


def _mlstm_kernel(qk_ref, v_ref, sm_ref, smT_ref, C0_ref, n0_ref, m0_ref, prow_ref, pcol_ref, norm_ref,
                  h_ref, C_ref, n_ref, m_ref, *, nb, L, NC, T_valid, H, DK, DV):
    step = pl.program_id(1)
    K = H * DK
    g0 = 3 * H

    @pl.when(step == 0)
    def _():
        C_ref[...] = C0_ref[...]
        n_ref[...] = n0_ref[...]
        m_ref[...] = m0_ref[...]

    incl, _ = _tri_masks(L)
    lane = lax.broadcasted_iota(jnp.int32, (1, K), 1)
    lane1 = lax.broadcasted_iota(jnp.int32, (1, LANES), 1)
    rows = lax.broadcasted_iota(jnp.int32, (L, 1), 0)
    seqs = range(nb)

    sm = [sm_ref[b] for b in seqs]
    smT = [smT_ref[b, 0] for b in seqs]
    li_c = [x + prow_ref[0:1, :] for x in sm]
    li_r = [x + pcol_ref[:, 0:1] for x in smT]
    lf_c = [-_softplus(-(x + prow_ref[1:2, :])) for x in sm]
    lf_r = [-_softplus(-(x + pcol_ref[:, 1:2])) for x in smT]
    if T_valid < NC * L:
        vc = rows < T_valid
        vr = lax.broadcasted_iota(jnp.int32, (1, L), 1) < T_valid
        li_c = [jnp.where(vc, x, NEG_BIG) for x in li_c]
        li_r = [jnp.where(vr, x, NEG_BIG) for x in li_r]
        lf_c = [jnp.where(vc, x, 0.0) for x in lf_c]
        lf_r = [jnp.where(vr, x, 0.0) for x in lf_r]
    FF = [_cumsum_both(c, r, incl) for c, r in zip(lf_c, lf_r)]
    F_c = [f[0] for f in FF]
    F_r = [f[1] for f in FF]
    r_c = [pltpu.roll(x, H, axis=1) - f for x, f in zip(li_c, F_c)]
    cm = r_c
    s = 1
    while s < L:
        cm = [jnp.maximum(x, jnp.where(rows >= s, pltpu.roll(x, s, axis=0), -jnp.inf)) for x in cm]
        s *= 2
    m_old = [m_ref[b] for b in seqs]
    mx = [jnp.maximum(x, m) for x, m in zip(cm, m_old)]
    mx_last = [x[L - 1:L, :] for x in mx]
    dec = [jnp.exp(m - x) for m, x in zip(m_old, mx_last)]
    wC = [jnp.exp(r - x) for r, x in zip(r_c, mx_last)]
    a_all = [jnp.exp(m - x) for m, x in zip(m_old, mx)]
    floor = [jnp.exp(-f - x) for f, x in zip(F_c, mx)]
    for b in seqs:
        m_ref[b] = jnp.where((lane1 >= g0) & (lane1 < g0 + H), F_c[b][L - 1:L, :] + mx_last[b], 0.0)

    q_all = [qk_ref[b, :, :K] * (DK ** -0.5) for b in seqs]
    k_all = [qk_ref[b, :, K:] for b in seqs]
    k_bf = [x.astype(BF16) for x in k_all]
    C_all = [C_ref[b] for b in seqs]
    C_bf = [x.astype(BF16) for x in C_all]
    n_row = [n_ref[b] for b in seqs]

    half = lane1 < DK
    kw_cols, dec_lanes = [], []
    for b in seqs:
        cols = []
        for c in range(K // LANES):
            w_lo = wC[b][:, g0 + 2 * c:g0 + 2 * c + 1]
            w_hi = wC[b][:, g0 + 2 * c + 1:g0 + 2 * c + 2]
            cols.append(k_all[b][:, c * LANES:(c + 1) * LANES] * jnp.where(half, w_lo, w_hi))
        kw_cols.append(cols)
        d = dec[b][:, g0:g0 + 1]
        for h in range(1, H):
            d = jnp.where(lane >= h * DK, dec[b][:, g0 + h:g0 + h + 1], d)
        dec_lanes.append(d)
    for b in seqs:
        n_upd = jnp.concatenate([jnp.sum(x, axis=0, keepdims=True) for x in kw_cols[b]], axis=1)
        n_ref[b] = dec_lanes[b] * n_row[b] + n_upd

    chains = [(b, h) for b in seqs for h in range(H)]
    in_head = [(lane >= h * DK) & (lane < (h + 1) * DK) for h in range(H)]
    q_h = [jnp.where(in_head[h], q_all[b], 0.0) for b, h in chains]
    q_bf = [x.astype(BF16) for x in q_h]
    v_bf = [v_ref[b, :, h * DV:(h + 1) * DV].astype(BF16) for b, h in chains]
    kw = [kw_cols[b][h // 2][:, (h % 2) * DK:(h % 2 + 1) * DK].astype(BF16) for b, h in chains]
    expD = [jnp.exp(jnp.where(incl, (li_r[b][2 * H + h:2 * H + h + 1, :] - F_r[b][g0 + h:g0 + h + 1, :])
                              - mx[b][:, g0 + h:g0 + h + 1], -jnp.inf)) for b, h in chains]
    qn = [jnp.sum(x * n_row[b], axis=1, keepdims=True) for x, (b, h) in zip(q_h, chains)]

    qk = [_mm_nt(x, k_bf[b]) for x, (b, h) in zip(q_bf, chains)]
    Sm = [e * x for e, x in zip(expD, qk)]
    inter_state = [_mm(x, C_bf[b]) for x, (b, h) in zip(q_bf, chains)]
    intra = [_mm(x, v) for x, v in zip(Sm, v_bf)]
    upd = _mm_tn_list(kw, v_bf)
    rowsum = [jnp.sum(x, axis=1, keepdims=True) for x in Sm]
    for i, (b, h) in enumerate(chains):
        C_ref[b, h * DK:(h + 1) * DK, :] = dec[b][:, g0 + h:g0 + h + 1] * C_all[b][h * DK:(h + 1) * DK, :] + upd[i]
    a_h = [a_all[b][:, g0 + h:g0 + h + 1] for b, h in chains]
    den = [a * x + y for a, x, y in zip(a_h, qn, rowsum)]
    scale = [1.0 / jnp.maximum(jnp.abs(d), floor[b][:, g0 + h:g0 + h + 1]) for d, (b, h) in zip(den, chains)]
    hh = [(a * x + y) * sc for a, x, y, sc in zip(a_h, inter_state, intra, scale)]
    ms = [jnp.mean(x * x, axis=-1, keepdims=True) for x in hh]
    rs = [lax.rsqrt(x + EPS) for x in ms]
    for x, r, (b, h) in zip(hh, rs, chains):
        h_ref[b, :, h * DV:(h + 1) * DV] = x * r * norm_ref[:, h * DV:(h + 1) * DV]


def _mlstm(proj, small, smT, C0_all, layer, C_prev, n0_all, m0_all, prow, pcol, norm, *, NC, L, T_valid, offs):
    depth, B, K, DV = C0_all.shape
    V = norm.shape[-1]
    H = V // DV
    DK = K // H
    T_pad = proj.shape[1]
    nb = _tile(B, max(1, 2 * CHAINS // H), 1)
    assert 2 * DK == LANES and 4 * H <= LANES
    kern = functools.partial(_mlstm_kernel, nb=nb, L=L, NC=NC, T_valid=T_valid, H=H, DK=DK, DV=DV)
    in_specs = [pl.BlockSpec((nb, L, 2 * K), lambda i, n: (i, n, offs["qk_m"] // (2 * K))),
                pl.BlockSpec((nb, L, V), lambda i, n: (i, n, offs["v_m"] // V)),
                pl.BlockSpec((nb, L, LANES), lambda i, n: (i, n, 0)),
                pl.BlockSpec((nb, 1, GATE_ROWS, L), lambda i, n: (i, n, 0, 0)),
                pl.BlockSpec((None, nb, K, DV), lambda i, n: (layer, i, 0, 0)),
                pl.BlockSpec((None, nb, 1, K), lambda i, n: (layer, i, 0, 0)),
                pl.BlockSpec((None, nb, 1, LANES), lambda i, n: (layer, i, 0, 0)),
                _layer_spec(layer, 2, LANES),
                _layer_spec(layer, GATE_ROWS, 2),
                _layer_spec(layer, 1, V)]
    operands = [proj, proj, small, smT, C0_all, n0_all, m0_all, prow, pcol, norm]
    aliases = {}
    if C_prev is not None:
        kern = _skip_ref(kern, len(operands))
        aliases = {len(operands): 1}
        in_specs.append(pl.BlockSpec(memory_space=pl.ANY))
        operands.append(C_prev)
    outs = pl.pallas_call(
        kern,
        grid=(B // nb, NC),
        in_specs=in_specs,
        out_specs=[pl.BlockSpec((nb, L, V), lambda i, n: (i, n, 0)),
                   pl.BlockSpec((None, nb, K, DV), lambda i, n: (layer, i, 0, 0)),
                   pl.BlockSpec((nb, 1, K), lambda i, n: (i, 0, 0)),
                   pl.BlockSpec((nb, 1, LANES), lambda i, n: (i, 0, 0))],
        out_shape=[jax.ShapeDtypeStruct((B, T_pad, V), F32),
                   jax.ShapeDtypeStruct(C0_all.shape, F32),
                   jax.ShapeDtypeStruct((B, 1, K), F32),
                   jax.ShapeDtypeStruct((B, 1, LANES), F32)],
        input_output_aliases=aliases,
        compiler_params=pltpu.CompilerParams(dimension_semantics=("parallel", "arbitrary"),
                                             vmem_limit_bytes=VMEM_LIMIT),
        name="mlstm_chunk",
    )(*operands)
    hm, C_all, n_new, m_new = outs
    return hm, C_all, n_new.reshape(B, H, DK), m_new[:, 0, 3 * H:4 * H]


def _merge_kernel(og_ref, hm_ref, x_ref, g_ref, wg_ref, wbg_ref, wbm_ref, wout_ref, o_ref, *, Vg, Vm, D):
    dot = functools.partial(jnp.dot, preferred_element_type=F32)
    parts = _row_parts(x_ref.shape[0])
    x = [x_ref[p, :] for p in parts]
    xn = [(v * lax.rsqrt(jnp.mean(v * v, axis=-1, keepdims=True) + EPS) * g_ref[...]).astype(BF16) for v in x]
    z = [dot(v, wg_ref[:, 0:Vg]) for v in xn]
    og = [(og_ref[p, :] * (v * _sigmoid(v))).astype(BF16) for p, v in zip(parts, z)]
    br_g = [dot(v, wbg_ref[...]) for v in og]
    om = [dot(v, wg_ref[:, Vg:Vg + Vm]) for v in xn]
    hm = [(hm_ref[p, :] * _sigmoid(v)).astype(BF16) for p, v in zip(parts, om)]
    br_m = [dot(v, wbm_ref[...]) for v in hm]
    gg = [dot(v, wg_ref[:, Vg + Vm:Vg + Vm + D]) for v in xn]
    merged = [_sigmoid(g) * b for g, b in zip(gg, br_g)]
    gm = [dot(v, wg_ref[:, Vg + Vm + D:]) for v in xn]
    merged = [m + _sigmoid(g) * b for m, g, b in zip(merged, gm, br_m)]
    for p, v, m in zip(parts, x, merged):
        o_ref[p, :] = v + dot(m.astype(BF16), wout_ref[...])


def _merge(og, hm, x, gamma, wg, wbg, wbm, wout, layer):
    M, D = x.shape
    Vg = og.shape[1]
    Vm = hm.shape[1]
    tm = _tile(M, 512, SUBLANES * ROW_PARTS)
    return pl.pallas_call(
        functools.partial(_merge_kernel, Vg=Vg, Vm=Vm, D=D),
        grid=(M // tm,),
        in_specs=[pl.BlockSpec((tm, Vg), lambda i: (i, 0)),
                  pl.BlockSpec((tm, Vm), lambda i: (i, 0)),
                  pl.BlockSpec((tm, D), lambda i: (i, 0)),
                  _layer_spec(layer, 1, D),
                  _layer_spec(layer, D, Vg + Vm + 2 * D),
                  _layer_spec(layer, Vg, D),
                  _layer_spec(layer, Vm, D),
                  _layer_spec(layer, D, D)],
        out_specs=pl.BlockSpec((tm, D), lambda i: (i, 0)),
        out_shape=jax.ShapeDtypeStruct((M, D), F32),
        compiler_params=pltpu.CompilerParams(dimension_semantics=("parallel",), vmem_limit_bytes=VMEM_LIMIT),
        name="merge_out",
    )(og, hm, x, gamma, wg, wbg, wbm, wout)


def _mlp_kernel(x_ref, g_ref, wup_ref, wdn_ref, gf_ref, o_ref, *, final_norm, tf):
    x = x_ref[...]
    ms = jnp.mean(x * x, axis=-1, keepdims=True)
    xn = (x * lax.rsqrt(ms + EPS) * g_ref[...]).astype(BF16)
    y = x
    for c in range(wup_ref.shape[1] // tf):
        hcol = jnp.maximum(jnp.dot(xn, wup_ref[:, c * tf:(c + 1) * tf], preferred_element_type=F32), 0.0)
        y = y + jnp.dot((hcol * hcol).astype(BF16), wdn_ref[c * tf:(c + 1) * tf, :], preferred_element_type=F32)
    if final_norm:
        ms = jnp.mean(y * y, axis=-1, keepdims=True)
        y = y * lax.rsqrt(ms + EPS) * gf_ref[...]
    o_ref[...] = y


def _mlp(x, gamma, wup, wdn, gamma_final, layer, *, final_norm):
    M, D = x.shape
    FF = wup.shape[2]
    tm = _tile(M, 512, SUBLANES)
    tf = _tile(FF, 1024, LANES)
    resident = pl.Buffered(1)
    return pl.pallas_call(
        functools.partial(_mlp_kernel, final_norm=final_norm, tf=tf),
        grid=(M // tm,),
        in_specs=[pl.BlockSpec((tm, D), lambda i: (i, 0)),
                  _layer_spec(layer, 1, D),
                  pl.BlockSpec((None, D, FF), lambda i: (layer, 0, 0), pipeline_mode=resident),
                  pl.BlockSpec((None, FF, D), lambda i: (layer, 0, 0), pipeline_mode=resident),
                  pl.BlockSpec((1, D), lambda i: (0, 0))],
        out_specs=pl.BlockSpec((tm, D), lambda i: (i, 0)),
        out_shape=jax.ShapeDtypeStruct((M, D), F32),
        compiler_params=pltpu.CompilerParams(dimension_semantics=("parallel",), vmem_limit_bytes=VMEM_LIMIT),
        name="mlp",
    )(x, gamma, wup, wdn, gamma_final)


def _pack_w_in(w_in, dims):
    D, Hg, DKg, DVg, Hm, DKm, DVm = dims
    Kg, Vg, Km, Vm = Hg * DKg, Hg * DVg, Hm * DKm, Hm * DVm
    sizes = (Kg, Kg, Vg, Vg, Hg, Hg, Km, Km, Vm, Vm, Hm, Hm, D, D)
    starts = [0]
    for s in sizes:
        starts.append(starts[-1] + s)
    col = lambda a, b: w_in[:, :, starts[a]:starts[b]]
    assert 2 * Hg + 2 * Hm <= GATE_ROWS and Hg == Hm
    pad = jnp.zeros(w_in.shape[:2] + (LANES - 2 * Hg - 2 * Hm,), w_in.dtype)
    segs = [("qkv", col(0, 3)), ("qk_m", col(6, 8)), ("v_m", col(8, 9)),
            ("small", jnp.concatenate([col(4, 6), col(10, 12), pad], axis=2))]
    offs, off = {}, 0
    for name, seg in segs:
        assert off % seg.shape[2] == 0, (name, off, seg.shape)
        offs[name] = off
        off += seg.shape[2]
    w_rec = jnp.concatenate([s for _, s in segs], axis=2).astype(BF16)
    w_gate = jnp.concatenate([col(3, 4), col(9, 10), col(12, 14)], axis=2).astype(BF16)
    return w_rec, w_gate, offs


def _gate_params(first, lane_first, second, lane_second):
    depth, H = first.shape
    place = lambda v, lane: jnp.pad(v.astype(F32), ((0, 0), (lane, LANES - lane - H)))
    row = jnp.stack([place(first, lane_first), place(second, lane_second)], axis=1)
    return row, jnp.swapaxes(row[:, :, :GATE_ROWS], 1, 2)


def _head_norm_rows(w, H, DV):
    depth = w.shape[0]
    return jnp.broadcast_to(w.astype(F32).reshape(depth, -1, DV), (depth, H, DV)).reshape(depth, 1, H * DV)


def _trunk(x, conv0, S0, C0, n0, m0, P, *, B, T_valid, L):
    M, D = x.shape
    NC = -(-T_valid // L)
    T_pad = NC * L
    pad_t = lambda a: a if T_pad == T_valid else jnp.pad(a, ((0, 0), (0, T_pad - T_valid), (0, 0)))
    depth = P["w_in"].shape[0]
    offs = P["offs"]
    _, _, H, DK, DV = C0.shape
    C0_all = C0.reshape(depth, B, H * DK, DV)
    n0_all = n0.reshape(depth, B, 1, H * DK)
    m0_all = jnp.pad(m0.reshape(depth, B, 1, H), ((0, 0), (0, 0), (0, 0), (3 * H, LANES - 4 * H)))
    new = ([], [], [])
    S_all, C_all = None, None
    for l in range(depth):
        proj, small = _norm_proj(x, P["norm_mix"], P["w_in"], l)
        proj3 = pad_t(proj.reshape(B, T_valid, proj.shape[1]))
        small3 = pad_t(small.reshape(B, T_valid, LANES))
        smT = jnp.swapaxes(small3[:, :, :GATE_ROWS].reshape(B, NC, L, GATE_ROWS), 2, 3)
        og, conv_n, S_all = _gdn(proj3, small3, smT, conv0, S0, l, S_all, P["conv_w"], P["gdn_prow"],
                                 P["gdn_pcol"], P["gdn_norm"], NC=NC, L=L, T_valid=T_valid, offs=offs)
        hm, C_all, n_n, m_n = _mlstm(proj3, small3, smT, C0_all, l, C_all, n0_all, m0_all, P["ml_prow"],
                                     P["ml_pcol"], P["ml_norm"], NC=NC, L=L, T_valid=T_valid, offs=offs)
        x = _merge(og[:, :T_valid].reshape(M, -1), hm[:, :T_valid].reshape(M, -1), x, P["norm_mix"], P["w_gate"],
                   P["w_bg"], P["w_bm"], P["w_out"], l)
        x = _mlp(x, P["norm_mlp"], P["w_up"], P["w_down"], P["norm_final"], l, final_norm=(l == depth - 1))
        for lst, s in zip(new, (conv_n, n_n, m_n)):
            lst.append(s)
    conv_new, n_new, m_new = (jnp.stack(lst) for lst in new)
    return x, conv_new, S_all, C_all.reshape(C0.shape), n_new, m_new


def kernel(x_prompt, x_sample, state_gdn_conv, state_gdn_S, state_mlstm_C, state_mlstm_n, state_mlstm_m, norm_mix, w_in, gdn_conv_w, gdn_A_log, gdn_dt_bias, gdn_norm, ml_i_bias, ml_f_bias, ml_norm, w_branch_gdn, w_branch_ml, w_out, norm_mlp, w_up, w_down, norm_final):
    Bp, Tp, D = x_prompt.shape
    Bs, Ts, _ = x_sample.shape
    depth = w_in.shape[0]
    _, _, Hg, DKg, DVg = state_gdn_S.shape
    _, _, Hm, DKm, DVm = state_mlstm_C.shape
    CG = state_gdn_conv.shape[-1]
    dims = (D, Hg, DKg, DVg, Hm, DKm, DVm)
    assert Ts >= CONV_W - 1 and Tp >= CONV_W - 1

    w_rec, w_gate, offs = _pack_w_in(w_in, dims)
    P = {
        "offs": offs,
        "w_in": w_rec,
        "w_gate": w_gate,
        "norm_mix": norm_mix.reshape(depth, 1, D),
        "norm_mlp": norm_mlp.reshape(depth, 1, D),
        "norm_final": norm_final.reshape(1, D),
        "conv_w": gdn_conv_w,
        "w_bg": w_branch_gdn.astype(BF16),
        "w_bm": w_branch_ml.astype(BF16),
        "w_out": w_out.astype(BF16),
        "w_up": w_up.astype(BF16),
        "w_down": w_down.astype(BF16),
        "gdn_norm": _head_norm_rows(gdn_norm, Hg, DVg),
        "ml_norm": _head_norm_rows(ml_norm, Hm, DVm),
    }
    P["gdn_prow"], P["gdn_pcol"] = _gate_params(gdn_A_log, Hg, gdn_dt_bias, Hg)
    P["ml_prow"], P["ml_pcol"] = _gate_params(ml_i_bias, 2 * Hg, ml_f_bias, 2 * Hg + Hm)

    Lp = CHUNK if Tp % CHUNK == 0 else Tp
    assert Lp % SUBLANES == 0
    zeros = lambda *s: jnp.zeros(s, F32)
    yp, conv_p, S_p, C_p, n_p, m_p = _trunk(
        x_prompt.reshape(Bp * Tp, D), zeros(depth, Bp, CONV_W - 1, CG), zeros(depth, Bp, Hg, DKg, DVg),
        zeros(depth, Bp, Hm, DKm, DVm), zeros(depth, Bp, Hm, DKm), zeros(depth, Bp, Hm), P,
        B=Bp, T_valid=Tp, L=Lp)

    Ls = -(-Ts // SUBLANES) * SUBLANES
    ys, conv_s, S_s, C_s, n_s, m_s = _trunk(
        x_sample.reshape(Bs * Ts, D), state_gdn_conv, state_gdn_S, state_mlstm_C, state_mlstm_n, state_mlstm_m, P,
        B=Bs, T_valid=Ts, L=Ls)
    return (yp.reshape(Bp, Tp, D), ys.reshape(Bs, Ts, D), conv_p, S_p, C_p, n_p, m_p, conv_s, S_s, C_s, n_s, m_s)
```

```python
import functools

import jax
import jax.numpy as jnp
from jax import lax
from jax.experimental import pallas as pl
from jax.experimental.pallas import tpu as pltpu

F32 = jnp.float32
BF16 = jnp.bfloat16
EPS = 1e-6
CONV_W = 4
LANES = 128
SUBLANES = 8
GATE_ROWS = 16
CHUNK = 64
CHAINS = 32
ROW_PARTS = 2
NEG_BIG = -1e30
VMEM_LIMIT = 48 * 1024 * 1024
HIGHEST = lax.Precision.HIGHEST


def _sigmoid(x):
    return 1.0 / (1.0 + jnp.exp(-x))


def _softplus(x):
    return jnp.maximum(x, 0.0) + jnp.log(1.0 + jnp.exp(-jnp.abs(x)))


def _mm(a, b):
    return jnp.dot(a.astype(BF16), b.astype(BF16), preferred_element_type=F32)


def _mm_nt(a, b):
    return lax.dot_general(a.astype(BF16), b.astype(BF16), (((1,), (1,)), ((), ())), preferred_element_type=F32)


def _mm_tn(a, b):
    return lax.dot_general(a.astype(BF16), b.astype(BF16), (((0,), (0,)), ((), ())), preferred_element_type=F32)


def _mm_tn_list(As, Bs):
    K, M = As[0].shape
    if K < 64:
        return [_mm_tn(a, b) for a, b in zip(As, Bs)]
    eye = jnp.where(lax.broadcasted_iota(jnp.int32, (M, M), 0) == lax.broadcasted_iota(jnp.int32, (M, M), 1),
                    1.0, 0.0).astype(BF16)
    a_t = [_mm_nt(eye, a) for a in As]
    return [_mm(a, b) for a, b in zip(a_t, Bs)]


def _split_bf16(a):
    hi = a.astype(BF16)
    return hi, (a - hi.astype(F32)).astype(BF16)


def _tile(n, cap, mult):
    best = None
    for t in range(mult, min(n, cap) + 1, mult):
        if n % t == 0:
            best = t
    return best if best is not None else n


def _tri_masks(L):
    row = lax.broadcasted_iota(jnp.int32, (L, L), 0)
    col = lax.broadcasted_iota(jnp.int32, (L, L), 1)
    return row >= col, row > col


def _cumsum_both(g_col, g_row, incl):
    L = incl.shape[0]
    row = lax.broadcasted_iota(jnp.int32, (L, L), 0)
    col = lax.broadcasted_iota(jnp.int32, (L, L), 1)
    lower = jnp.where(incl, 1.0, 0.0)
    upper = jnp.where(row <= col, 1.0, 0.0)
    c_col = jnp.dot(lower, g_col, precision=HIGHEST, preferred_element_type=F32)
    c_row = jnp.dot(g_row, upper, precision=HIGHEST, preferred_element_type=F32)
    return c_col, c_row


def _tri_inv_unit_lower(As):
    L = As[0].shape[0]
    row = lax.broadcasted_iota(jnp.int32, (L, L), 0)
    col = lax.broadcasted_iota(jnp.int32, (L, L), 1)
    eye = jnp.where(row == col, 1.0, 0.0)
    levels = max(1, (L - 1).bit_length())
    Xs = [eye - A for A in As]
    if levels > 1:
        Ps = [_mm(A, A) for A in As]
        for _ in range(1, levels - 1):
            Rs = [_mm(jnp.concatenate([P, X], axis=0), P) for P, X in zip(Ps, Xs)]
            Ps = [R[:L] for R in Rs]
            Xs = [X + R[L:] for X, R in zip(Xs, Rs)]
        Xs = [X + _mm(X, P) for X, P in zip(Xs, Ps)]
    splits = [(_split_bf16(A), _split_bf16(X)) for A, X in zip(As, Xs)]
    AX1 = [jnp.dot(jnp.concatenate([a_hi, a_lo], axis=0), x_hi, preferred_element_type=F32)
           for (a_hi, a_lo), (x_hi, _) in splits]
    AX2 = [jnp.dot(a_hi, x_lo, preferred_element_type=F32) for (a_hi, _), (_, x_lo) in splits]
    Rs = [(eye - X) - (r1[:L] + (r1[L:] + r2)) for X, r1, r2 in zip(Xs, AX1, AX2)]
    return Xs, Rs


def _layer_spec(layer, *shape):
    return pl.BlockSpec((None,) + shape, lambda *_: (layer,) + (0,) * len(shape))


def _row_parts(tm):
    return [pl.ds(r * (tm // ROW_PARTS), tm // ROW_PARTS) for r in range(ROW_PARTS)]


def _norm_proj_kernel(x_ref, g_ref, w_ref, o_ref, s_ref):
    parts = _row_parts(x_ref.shape[0])
    x = [x_ref[p, :] for p in parts]
    xn = [(v * lax.rsqrt(jnp.mean(v * v, axis=-1, keepdims=True) + EPS) * g_ref[...]).astype(BF16) for v in x]
    res = [jnp.dot(v, w_ref[...], preferred_element_type=F32) for v in xn]
    for p, r in zip(parts, res):
        o_ref[p, :] = r[:, :o_ref.shape[1]]
        s_ref[p, :] = r[:, o_ref.shape[1]:]


def _norm_proj(x, gamma_all, w_all, layer):
    M, D = x.shape
    N = w_all.shape[2]
    tm = _tile(M, 512, SUBLANES * ROW_PARTS)
    return pl.pallas_call(
        _norm_proj_kernel,
        grid=(M // tm,),
        in_specs=[pl.BlockSpec((tm, D), lambda i: (i, 0)),
                  _layer_spec(layer, 1, D),
                  _layer_spec(layer, D, N)],
        out_specs=[pl.BlockSpec((tm, N - LANES), lambda i: (i, 0)),
                   pl.BlockSpec((tm, LANES), lambda i: (i, 0))],
        out_shape=[jax.ShapeDtypeStruct((M, N - LANES), F32),
                   jax.ShapeDtypeStruct((M, LANES), F32)],
        compiler_params=pltpu.CompilerParams(dimension_semantics=("parallel",), vmem_limit_bytes=VMEM_LIMIT),
        name="norm_proj",
    )(x, gamma_all, w_all)


def _gdn_prep_seq(b, qkv_ref, sm_ref, smT_ref, xp_ref, convn_ref, cw_ref, prow_ref, pcol_ref,
                  *, L, NC, T_valid, H, DK, DV):
    K = H * DK
    pad0 = SUBLANES - (CONV_W - 1)
    incl, _ = _tri_masks(L)
    lv = L if NC > 1 else T_valid
    xp_ref[b, SUBLANES:SUBLANES + L, :] = qkv_ref[b]
    xfull = xp_ref[b]
    y = xfull * cw_ref[CONV_W - 1:CONV_W, :]
    for j in range(CONV_W - 1):
        y = y + pltpu.roll(xfull, CONV_W - 1 - j, axis=0) * cw_ref[j:j + 1, :]
    y = y[SUBLANES:, :]
    qkv = y * _sigmoid(y)
    tail = xp_ref[b, SUBLANES + lv - (CONV_W - 1):SUBLANES + lv, :]
    xp_ref[b, pad0:SUBLANES, :] = tail
    convn_ref[b] = tail

    sm = sm_ref[b]
    smT = smT_ref[b, 0]
    beta_c = _sigmoid(sm)
    g_c = -jnp.exp(prow_ref[0:1, :]) * _softplus(sm + prow_ref[1:2, :])
    g_r = -jnp.exp(pcol_ref[:, 0:1]) * _softplus(smT + pcol_ref[:, 1:2])
    if T_valid < NC * L:
        vc = lax.broadcasted_iota(jnp.int32, (L, 1), 0) < T_valid
        vr = lax.broadcasted_iota(jnp.int32, (1, L), 1) < T_valid
        beta_c = jnp.where(vc, beta_c, 0.0)
        g_c = jnp.where(vc, g_c, 0.0)
        g_r = jnp.where(vr, g_r, 0.0)
    gc_c, gc_r = _cumsum_both(g_c, g_r, incl)
    chains = []
    for h in range(H):
        qh = qkv[:, h * DK:(h + 1) * DK]
        kh = qkv[:, K + h * DK:K + (h + 1) * DK]
        vh = qkv[:, 2 * K + h * DV:2 * K + (h + 1) * DV]
        qh = qh * lax.rsqrt(jnp.sum(qh * qh, axis=-1, keepdims=True) + EPS) * (DK ** -0.5)
        kh = kh * lax.rsqrt(jnp.sum(kh * kh, axis=-1, keepdims=True) + EPS)
        b_c = beta_c[:, h:h + 1]
        gcc = gc_c[:, H + h:H + h + 1]
        gcr = gc_r[H + h:H + h + 1, :]
        gl = gcc[L - 1:L, :]
        eg = jnp.exp(gcc)
        kb = kh * b_c
        chains.append(dict(
            kh=kh.astype(BF16), kbq=jnp.concatenate([kb, qh], axis=0).astype(BF16),
            decay=jnp.exp(jnp.where(incl, gcc - gcr, -jnp.inf)),
            rhs=jnp.concatenate([kb * eg, vh * b_c], axis=1), qd=(qh * eg).astype(BF16),
            kd=(kh * jnp.exp(gl - gcc)).astype(BF16), dl=jnp.exp(gl)))
    return chains


def _gdn_chain_phase(chains, S_ref, o_ref, gnorm_ref, *, L, DK, DV):
    _, strict = _tri_masks(L)
    ops = [c[2] for c in chains]
    r1 = [_mm_nt(c["kbq"], c["kh"]) for c in ops]
    A = [jnp.where(strict, r[:L] * c["decay"], 0.0) for r, c in zip(r1, ops)]
    qk = [(r[L:] * c["decay"]).astype(BF16) for r, c in zip(r1, ops)]
    Xs, Rs = _tri_inv_unit_lower(A)
    corr = [_mm(R, c["rhs"]) for R, c in zip(Rs, ops)]
    wu = [_mm(X, c["rhs"] + cr) for X, c, cr in zip(Xs, ops, corr)]
    states = [S_ref[b, h] for b, h, _ in chains]
    wq = [_mm(jnp.concatenate([x[:, :DK].astype(BF16), c["qd"]], axis=0), S) for x, c, S in zip(wu, ops, states)]
    v_new = [(x[:, DK:] - y[:L]).astype(BF16) for x, y in zip(wu, wq)]
    o_intra = [_mm(q, v) for q, v in zip(qk, v_new)]
    s_upd = [_mm_tn(c["kd"], v) for c, v in zip(ops, v_new)]
    for (b, h, c), S, su in zip(chains, states, s_upd):
        S_ref[b, h] = S * c["dl"] + su
    o = [y[L:] + oi for y, oi in zip(wq, o_intra)]
    rs = [lax.rsqrt(jnp.mean(x * x, axis=-1, keepdims=True) + EPS) for x in o]
    for x, r, (b, h, _) in zip(o, rs, chains):
        o_ref[b, :, h * DV:(h + 1) * DV] = x * r * gnorm_ref[:, h * DV:(h + 1) * DV]


def _gdn_kernel(qkv_ref, sm_ref, smT_ref, conv0_ref, S0_ref, cw_ref, prow_ref, pcol_ref, gnorm_ref,
                o_ref, convn_ref, S_ref, xp_ref, *, nb, L, NC, T_valid, H, DK, DV):
    pad0 = SUBLANES - (CONV_W - 1)

    @pl.when(pl.program_id(1) == 0)
    def _():
        xp_ref[:, pad0:SUBLANES, :] = conv0_ref[...]
        S_ref[...] = S0_ref[...]

    chains = []
    for b in range(nb):
        ops = _gdn_prep_seq(b, qkv_ref, sm_ref, smT_ref, xp_ref, convn_ref, cw_ref, prow_ref, pcol_ref,
                            L=L, NC=NC, T_valid=T_valid, H=H, DK=DK, DV=DV)
        chains += [(b, h, c) for h, c in enumerate(ops)]
    _gdn_chain_phase(chains, S_ref, o_ref, gnorm_ref, L=L, DK=DK, DV=DV)


def _skip_ref(kernel_fn, index):
    def wrapped(*refs, **kw):
        kernel_fn(*refs[:index], *refs[index + 1:], **kw)
    return wrapped


def _gdn(proj, small, smT, conv0_all, S0_all, layer, S_prev, cw, prow, pcol, gnorm, *, NC, L, T_valid, offs):
    depth, B, H, DK, DV = S0_all.shape
    CG = conv0_all.shape[-1]
    V = H * DV
    T_pad = proj.shape[1]
    nb = _tile(B, max(1, CHAINS // H), 1)
    kern = functools.partial(_gdn_kernel, nb=nb, L=L, NC=NC, T_valid=T_valid, H=H, DK=DK, DV=DV)
    in_specs = [pl.BlockSpec((nb, L, CG), lambda i, n: (i, n, offs["qkv"] // CG)),
                pl.BlockSpec((nb, L, LANES), lambda i, n: (i, n, 0)),
                pl.BlockSpec((nb, 1, GATE_ROWS, L), lambda i, n: (i, n, 0, 0)),
                pl.BlockSpec((None, nb, CONV_W - 1, CG), lambda i, n: (layer, i, 0, 0)),
                pl.BlockSpec((None, nb, H, DK, DV), lambda i, n: (layer, i, 0, 0, 0)),
                _layer_spec(layer, CONV_W, CG),
                _layer_spec(layer, 2, LANES),
                _layer_spec(layer, GATE_ROWS, 2),
                _layer_spec(layer, 1, V)]
    operands = [proj, small, smT, conv0_all, S0_all, cw, prow, pcol, gnorm]
    aliases = {}
    if S_prev is not None:
        kern = _skip_ref(kern, len(operands))
        aliases = {len(operands): 2}
        in_specs.append(pl.BlockSpec(memory_space=pl.ANY))
        operands.append(S_prev)
    return pl.pallas_call(
        kern,
        grid=(B // nb, NC),
        in_specs=in_specs,
        out_specs=[pl.BlockSpec((nb, L, V), lambda i, n: (i, n, 0)),
                   pl.BlockSpec((nb, CONV_W - 1, CG), lambda i, n: (i, 0, 0)),
                   pl.BlockSpec((None, nb, H, DK, DV), lambda i, n: (layer, i, 0, 0, 0))],
        out_shape=[jax.ShapeDtypeStruct((B, T_pad, V), F32),
                   jax.ShapeDtypeStruct(conv0_all.shape[1:], F32),
                   jax.ShapeDtypeStruct(S0_all.shape, F32)],
        scratch_shapes=[pltpu.VMEM((nb, SUBLANES + L, CG), F32)],
        input_output_aliases=aliases,
        compiler_params=pltpu.CompilerParams(dimension_semantics=("parallel", "arbitrary"),
                                             vmem_limit_bytes=VMEM_LIMIT),
        name="gdn_chunk",
    )(*operands)


def _mlstm_kernel(qk_ref, v_ref, sm_ref, smT_ref, C0_ref, n0_ref, m0_ref, prow_ref, pcol_ref, norm_ref,
                  h_ref, C_ref, n_ref, m_ref, *, nb, L, NC, T_valid, H, DK, DV):
    step = pl.program_id(1)
    K = H * DK
    g0 = 3 * H

    @pl.when(step == 0)
    def _():
        C_ref[...] = C0_ref[...]
        n_ref[...] = n0_ref[...]
        m_ref[...] = m0_ref[...]

    incl, _ = _tri_masks(L)
    lane = lax.broadcasted_iota(jnp.int32, (1, K), 1)
    lane1 = lax.broadcasted_iota(jnp.int32, (1, LANES), 1)
    rows = lax.broadcasted_iota(jnp.int32, (L, 1), 0)
    seqs = range(nb)

    sm = [sm_ref[b] for b in seqs]
    smT = [smT_ref[b, 0] for b in seqs]
    li_c = [x + prow_ref[0:1, :] for x in sm]
    li_r = [x + pcol_ref[:, 0:1] for x in smT]
    lf_c = [-_softplus(-(x + prow_ref[1:2, :])) for x in sm]
    lf_r = [-_softplus(-(x + pcol_ref[:, 1:2])) for x in smT]
    if T_valid < NC * L:
        vc = rows < T_valid
        vr = lax.broadcasted_iota(jnp.int32, (1, L), 1) < T_valid
        li_c = [jnp.where(vc, x, NEG_BIG) for x in li_c]
        li_r = [jnp.where(vr, x, NEG_BIG) for x in li_r]
        lf_c = [jnp.where(vc, x, 0.0) for x in lf_c]
        lf_r = [jnp.where(vr, x, 0.0) for x in lf_r]
    FF = [_cumsum_both(c, r, incl) for c, r in zip(lf_c, lf_r)]
    F_c = [f[0] for f in FF]
    F_r = [f[1] for f in FF]
    r_c = [pltpu.roll(x, H, axis=1) - f for x, f in zip(li_c, F_c)]
    cm = r_c
    s = 1
    while s < L:
        cm = [jnp.maximum(x, jnp.where(rows >= s, pltpu.roll(x, s, axis=0), -jnp.inf)) for x in cm]
        s *= 2
    m_old = [m_ref[b] for b in seqs]
    mx = [jnp.maximum(x, m) for x, m in zip(cm, m_old)]
    mx_last = [x[L - 1:L, :] for x in mx]
    dec = [jnp.exp(m - x) for m, x in zip(m_old, mx_last)]
    wC = [jnp.exp(r - x) for r, x in zip(r_c, mx_last)]
    a_all = [jnp.exp(m - x) for m, x in zip(m_old, mx)]
    floor = [jnp.exp(-f - x) for f, x in zip(F_c, mx)]
    for b in seqs:
        m_ref[b] = jnp.where((lane1 >= g0) & (lane1 < g0 + H), F_c[b][L - 1:L, :] + mx_last[b], 0.0)

    q_all = [qk_ref[b, :, :K] * (DK ** -0.5) for b in seqs]
    k_all = [qk_ref[b, :, K:] for b in seqs]
    k_bf = [x.astype(BF16) for x in k_all]
    C_all = [C_ref[b] for b in seqs]
    C_bf = [x.astype(BF16) for x in C_all]
    n_row = [n_ref[b] for b in seqs]

    half = lane1 < DK
    kw_cols, dec_lanes = [], []
    for b in seqs:
        cols = []
        for c in range(K // LANES):
            w_lo = wC[b][:, g0 + 2 * c:g0 + 2 * c + 1]
            w_hi = wC[b][:, g0 + 2 * c + 1:g0 + 2 * c + 2]
            cols.append(k_all[b][:, c * LANES:(c + 1) * LANES] * jnp.where(half, w_lo, w_hi))
        kw_cols.append(cols)
        d = dec[b][:, g0:g0 + 1]
        for h in range(1, H):
            d = jnp.where(lane >= h * DK, dec[b][:, g0 + h:g0 + h + 1], d)
        dec_lanes.append(d)
    for b in seqs:
        n_upd = jnp.concatenate([jnp.sum(x, axis=0, keepdims=True) for x in kw_cols[b]], axis=1)
        n_ref[b] = dec_lanes[b] * n_row[b] + n_upd

    chains = [(b, h) for b in seqs for h in range(H)]
    in_head = [(lane >= h * DK) & (lane < (h + 1) * DK) for h in range(H)]
    q_h = [jnp.where(in_head[h], q_all[b], 0.0) for b, h in chains]
    q_bf = [x.astype(BF16) for x in q_h]
    v_bf = [v_ref[b, :, h * DV:(h + 1) * DV].astype(BF16) for b, h in chains]
    kw = [kw_cols[b][h // 2][:, (h % 2) * DK:(h % 2 + 1) * DK].astype(BF16) for b, h in chains]
    expD = [jnp.exp(jnp.where(incl, (li_r[b][2 * H + h:2 * H + h + 1, :] - F_r[b][g0 + h:g0 + h + 1, :])
                              - mx[b][:, g0 + h:g0 + h + 1], -jnp.inf)) for b, h in chains]
    qn = [jnp.sum(x * n_row[b], axis=1, keepdims=True) for x, (b, h) in zip(q_h, chains)]

    qk = [_mm_nt(x, k_bf[b]) for x, (b, h) in zip(q_bf, chains)]
    Sm = [e * x for e, x in zip(expD, qk)]
    inter_state = [_mm(x, C_bf[b]) for x, (b, h) in zip(q_bf, chains)]
    intra = [_mm(x, v) for x, v in zip(Sm, v_bf)]
    upd = _mm_tn_list(kw, v_bf)
    rowsum = [jnp.sum(x, axis=1, keepdims=True) for x in Sm]
    for i, (b, h) in enumerate(chains):
        C_ref[b, h * DK:(h + 1) * DK, :] = dec[b][:, g0 + h:g0 + h + 1] * C_all[b][h * DK:(h + 1) * DK, :] + upd[i]
    a_h = [a_all[b][:, g0 + h:g0 + h + 1] for b, h in chains]
    den = [a * x + y for a, x, y in zip(a_h, qn, rowsum)]
    scale = [1.0 / jnp.maximum(jnp.abs(d), floor[b][:, g0 + h:g0 + h + 1]) for d, (b, h) in zip(den, chains)]
    hh = [(a * x + y) * sc for a, x, y, sc in zip(a_h, inter_state, intra, scale)]
    ms = [jnp.mean(x * x, axis=-1, keepdims=True) for x in hh]
    rs = [lax.rsqrt(x + EPS) for x in ms]
    for x, r, (b, h) in zip(hh, rs, chains):
        h_ref[b, :, h * DV:(h + 1) * DV] = x * r * norm_ref[:, h * DV:(h + 1) * DV]


def _mlstm(proj, small, smT, C0_all, layer, C_prev, n0_all, m0_all, prow, pcol, norm, *, NC, L, T_valid, offs):
    depth, B, K, DV = C0_all.shape
    V = norm.shape[-1]
    H = V // DV
    DK = K // H
    T_pad = proj.shape[1]
    nb = _tile(B, max(1, CHAINS // H), 1)
    assert 2 * DK == LANES and 4 * H <= LANES
    kern = functools.partial(_mlstm_kernel, nb=nb, L=L, NC=NC, T_valid=T_valid, H=H, DK=DK, DV=DV)
    in_specs = [pl.BlockSpec((nb, L, 2 * K), lambda i, n: (i, n, offs["qk_m"] // (2 * K))),
                pl.BlockSpec((nb, L, V), lambda i, n: (i, n, offs["v_m"] // V)),
                pl.BlockSpec((nb, L, LANES), lambda i, n: (i, n, 0)),
                pl.BlockSpec((nb, 1, GATE_ROWS, L), lambda i, n: (i, n, 0, 0)),
                pl.BlockSpec((None, nb, K, DV), lambda i, n: (layer, i, 0, 0)),
                pl.BlockSpec((None, nb, 1, K), lambda i, n: (layer, i, 0, 0)),
                pl.BlockSpec((None, nb, 1, LANES), lambda i, n: (layer, i, 0, 0)),
                _layer_spec(layer, 2, LANES),
                _layer_spec(layer, GATE_ROWS, 2),
                _layer_spec(layer, 1, V)]
    operands = [proj, proj, small, smT, C0_all, n0_all, m0_all, prow, pcol, norm]
    aliases = {}
    if C_prev is not None:
        kern = _skip_ref(kern, len(operands))
        aliases = {len(operands): 1}
        in_specs.append(pl.BlockSpec(memory_space=pl.ANY))
        operands.append(C_prev)
    outs = pl.pallas_call(
        kern,
        grid=(B // nb, NC),
        in_specs=in_specs,
        out_specs=[pl.BlockSpec((nb, L, V), lambda i, n: (i, n, 0)),
                   pl.BlockSpec((None, nb, K, DV), lambda i, n: (layer, i, 0, 0)),
                   pl.BlockSpec((nb, 1, K), lambda i, n: (i, 0, 0)),
                   pl.BlockSpec((nb, 1, LANES), lambda i, n: (i, 0, 0))],
        out_shape=[jax.ShapeDtypeStruct((B, T_pad, V), F32),
                   jax.ShapeDtypeStruct(C0_all.shape, F32),
                   jax.ShapeDtypeStruct((B, 1, K), F32),
                   jax.ShapeDtypeStruct((B, 1, LANES), F32)],
        input_output_aliases=aliases,
        compiler_params=pltpu.CompilerParams(dimension_semantics=("parallel", "arbitrary"),
                                             vmem_limit_bytes=VMEM_LIMIT),
        name="mlstm_chunk",
    )(*operands)
    hm, C_all, n_new, m_new = outs
    return hm, C_all, n_new.reshape(B, H, DK), m_new[:, 0, 3 * H:4 * H]


def _merge_kernel(og_ref, hm_ref, x_ref, g_ref, wg_ref, wbg_ref, wbm_ref, wout_ref, o_ref, *, Vg, Vm, D):
    dot = functools.partial(jnp.dot, preferred_element_type=F32)
    parts = _row_parts(x_ref.shape[0])
    x = [x_ref[p, :] for p in parts]
    xn = [(v * lax.rsqrt(jnp.mean(v * v, axis=-1, keepdims=True) + EPS) * g_ref[...]).astype(BF16) for v in x]
    z = [dot(v, wg_ref[:, 0:Vg]) for v in xn]
    og = [(og_ref[p, :] * (v * _sigmoid(v))).astype(BF16) for p, v in zip(parts, z)]
    br_g = [dot(v, wbg_ref[...]) for v in og]
    om = [dot(v, wg_ref[:, Vg:Vg + Vm]) for v in xn]
    hm = [(hm_ref[p, :] * _sigmoid(v)).astype(BF16) for p, v in zip(parts, om)]
    br_m = [dot(v, wbm_ref[...]) for v in hm]
    gg = [dot(v, wg_ref[:, Vg + Vm:Vg + Vm + D]) for v in xn]
    merged = [_sigmoid(g) * b for g, b in zip(gg, br_g)]
    gm = [dot(v, wg_ref[:, Vg + Vm + D:]) for v in xn]
    merged = [m + _sigmoid(g) * b for m, g, b in zip(merged, gm, br_m)]
    for p, v, m in zip(parts, x, merged):
        o_ref[p, :] = v + dot(m.astype(BF16), wout_ref[...])


def _merge(og, hm, x, gamma, wg, wbg, wbm, wout, layer):
    M, D = x.shape
    Vg = og.shape[1]
    Vm = hm.shape[1]
    tm = _tile(M, 512, SUBLANES * ROW_PARTS)
    return pl.pallas_call(
        functools.partial(_merge_kernel, Vg=Vg, Vm=Vm, D=D),
        grid=(M // tm,),
        in_specs=[pl.BlockSpec((tm, Vg), lambda i: (i, 0)),
                  pl.BlockSpec((tm, Vm), lambda i: (i, 0)),
                  pl.BlockSpec((tm, D), lambda i: (i, 0)),
                  _layer_spec(layer, 1, D),
                  _layer_spec(layer, D, Vg + Vm + 2 * D),
                  _layer_spec(layer, Vg, D),
                  _layer_spec(layer, Vm, D),
                  _layer_spec(layer, D, D)],
        out_specs=pl.BlockSpec((tm, D), lambda i: (i, 0)),
        out_shape=jax.ShapeDtypeStruct((M, D), F32),
        compiler_params=pltpu.CompilerParams(dimension_semantics=("parallel",), vmem_limit_bytes=VMEM_LIMIT),
        name="merge_out",
    )(og, hm, x, gamma, wg, wbg, wbm, wout)


def _mlp_kernel(x_ref, g_ref, wup_ref, wdn_ref, gf_ref, o_ref, *, final_norm, tf):
    x = x_ref[...]
    ms = jnp.mean(x * x, axis=-1, keepdims=True)
    xn = (x * lax.rsqrt(ms + EPS) * g_ref[...]).astype(BF16)
    y = x
    for c in range(wup_ref.shape[1] // tf):
        hcol = jnp.maximum(jnp.dot(xn, wup_ref[:, c * tf:(c + 1) * tf], preferred_element_type=F32), 0.0)
        y = y + jnp.dot((hcol * hcol).astype(BF16), wdn_ref[c * tf:(c + 1) * tf, :], preferred_element_type=F32)
    if final_norm:
        ms = jnp.mean(y * y, axis=-1, keepdims=True)
        y = y * lax.rsqrt(ms + EPS) * gf_ref[...]
    o_ref[...] = y


def _mlp(x, gamma, wup, wdn, gamma_final, layer, *, final_norm):
    M, D = x.shape
    FF = wup.shape[2]
    tm = _tile(M, 512, SUBLANES)
    tf = _tile(FF, 1024, LANES)
    resident = pl.Buffered(1)
    return pl.pallas_call(
        functools.partial(_mlp_kernel, final_norm=final_norm, tf=tf),
        grid=(M // tm,),
        in_specs=[pl.BlockSpec((tm, D), lambda i: (i, 0)),
                  _layer_spec(layer, 1, D),
                  pl.BlockSpec((None, D, FF), lambda i: (layer, 0, 0), pipeline_mode=resident),
                  pl.BlockSpec((None, FF, D), lambda i: (layer, 0, 0), pipeline_mode=resident),
                  pl.BlockSpec((1, D), lambda i: (0, 0))],
        out_specs=pl.BlockSpec((tm, D), lambda i: (i, 0)),
        out_shape=jax.ShapeDtypeStruct((M, D), F32),
        compiler_params=pltpu.CompilerParams(dimension_semantics=("parallel",), vmem_limit_bytes=VMEM_LIMIT),
        name="mlp",
    )(x, gamma, wup, wdn, gamma_final)


def _repack_kernel(w_ref, rec_ref, gate_ref, *, rec_moves, gate_moves, rec_fill):
    for out_ref, moves in ((rec_ref, rec_moves), (gate_ref, gate_moves)):
        for a, b, d in moves:
            out_ref[:, d:d + (b - a)] = w_ref[:, a:b].astype(BF16)
    rec_ref[:, rec_fill:] = jnp.zeros((rec_ref.shape[0], rec_ref.shape[1] - rec_fill), BF16)


def _pack_w_in(w_in, dims):
    D, Hg, DKg, DVg, Hm, DKm, DVm = dims
    Kg, Vg, Km, Vm = Hg * DKg, Hg * DVg, Hm * DKm, Hm * DVm
    sizes = (Kg, Kg, Vg, Vg, Hg, Hg, Km, Km, Vm, Vm, Hm, Hm, D, D)
    starts = [0]
    for s in sizes:
        starts.append(starts[-1] + s)
    assert 2 * Hg + 2 * Hm <= GATE_ROWS and Hg == Hm
    depth, _, IN = w_in.shape

    def layout(groups):
        moves, offs, off = [], [], 0
        for ranges in groups:
            offs.append(off)
            for a, b in ranges:
                moves.append((starts[a], starts[b], off))
                off += starts[b] - starts[a]
        return moves, offs, off

    rec_moves, rec_offs, rec_fill = layout([[(0, 3)], [(6, 8)], [(8, 9)], [(4, 6), (10, 12)]])
    gate_moves, _, n_gate = layout([[(3, 4)], [(9, 10)], [(12, 14)]])
    offs = dict(zip(("qkv", "qk_m", "v_m", "small"), rec_offs))
    n_rec = offs["small"] + LANES
    for name, width in (("qkv", 2 * Kg + Vg), ("qk_m", 2 * Km), ("v_m", Vm), ("small", LANES)):
        assert offs[name] % width == 0, (name, offs[name], width)
    tk = _tile(D, 256, 2 * SUBLANES)
    w_rec, w_gate = pl.pallas_call(
        functools.partial(_repack_kernel, rec_moves=rec_moves, gate_moves=gate_moves, rec_fill=rec_fill),
        grid=(depth, D // tk),
        in_specs=[pl.BlockSpec((None, tk, IN), lambda l, i: (l, i, 0))],
        out_specs=[pl.BlockSpec((None, tk, n_rec), lambda l, i: (l, i, 0)),
                   pl.BlockSpec((None, tk, n_gate), lambda l, i: (l, i, 0))],
        out_shape=[jax.ShapeDtypeStruct((depth, D, n_rec), BF16),
                   jax.ShapeDtypeStruct((depth, D, n_gate), BF16)],
        compiler_params=pltpu.CompilerParams(dimension_semantics=("parallel", "parallel"),
                                             vmem_limit_bytes=VMEM_LIMIT),
        name="repack_w_in",
    )(w_in)
    return w_rec, w_gate, offs


def _gate_params(first, lane_first, second, lane_second):
    depth, H = first.shape
    place = lambda v, lane: jnp.pad(v.astype(F32), ((0, 0), (lane, LANES - lane - H)))
    row = jnp.stack([place(first, lane_first), place(second, lane_second)], axis=1)
    return row, jnp.swapaxes(row[:, :, :GATE_ROWS], 1, 2)


def _head_norm_rows(w, H, DV):
    depth = w.shape[0]
    return jnp.broadcast_to(w.astype(F32).reshape(depth, -1, DV), (depth, H, DV)).reshape(depth, 1, H * DV)


def _trunk(x, conv0, S0, C0, n0, m0, P, *, B, T_valid, L):
    M, D = x.shape
    NC = -(-T_valid // L)
    T_pad = NC * L
    pad_t = lambda a: a if T_pad == T_valid else jnp.pad(a, ((0, 0), (0, T_pad - T_valid), (0, 0)))
    depth = P["w_in"].shape[0]
    offs = P["offs"]
    _, _, H, DK, DV = C0.shape
    C0_all = C0.reshape(depth, B, H * DK, DV)
    n0_all = n0.reshape(depth, B, 1, H * DK)
    m0_all = jnp.pad(m0.reshape(depth, B, 1, H), ((0, 0), (0, 0), (0, 0), (3 * H, LANES - 4 * H)))
    new = ([], [], [])
    S_all, C_all = None, None
    for l in range(depth):
        proj, small = _norm_proj(x, P["norm_mix"], P["w_in"], l)
        proj3 = pad_t(proj.reshape(B, T_valid, proj.shape[1]))
        small3 = pad_t(small.reshape(B, T_valid, LANES))
        smT = jnp.swapaxes(small3[:, :, :GATE_ROWS].reshape(B, NC, L, GATE_ROWS), 2, 3)
        og, conv_n, S_all = _gdn(proj3, small3, smT, conv0, S0, l, S_all, P["conv_w"], P["gdn_prow"],
                                 P["gdn_pcol"], P["gdn_norm"], NC=NC, L=L, T_valid=T_valid, offs=offs)
        hm, C_all, n_n, m_n = _mlstm(proj3, small3, smT, C0_all, l, C_all, n0_all, m0_all, P["ml_prow"],
                                     P["ml_pcol"], P["ml_norm"], NC=NC, L=L, T_valid=T_valid, offs=offs)
        x = _merge(og[:, :T_valid].reshape(M, -1), hm[:, :T_valid].reshape(M, -1), x, P["norm_mix"], P["w_gate"],
                   P["w_bg"], P["w_bm"], P["w_out"], l)
        x = _mlp(x, P["norm_mlp"], P["w_up"], P["w_down"], P["norm_final"], l, final_norm=(l == depth - 1))
        for lst, s in zip(new, (conv_n, n_n, m_n)):
            lst.append(s)
    conv_new, n_new, m_new = (jnp.stack(lst) for lst in new)
    return x, conv_new, S_all, C_all.reshape(C0.shape), n_new, m_new


def kernel(x_prompt, x_sample, state_gdn_conv, state_gdn_S, state_mlstm_C, state_mlstm_n, state_mlstm_m, norm_mix, w_in, gdn_conv_w, gdn_A_log, gdn_dt_bias, gdn_norm, ml_i_bias, ml_f_bias, ml_norm, w_branch_gdn, w_branch_ml, w_out, norm_mlp, w_up, w_down, norm_final):
    Bp, Tp, D = x_prompt.shape
    Bs, Ts, _ = x_sample.shape
    depth = w_in.shape[0]
    _, _, Hg, DKg, DVg = state_gdn_S.shape
    _, _, Hm, DKm, DVm = state_mlstm_C.shape
    CG = state_gdn_conv.shape[-1]
    dims = (D, Hg, DKg, DVg, Hm, DKm, DVm)
    assert Ts >= CONV_W - 1 and Tp >= CONV_W - 1

    w_rec, w_gate, offs = _pack_w_in(w_in, dims)
    P = {
        "offs": offs,
        "w_in": w_rec,
        "w_gate": w_gate,
        "norm_mix": norm_mix.reshape(depth, 1, D),
        "norm_mlp": norm_mlp.reshape(depth, 1, D),
        "norm_final": norm_final.reshape(1, D),
        "conv_w": gdn_conv_w,
        "w_bg": w_branch_gdn.astype(BF16),
        "w_bm": w_branch_ml.astype(BF16),
        "w_out": w_out.astype(BF16),
        "w_up": w_up.astype(BF16),
        "w_down": w_down.astype(BF16),
        "gdn_norm": _head_norm_rows(gdn_norm, Hg, DVg),
        "ml_norm": _head_norm_rows(ml_norm, Hm, DVm),
    }
    P["gdn_prow"], P["gdn_pcol"] = _gate_params(gdn_A_log, Hg, gdn_dt_bias, Hg)
    P["ml_prow"], P["ml_pcol"] = _gate_params(ml_i_bias, 2 * Hg, ml_f_bias, 2 * Hg + Hm)

    Lp = CHUNK if Tp % CHUNK == 0 else Tp
    assert Lp % SUBLANES == 0
    zeros = lambda *s: jnp.zeros(s, F32)
    yp, conv_p, S_p, C_p, n_p, m_p = _trunk(
        x_prompt.reshape(Bp * Tp, D), zeros(depth, Bp, CONV_W - 1, CG), zeros(depth, Bp, Hg, DKg, DVg),
        zeros(depth, Bp, Hm, DKm, DVm), zeros(depth, Bp, Hm, DKm), zeros(depth, Bp, Hm), P,
        B=Bp, T_valid=Tp, L=Lp)

    Ls = -(-Ts // SUBLANES) * SUBLANES
    ys, conv_s, S_s, C_s, n_s, m_s = _trunk(
        x_sample.reshape(Bs * Ts, D), state_gdn_conv, state_gdn_S, state_mlstm_C, state_mlstm_n, state_mlstm_m, P,
        B=Bs, T_valid=Ts, L=Ls)
    return (yp.reshape(Bp, Tp, D), ys.reshape(Bs, Ts, D), conv_p, S_p, C_p, n_p, m_p, conv_s, S_s, C_s, n_s, m_s)
```

```python
import functools

import jax
import jax.numpy as jnp
from jax import lax
from jax.experimental import pallas as pl
from jax.experimental.pallas import tpu as pltpu

F32 = jnp.float32
BF16 = jnp.bfloat16
EPS = 1e-6
CONV_W = 4
LANES = 128
SUBLANES = 8
GATE_ROWS = 16
CHUNK = 64
CHAINS = 32
ROW_PARTS = 2
NEG_BIG = -1e30
VMEM_LIMIT = 48 * 1024 * 1024
HIGHEST = lax.Precision.HIGHEST


def _sigmoid(x):
    return 1.0 / (1.0 + jnp.exp(-x))


def _softplus(x):
    return jnp.maximum(x, 0.0) + jnp.log(1.0 + jnp.exp(-jnp.abs(x)))


def _mm(a, b):
    return jnp.dot(a.astype(BF16), b.astype(BF16), preferred_element_type=F32)


def _mm_nt(a, b):
    return lax.dot_general(a.astype(BF16), b.astype(BF16), (((1,), (1,)), ((), ())), preferred_element_type=F32)


def _mm_tn(a, b):
    return lax.dot_general(a.astype(BF16), b.astype(BF16), (((0,), (0,)), ((), ())), preferred_element_type=F32)


def _mm_tn_list(As, Bs):
    K, M = As[0].shape
    if K < 64:
        return [_mm_tn(a, b) for a, b in zip(As, Bs)]
    eye = jnp.where(lax.broadcasted_iota(jnp.int32, (M, M), 0) == lax.broadcasted_iota(jnp.int32, (M, M), 1),
                    1.0, 0.0).astype(BF16)
    a_t = [_mm_nt(eye, a) for a in As]
    return [_mm(a, b) for a, b in zip(a_t, Bs)]


def _split_bf16(a):
    hi = a.astype(BF16)
    return hi, (a - hi.astype(F32)).astype(BF16)


def _tile(n, cap, mult):
    best = None
    for t in range(mult, min(n, cap) + 1, mult):
        if n % t == 0:
            best = t
    return best if best is not None else n


def _tri_masks(L):
    row = lax.broadcasted_iota(jnp.int32, (L, L), 0)
    col = lax.broadcasted_iota(jnp.int32, (L, L), 1)
    return row >= col, row > col


def _cumsum_both(g_col, g_row, incl):
    L = incl.shape[0]
    row = lax.broadcasted_iota(jnp.int32, (L, L), 0)
    col = lax.broadcasted_iota(jnp.int32, (L, L), 1)
    lower = jnp.where(incl, 1.0, 0.0)
    upper = jnp.where(row <= col, 1.0, 0.0)
    c_col = jnp.dot(lower, g_col, precision=HIGHEST, preferred_element_type=F32)
    c_row = jnp.dot(g_row, upper, precision=HIGHEST, preferred_element_type=F32)
    return c_col, c_row


def _tri_inv_unit_lower(As):
    L = As[0].shape[0]
    row = lax.broadcasted_iota(jnp.int32, (L, L), 0)
    col = lax.broadcasted_iota(jnp.int32, (L, L), 1)
    eye = jnp.where(row == col, 1.0, 0.0)
    levels = max(1, (L - 1).bit_length())
    Xs = [eye - A for A in As]
    if levels > 1:
        Ps = [_mm(A, A) for A in As]
        for _ in range(1, levels - 1):
            Rs = [_mm(jnp.concatenate([P, X], axis=0), P) for P, X in zip(Ps, Xs)]
            Ps = [R[:L] for R in Rs]
            Xs = [X + R[L:] for X, R in zip(Xs, Rs)]
        Xs = [X + _mm(X, P) for X, P in zip(Xs, Ps)]
    splits = [(_split_bf16(A), _split_bf16(X)) for A, X in zip(As, Xs)]
    AX1 = [jnp.dot(jnp.concatenate([a_hi, a_lo], axis=0), x_hi, preferred_element_type=F32)
           for (a_hi, a_lo), (x_hi, _) in splits]
    AX2 = [jnp.dot(a_hi, x_lo, preferred_element_type=F32) for (a_hi, _), (_, x_lo) in splits]
    Rs = [(eye - X) - (r1[:L] + (r1[L:] + r2)) for X, r1, r2 in zip(Xs, AX1, AX2)]
    return Xs, Rs


def _layer_spec(layer, *shape):
    return pl.BlockSpec((None,) + shape, lambda *_: (layer,) + (0,) * len(shape))


def _row_parts(tm):
    return [pl.ds(r * (tm // ROW_PARTS), tm // ROW_PARTS) for r in range(ROW_PARTS)]


def _norm_proj_kernel(x_ref, g_ref, w_ref, o_ref, s_ref):
    parts = _row_parts(x_ref.shape[0])
    x = [x_ref[p, :] for p in parts]
    xn = [(v * lax.rsqrt(jnp.mean(v * v, axis=-1, keepdims=True) + EPS) * g_ref[...]).astype(BF16) for v in x]
    res = [jnp.dot(v, w_ref[...], preferred_element_type=F32) for v in xn]
    for p, r in zip(parts, res):
        o_ref[p, :] = r[:, :o_ref.shape[1]]
        s_ref[p, :] = r[:, o_ref.shape[1]:]


def _norm_proj(x, gamma_all, w_all, layer):
    M, D = x.shape
    N = w_all.shape[2]
    tm = _tile(M, 512, SUBLANES * ROW_PARTS)
    return pl.pallas_call(
        _norm_proj_kernel,
        grid=(M // tm,),
        in_specs=[pl.BlockSpec((tm, D), lambda i: (i, 0)),
                  _layer_spec(layer, 1, D),
                  _layer_spec(layer, D, N)],
        out_specs=[pl.BlockSpec((tm, N - LANES), lambda i: (i, 0)),
                   pl.BlockSpec((tm, LANES), lambda i: (i, 0))],
        out_shape=[jax.ShapeDtypeStruct((M, N - LANES), F32),
                   jax.ShapeDtypeStruct((M, LANES), F32)],
        compiler_params=pltpu.CompilerParams(dimension_semantics=("parallel",), vmem_limit_bytes=VMEM_LIMIT),
        name="norm_proj",
    )(x, gamma_all, w_all)


def _gdn_prep_seq(b, qkv_ref, sm_ref, smT_ref, xp_ref, convn_ref, cw_ref, prow_ref, pcol_ref,
                  *, L, NC, T_valid, H, DK, DV):
    K = H * DK
    pad0 = SUBLANES - (CONV_W - 1)
    incl, _ = _tri_masks(L)
    lv = L if NC > 1 else T_valid
    xp_ref[b, SUBLANES:SUBLANES + L, :] = qkv_ref[b]
    xfull = xp_ref[b]
    y = xfull * cw_ref[CONV_W - 1:CONV_W, :]
    for j in range(CONV_W - 1):
        y = y + pltpu.roll(xfull, CONV_W - 1 - j, axis=0) * cw_ref[j:j + 1, :]
    y = y[SUBLANES:, :]
    qkv = y * _sigmoid(y)
    tail = xp_ref[b, SUBLANES + lv - (CONV_W - 1):SUBLANES + lv, :]
    xp_ref[b, pad0:SUBLANES, :] = tail
    convn_ref[b] = tail

    sm = sm_ref[b]
    smT = smT_ref[b, 0]
    beta_c = _sigmoid(sm)
    g_c = -jnp.exp(prow_ref[0:1, :]) * _softplus(sm + prow_ref[1:2, :])
    g_r = -jnp.exp(pcol_ref[:, 0:1]) * _softplus(smT + pcol_ref[:, 1:2])
    if T_valid < NC * L:
        vc = lax.broadcasted_iota(jnp.int32, (L, 1), 0) < T_valid
        vr = lax.broadcasted_iota(jnp.int32, (1, L), 1) < T_valid
        beta_c = jnp.where(vc, beta_c, 0.0)
        g_c = jnp.where(vc, g_c, 0.0)
        g_r = jnp.where(vr, g_r, 0.0)
    gc_c, gc_r = _cumsum_both(g_c, g_r, incl)
    chains = []
    for h in range(H):
        qh = qkv[:, h * DK:(h + 1) * DK]
        kh = qkv[:, K + h * DK:K + (h + 1) * DK]
        vh = qkv[:, 2 * K + h * DV:2 * K + (h + 1) * DV]
        qh = qh * lax.rsqrt(jnp.sum(qh * qh, axis=-1, keepdims=True) + EPS) * (DK ** -0.5)
        kh = kh * lax.rsqrt(jnp.sum(kh * kh, axis=-1, keepdims=True) + EPS)
        b_c = beta_c[:, h:h + 1]
        gcc = gc_c[:, H + h:H + h + 1]
        gcr = gc_r[H + h:H + h + 1, :]
        gl = gcc[L - 1:L, :]
        eg = jnp.exp(gcc)
        kb = kh * b_c
        chains.append(dict(
            kh=kh.astype(BF16), kbq=jnp.concatenate([kb, qh], axis=0).astype(BF16),
            decay=jnp.exp(jnp.where(incl, gcc - gcr, -jnp.inf)),
            rhs=jnp.concatenate([kb * eg, vh * b_c], axis=1), qd=(qh * eg).astype(BF16),
            kd=(kh * jnp.exp(gl - gcc)).astype(BF16), dl=jnp.exp(gl)))
    return chains


def _gdn_chain_phase(chains, S_ref, o_ref, gnorm_ref, *, L, DK, DV):
    _, strict = _tri_masks(L)
    ops = [c[2] for c in chains]
    r1 = [_mm_nt(c["kbq"], c["kh"]) for c in ops]
    A = [jnp.where(strict, r[:L] * c["decay"], 0.0) for r, c in zip(r1, ops)]
    qk = [(r[L:] * c["decay"]).astype(BF16) for r, c in zip(r1, ops)]
    Xs, Rs = _tri_inv_unit_lower(A)
    corr = [_mm(R, c["rhs"]) for R, c in zip(Rs, ops)]
    wu = [_mm(X, c["rhs"] + cr) for X, c, cr in zip(Xs, ops, corr)]
    states = [S_ref[b, h] for b, h, _ in chains]
    wq = [_mm(jnp.concatenate([x[:, :DK].astype(BF16), c["qd"]], axis=0), S) for x, c, S in zip(wu, ops, states)]
    v_new = [(x[:, DK:] - y[:L]).astype(BF16) for x, y in zip(wu, wq)]
    o_intra = [_mm(q, v) for q, v in zip(qk, v_new)]
    s_upd = [_mm_tn(c["kd"], v) for c, v in zip(ops, v_new)]
    for (b, h, c), S, su in zip(chains, states, s_upd):
        S_ref[b, h] = S * c["dl"] + su
    o = [y[L:] + oi for y, oi in zip(wq, o_intra)]
    rs = [lax.rsqrt(jnp.mean(x * x, axis=-1, keepdims=True) + EPS) for x in o]
    for x, r, (b, h, _) in zip(o, rs, chains):
        o_ref[b, :, h * DV:(h + 1) * DV] = x * r * gnorm_ref[:, h * DV:(h + 1) * DV]


def _gdn_kernel(qkv_ref, sm_ref, smT_ref, conv0_ref, S0_ref, cw_ref, prow_ref, pcol_ref, gnorm_ref,
                o_ref, convn_ref, S_ref, xp_ref, *, nb, L, NC, T_valid, H, DK, DV):
    pad0 = SUBLANES - (CONV_W - 1)

    @pl.when(pl.program_id(1) == 0)
    def _():
        xp_ref[:, pad0:SUBLANES, :] = conv0_ref[...]
        S_ref[...] = S0_ref[...]

    chains = []
    for b in range(nb):
        ops = _gdn_prep_seq(b, qkv_ref, sm_ref, smT_ref, xp_ref, convn_ref, cw_ref, prow_ref, pcol_ref,
                            L=L, NC=NC, T_valid=T_valid, H=H, DK=DK, DV=DV)
        chains += [(b, h, c) for h, c in enumerate(ops)]
    _gdn_chain_phase(chains, S_ref, o_ref, gnorm_ref, L=L, DK=DK, DV=DV)


def _skip_ref(kernel_fn, index):
    def wrapped(*refs, **kw):
        kernel_fn(*refs[:index], *refs[index + 1:], **kw)
    return wrapped


def _gdn(proj, small, smT, conv0_all, S0_all, layer, S_prev, cw, prow, pcol, gnorm, *, NC, L, T_valid, offs):
    depth, B, H, DK, DV = S0_all.shape
    CG = conv0_all.shape[-1]
    V = H * DV
    T_pad = proj.shape[1]
    nb = _tile(B, max(1, CHAINS // H), 1)
    kern = functools.partial(_gdn_kernel, nb=nb, L=L, NC=NC, T_valid=T_valid, H=H, DK=DK, DV=DV)
    in_specs = [pl.BlockSpec((nb, L, CG), lambda i, n: (i, n, offs["qkv"] // CG)),
                pl.BlockSpec((nb, L, LANES), lambda i, n: (i, n, 0)),
                pl.BlockSpec((nb, 1, GATE_ROWS, L), lambda i, n: (i, n, 0, 0)),
                pl.BlockSpec((None, nb, CONV_W - 1, CG), lambda i, n: (layer, i, 0, 0)),
                pl.BlockSpec((None, nb, H, DK, DV), lambda i, n: (layer, i, 0, 0, 0)),
                _layer_spec(layer, CONV_W, CG),
                _layer_spec(layer, 2, LANES),
                _layer_spec(layer, GATE_ROWS, 2),
                _layer_spec(layer, 1, V)]
    operands = [proj, small, smT, conv0_all, S0_all, cw, prow, pcol, gnorm]
    aliases = {}
    if S_prev is not None:
        kern = _skip_ref(kern, len(operands))
        aliases = {len(operands): 2}
        in_specs.append(pl.BlockSpec(memory_space=pl.ANY))
        operands.append(S_prev)
    return pl.pallas_call(
        kern,
        grid=(B // nb, NC),
        in_specs=in_specs,
        out_specs=[pl.BlockSpec((nb, L, V), lambda i, n: (i, n, 0)),
                   pl.BlockSpec((nb, CONV_W - 1, CG), lambda i, n: (i, 0, 0)),
                   pl.BlockSpec((None, nb, H, DK, DV), lambda i, n: (layer, i, 0, 0, 0))],
        out_shape=[jax.ShapeDtypeStruct((B, T_pad, V), F32),
                   jax.ShapeDtypeStruct(conv0_all.shape[1:], F32),
                   jax.ShapeDtypeStruct(S0_all.shape, F32)],
        scratch_shapes=[pltpu.VMEM((nb, SUBLANES + L, CG), F32)],
        input_output_aliases=aliases,
        compiler_params=pltpu.CompilerParams(dimension_semantics=("parallel", "arbitrary"),
                                             vmem_limit_bytes=VMEM_LIMIT),
        name="gdn_chunk",
    )(*operands)


def _mlstm_kernel(qk_ref, v_ref, sm_ref, smT_ref, C0_ref, n0_ref, m0_ref, prow_ref, pcol_ref, norm_ref,
                  h_ref, C_ref, n_ref, m_ref, *, nb, L, NC, T_valid, H, DK, DV):
    step = pl.program_id(1)
    K = H * DK
    g0 = 3 * H

    @pl.when(step == 0)
    def _():
        C_ref[...] = C0_ref[...]
        n_ref[...] = n0_ref[...]
        m_ref[...] = m0_ref[...]

    incl, _ = _tri_masks(L)
    lane = lax.broadcasted_iota(jnp.int32, (1, K), 1)
    lane1 = lax.broadcasted_iota(jnp.int32, (1, LANES), 1)
    rows = lax.broadcasted_iota(jnp.int32, (L, 1), 0)
    seqs = range(nb)

    sm = [sm_ref[b] for b in seqs]
    smT = [smT_ref[b, 0] for b in seqs]
    li_c = [x + prow_ref[0:1, :] for x in sm]
    li_r = [x + pcol_ref[:, 0:1] for x in smT]
    lf_c = [-_softplus(-(x + prow_ref[1:2, :])) for x in sm]
    lf_r = [-_softplus(-(x + pcol_ref[:, 1:2])) for x in smT]
    if T_valid < NC * L:
        vc = rows < T_valid
        vr = lax.broadcasted_iota(jnp.int32, (1, L), 1) < T_valid
        li_c = [jnp.where(vc, x, NEG_BIG) for x in li_c]
        li_r = [jnp.where(vr, x, NEG_BIG) for x in li_r]
        lf_c = [jnp.where(vc, x, 0.0) for x in lf_c]
        lf_r = [jnp.where(vr, x, 0.0) for x in lf_r]
    FF = [_cumsum_both(c, r, incl) for c, r in zip(lf_c, lf_r)]
    F_c = [f[0] for f in FF]
    F_r = [f[1] for f in FF]
    r_c = [pltpu.roll(x, H, axis=1) - f for x, f in zip(li_c, F_c)]
    cm = r_c
    s = 1
    while s < L:
        cm = [jnp.maximum(x, jnp.where(rows >= s, pltpu.roll(x, s, axis=0), -jnp.inf)) for x in cm]
        s *= 2
    m_old = [m_ref[b] for b in seqs]
    mx = [jnp.maximum(x, m) for x, m in zip(cm, m_old)]
    mx_last = [x[L - 1:L, :] for x in mx]
    dec = [jnp.exp(m - x) for m, x in zip(m_old, mx_last)]
    wC = [jnp.exp(r - x) for r, x in zip(r_c, mx_last)]
    a_all = [jnp.exp(m - x) for m, x in zip(m_old, mx)]
    floor = [jnp.exp(-f - x) for f, x in zip(F_c, mx)]
    for b in seqs:
        m_ref[b] = jnp.where((lane1 >= g0) & (lane1 < g0 + H), F_c[b][L - 1:L, :] + mx_last[b], 0.0)

    q_all = [qk_ref[b, :, :K] * (DK ** -0.5) for b in seqs]
    k_all = [qk_ref[b, :, K:] for b in seqs]
    k_bf = [x.astype(BF16) for x in k_all]
    C_all = [C_ref[b] for b in seqs]
    C_bf = [x.astype(BF16) for x in C_all]
    n_row = [n_ref[b] for b in seqs]

    half = lane1 < DK
    kw_cols, dec_lanes = [], []
    for b in seqs:
        cols = []
        for c in range(K // LANES):
            w_lo = wC[b][:, g0 + 2 * c:g0 + 2 * c + 1]
            w_hi = wC[b][:, g0 + 2 * c + 1:g0 + 2 * c + 2]
            cols.append(k_all[b][:, c * LANES:(c + 1) * LANES] * jnp.where(half, w_lo, w_hi))
        kw_cols.append(cols)
        d = dec[b][:, g0:g0 + 1]
        for h in range(1, H):
            d = jnp.where(lane >= h * DK, dec[b][:, g0 + h:g0 + h + 1], d)
        dec_lanes.append(d)
    for b in seqs:
        n_upd = jnp.concatenate([jnp.sum(x, axis=0, keepdims=True) for x in kw_cols[b]], axis=1)
        n_ref[b] = dec_lanes[b] * n_row[b] + n_upd

    chains = [(b, h) for b in seqs for h in range(H)]
    in_head = [(lane >= h * DK) & (lane < (h + 1) * DK) for h in range(H)]
    q_h = [jnp.where(in_head[h], q_all[b], 0.0) for b, h in chains]
    q_bf = [x.astype(BF16) for x in q_h]
    v_bf = [v_ref[b, :, h * DV:(h + 1) * DV].astype(BF16) for b, h in chains]
    kw = [kw_cols[b][h // 2][:, (h % 2) * DK:(h % 2 + 1) * DK].astype(BF16) for b, h in chains]
    expD = [jnp.exp(jnp.where(incl, (li_r[b][2 * H + h:2 * H + h + 1, :] - F_r[b][g0 + h:g0 + h + 1, :])
                              - mx[b][:, g0 + h:g0 + h + 1], -jnp.inf)) for b, h in chains]
    qn = [jnp.sum(x * n_row[b], axis=1, keepdims=True) for x, (b, h) in zip(q_h, chains)]

    qk = [_mm_nt(x, k_bf[b]) for x, (b, h) in zip(q_bf, chains)]
    Sm = [e * x for e, x in zip(expD, qk)]
    inter_state = [_mm(x, C_bf[b]) for x, (b, h) in zip(q_bf, chains)]
    intra = [_mm(x, v) for x, v in zip(Sm, v_bf)]
    upd = _mm_tn_list(kw, v_bf)
    rowsum = [jnp.sum(x, axis=1, keepdims=True) for x in Sm]
    for i, (b, h) in enumerate(chains):
        C_ref[b, h * DK:(h + 1) * DK, :] = dec[b][:, g0 + h:g0 + h + 1] * C_all[b][h * DK:(h + 1) * DK, :] + upd[i]
    a_h = [a_all[b][:, g0 + h:g0 + h + 1] for b, h in chains]
    den = [a * x + y for a, x, y in zip(a_h, qn, rowsum)]
    scale = [1.0 / jnp.maximum(jnp.abs(d), floor[b][:, g0 + h:g0 + h + 1]) for d, (b, h) in zip(den, chains)]
    hh = [(a * x + y) * sc for a, x, y, sc in zip(a_h, inter_state, intra, scale)]
    ms = [jnp.mean(x * x, axis=-1, keepdims=True) for x in hh]
    rs = [lax.rsqrt(x + EPS) for x in ms]
    for x, r, (b, h) in zip(hh, rs, chains):
        h_ref[b, :, h * DV:(h + 1) * DV] = x * r * norm_ref[:, h * DV:(h + 1) * DV]


def _mlstm(proj, small, smT, C0_all, layer, C_prev, n0_all, m0_all, prow, pcol, norm, *, NC, L, T_valid, offs):
    depth, B, K, DV = C0_all.shape
    V = norm.shape[-1]
    H = V // DV
    DK = K // H
    T_pad = proj.shape[1]
    nb = _tile(B, max(1, CHAINS // H), 1)
    assert 2 * DK == LANES and 4 * H <= LANES
    kern = functools.partial(_mlstm_kernel, nb=nb, L=L, NC=NC, T_valid=T_valid, H=H, DK=DK, DV=DV)
    in_specs = [pl.BlockSpec((nb, L, 2 * K), lambda i, n: (i, n, offs["qk_m"] // (2 * K))),
                pl.BlockSpec((nb, L, V), lambda i, n: (i, n, offs["v_m"] // V)),
                pl.BlockSpec((nb, L, LANES), lambda i, n: (i, n, 0)),
                pl.BlockSpec((nb, 1, GATE_ROWS, L), lambda i, n: (i, n, 0, 0)),
                pl.BlockSpec((None, nb, K, DV), lambda i, n: (layer, i, 0, 0)),
                pl.BlockSpec((None, nb, 1, K), lambda i, n: (layer, i, 0, 0)),
                pl.BlockSpec((None, nb, 1, LANES), lambda i, n: (layer, i, 0, 0)),
                _layer_spec(layer, 2, LANES),
                _layer_spec(layer, GATE_ROWS, 2),
                _layer_spec(layer, 1, V)]
    operands = [proj, proj, small, smT, C0_all, n0_all, m0_all, prow, pcol, norm]
    aliases = {}
    if C_prev is not None:
        kern = _skip_ref(kern, len(operands))
        aliases = {len(operands): 1}
        in_specs.append(pl.BlockSpec(memory_space=pl.ANY))
        operands.append(C_prev)
    outs = pl.pallas_call(
        kern,
        grid=(B // nb, NC),
        in_specs=in_specs,
        out_specs=[pl.BlockSpec((nb, L, V), lambda i, n: (i, n, 0)),
                   pl.BlockSpec((None, nb, K, DV), lambda i, n: (layer, i, 0, 0)),
                   pl.BlockSpec((nb, 1, K), lambda i, n: (i, 0, 0)),
                   pl.BlockSpec((nb, 1, LANES), lambda i, n: (i, 0, 0))],
        out_shape=[jax.ShapeDtypeStruct((B, T_pad, V), F32),
                   jax.ShapeDtypeStruct(C0_all.shape, F32),
                   jax.ShapeDtypeStruct((B, 1, K), F32),
                   jax.ShapeDtypeStruct((B, 1, LANES), F32)],
        input_output_aliases=aliases,
        compiler_params=pltpu.CompilerParams(dimension_semantics=("parallel", "arbitrary"),
                                             vmem_limit_bytes=VMEM_LIMIT),
        name="mlstm_chunk",
    )(*operands)
    hm, C_all, n_new, m_new = outs
    return hm, C_all, n_new.reshape(B, H, DK), m_new[:, 0, 3 * H:4 * H]


def _merge_kernel(og_ref, hm_ref, x_ref, g_ref, wg_ref, wbg_ref, wbm_ref, wout_ref, o_ref, *, Vg, Vm, D):
    dot = functools.partial(jnp.dot, preferred_element_type=F32)
    parts = _row_parts(x_ref.shape[0])
    x = [x_ref[p, :] for p in parts]
    xn = [(v * lax.rsqrt(jnp.mean(v * v, axis=-1, keepdims=True) + EPS) * g_ref[...]).astype(BF16) for v in x]
    z = [dot(v, wg_ref[:, 0:Vg]) for v in xn]
    og = [(og_ref[p, :] * (v * _sigmoid(v))).astype(BF16) for p, v in zip(parts, z)]
    br_g = [dot(v, wbg_ref[...]) for v in og]
    om = [dot(v, wg_ref[:, Vg:Vg + Vm]) for v in xn]
    hm = [(hm_ref[p, :] * _sigmoid(v)).astype(BF16) for p, v in zip(parts, om)]
    br_m = [dot(v, wbm_ref[...]) for v in hm]
    gg = [dot(v, wg_ref[:, Vg + Vm:Vg + Vm + D]) for v in xn]
    merged = [_sigmoid(g) * b for g, b in zip(gg, br_g)]
    gm = [dot(v, wg_ref[:, Vg + Vm + D:]) for v in xn]
    merged = [m + _sigmoid(g) * b for m, g, b in zip(merged, gm, br_m)]
    for p, v, m in zip(parts, x, merged):
        o_ref[p, :] = v + dot(m.astype(BF16), wout_ref[...])


def _merge(og, hm, x, gamma, wg, wbg, wbm, wout, layer):
    M, D = x.shape
    Vg = og.shape[1]
    Vm = hm.shape[1]
    tm = _tile(M, 512, SUBLANES * ROW_PARTS)
    return pl.pallas_call(
        functools.partial(_merge_kernel, Vg=Vg, Vm=Vm, D=D),
        grid=(M // tm,),
        in_specs=[pl.BlockSpec((tm, Vg), lambda i: (i, 0)),
                  pl.BlockSpec((tm, Vm), lambda i: (i, 0)),
                  pl.BlockSpec((tm, D), lambda i: (i, 0)),
                  _layer_spec(layer, 1, D),
                  _layer_spec(layer, D, Vg + Vm + 2 * D),
                  _layer_spec(layer, Vg, D),
                  _layer_spec(layer, Vm, D),
                  _layer_spec(layer, D, D)],
        out_specs=pl.BlockSpec((tm, D), lambda i: (i, 0)),
        out_shape=jax.ShapeDtypeStruct((M, D), F32),
        compiler_params=pltpu.CompilerParams(dimension_semantics=("parallel",), vmem_limit_bytes=VMEM_LIMIT),
        name="merge_out",
    )(og, hm, x, gamma, wg, wbg, wbm, wout)


def _mlp_kernel(x_ref, g_ref, wup_ref, wdn_ref, gf_ref, o_ref, *, final_norm, tf):
    x = x_ref[...]
    ms = jnp.mean(x * x, axis=-1, keepdims=True)
    xn = (x * lax.rsqrt(ms + EPS) * g_ref[...]).astype(BF16)
    y = x
    for c in range(wup_ref.shape[1] // tf):
        hcol = jnp.maximum(jnp.dot(xn, wup_ref[:, c * tf:(c + 1) * tf], preferred_element_type=F32), 0.0)
        y = y + jnp.dot((hcol * hcol).astype(BF16), wdn_ref[c * tf:(c + 1) * tf, :], preferred_element_type=F32)
    if final_norm:
        ms = jnp.mean(y * y, axis=-1, keepdims=True)
        y = y * lax.rsqrt(ms + EPS) * gf_ref[...]
    o_ref[...] = y


def _mlp(x, gamma, wup, wdn, gamma_final, layer, *, final_norm):
    M, D = x.shape
    FF = wup.shape[2]
    tm = _tile(M, 512, SUBLANES)
    tf = _tile(FF, 1024, LANES)
    resident = pl.Buffered(1)
    return pl.pallas_call(
        functools.partial(_mlp_kernel, final_norm=final_norm, tf=tf),
        grid=(M // tm,),
        in_specs=[pl.BlockSpec((tm, D), lambda i: (i, 0)),
                  _layer_spec(layer, 1, D),
                  pl.BlockSpec((None, D, FF), lambda i: (layer, 0, 0), pipeline_mode=resident),
                  pl.BlockSpec((None, FF, D), lambda i: (layer, 0, 0), pipeline_mode=resident),
                  pl.BlockSpec((1, D), lambda i: (0, 0))],
        out_specs=pl.BlockSpec((tm, D), lambda i: (i, 0)),
        out_shape=jax.ShapeDtypeStruct((M, D), F32),
        compiler_params=pltpu.CompilerParams(dimension_semantics=("parallel",), vmem_limit_bytes=VMEM_LIMIT),
        name="mlp",
    )(x, gamma, wup, wdn, gamma_final)


def _repack_kernel(w_ref, rec_ref, gate_ref, *, rec_moves, gate_moves, rec_fill):
    for out_ref, moves in ((rec_ref, rec_moves), (gate_ref, gate_moves)):
        for a, b, d in moves:
            out_ref[:, d:d + (b - a)] = w_ref[:, a:b].astype(BF16)
    rec_ref[:, rec_fill:] = jnp.zeros((rec_ref.shape[0], rec_ref.shape[1] - rec_fill), BF16)


def _pack_w_in(w_in, dims):
    D, Hg, DKg, DVg, Hm, DKm, DVm = dims
    Kg, Vg, Km, Vm = Hg * DKg, Hg * DVg, Hm * DKm, Hm * DVm
    sizes = (Kg, Kg, Vg, Vg, Hg, Hg, Km, Km, Vm, Vm, Hm, Hm, D, D)
    starts = [0]
    for s in sizes:
        starts.append(starts[-1] + s)
    assert 2 * Hg + 2 * Hm <= GATE_ROWS and Hg == Hm
    depth, _, IN = w_in.shape

    def layout(groups):
        moves, offs, off = [], [], 0
        for ranges in groups:
            offs.append(off)
            for a, b in ranges:
                moves.append((starts[a], starts[b], off))
                off += starts[b] - starts[a]
        return moves, offs, off

    rec_moves, rec_offs, rec_fill = layout([[(0, 3)], [(6, 8)], [(8, 9)], [(4, 6), (10, 12)]])
    gate_moves, _, n_gate = layout([[(3, 4)], [(9, 10)], [(12, 14)]])
    offs = dict(zip(("qkv", "qk_m", "v_m", "small"), rec_offs))
    n_rec = offs["small"] + LANES
    for name, width in (("qkv", 2 * Kg + Vg), ("qk_m", 2 * Km), ("v_m", Vm), ("small", LANES)):
        assert offs[name] % width == 0, (name, offs[name], width)
    tk = _tile(D, 256, 2 * SUBLANES)
    w_rec, w_gate = pl.pallas_call(
        functools.partial(_repack_kernel, rec_moves=rec_moves, gate_moves=gate_moves, rec_fill=rec_fill),
        grid=(depth, D // tk),
        in_specs=[pl.BlockSpec((None, tk, IN), lambda l, i: (l, i, 0))],
        out_specs=[pl.BlockSpec((None, tk, n_rec), lambda l, i: (l, i, 0)),
                   pl.BlockSpec((None, tk, n_gate), lambda l, i: (l, i, 0))],
        out_shape=[jax.ShapeDtypeStruct((depth, D, n_rec), BF16),
                   jax.ShapeDtypeStruct((depth, D, n_gate), BF16)],
        compiler_params=pltpu.CompilerParams(dimension_semantics=("parallel", "parallel"),
                                             vmem_limit_bytes=VMEM_LIMIT),
        name="repack_w_in",
    )(w_in.astype(BF16))
    return w_rec, w_gate, offs


def _gate_params(first, lane_first, second, lane_second):
    depth, H = first.shape
    place = lambda v, lane: jnp.pad(v.astype(F32), ((0, 0), (lane, LANES - lane - H)))
    row = jnp.stack([place(first, lane_first), place(second, lane_second)], axis=1)
    return row, jnp.swapaxes(row[:, :, :GATE_ROWS], 1, 2)


def _head_norm_rows(w, H, DV):
    depth = w.shape[0]
    return jnp.broadcast_to(w.astype(F32).reshape(depth, -1, DV), (depth, H, DV)).reshape(depth, 1, H * DV)


def _trunk(x, conv0, S0, C0, n0, m0, P, *, B, T_valid, L):
    M, D = x.shape
    NC = -(-T_valid // L)
    T_pad = NC * L
    pad_t = lambda a: a if T_pad == T_valid else jnp.pad(a, ((0, 0), (0, T_pad - T_valid), (0, 0)))
    depth = P["w_in"].shape[0]
    offs = P["offs"]
    _, _, H, DK, DV = C0.shape
    C0_all = C0.reshape(depth, B, H * DK, DV)
    n0_all = n0.reshape(depth, B, 1, H * DK)
    m0_all = jnp.pad(m0.reshape(depth, B, 1, H), ((0, 0), (0, 0), (0, 0), (3 * H, LANES - 4 * H)))
    new = ([], [], [])
    S_all, C_all = None, None
    for l in range(depth):
        proj, small = _norm_proj(x, P["norm_mix"], P["w_in"], l)
        proj3 = pad_t(proj.reshape(B, T_valid, proj.shape[1]))
        small3 = pad_t(small.reshape(B, T_valid, LANES))
        smT = jnp.swapaxes(small3[:, :, :GATE_ROWS].reshape(B, NC, L, GATE_ROWS), 2, 3)
        og, conv_n, S_all = _gdn(proj3, small3, smT, conv0, S0, l, S_all, P["conv_w"], P["gdn_prow"],
                                 P["gdn_pcol"], P["gdn_norm"], NC=NC, L=L, T_valid=T_valid, offs=offs)
        hm, C_all, n_n, m_n = _mlstm(proj3, small3, smT, C0_all, l, C_all, n0_all, m0_all, P["ml_prow"],
                                     P["ml_pcol"], P["ml_norm"], NC=NC, L=L, T_valid=T_valid, offs=offs)
        x = _merge(og[:, :T_valid].reshape(M, -1), hm[:, :T_valid].reshape(M, -1), x, P["norm_mix"], P["w_gate"],
                   P["w_bg"], P["w_bm"], P["w_out"], l)
        x = _mlp(x, P["norm_mlp"], P["w_up"], P["w_down"], P["norm_final"], l, final_norm=(l == depth - 1))
        for lst, s in zip(new, (conv_n, n_n, m_n)):
            lst.append(s)
    conv_new, n_new, m_new = (jnp.stack(lst) for lst in new)
    return x, conv_new, S_all, C_all.reshape(C0.shape), n_new, m_new


def kernel(x_prompt, x_sample, state_gdn_conv, state_gdn_S, state_mlstm_C, state_mlstm_n, state_mlstm_m, norm_mix, w_in, gdn_conv_w, gdn_A_log, gdn_dt_bias, gdn_norm, ml_i_bias, ml_f_bias, ml_norm, w_branch_gdn, w_branch_ml, w_out, norm_mlp, w_up, w_down, norm_final):
    Bp, Tp, D = x_prompt.shape
    Bs, Ts, _ = x_sample.shape
    depth = w_in.shape[0]
    _, _, Hg, DKg, DVg = state_gdn_S.shape
    _, _, Hm, DKm, DVm = state_mlstm_C.shape
    CG = state_gdn_conv.shape[-1]
    dims = (D, Hg, DKg, DVg, Hm, DKm, DVm)
    assert Ts >= CONV_W - 1 and Tp >= CONV_W - 1

    w_rec, w_gate, offs = _pack_w_in(w_in, dims)
    P = {
        "offs": offs,
        "w_in": w_rec,
        "w_gate": w_gate,
        "norm_mix": norm_mix.reshape(depth, 1, D),
        "norm_mlp": norm_mlp.reshape(depth, 1, D),
        "norm_final": norm_final.reshape(1, D),
        "conv_w": gdn_conv_w,
        "w_bg": w_branch_gdn.astype(BF16),
        "w_bm": w_branch_ml.astype(BF16),
        "w_out": w_out.astype(BF16),
        "w_up": w_up.astype(BF16),
        "w_down": w_down.astype(BF16),
        "gdn_norm": _head_norm_rows(gdn_norm, Hg, DVg),
        "ml_norm": _head_norm_rows(ml_norm, Hm, DVm),
    }
    P["gdn_prow"], P["gdn_pcol"] = _gate_params(gdn_A_log, Hg, gdn_dt_bias, Hg)
    P["ml_prow"], P["ml_pcol"] = _gate_params(ml_i_bias, 2 * Hg, ml_f_bias, 2 * Hg + Hm)

    Lp = CHUNK if Tp % CHUNK == 0 else Tp
    assert Lp % SUBLANES == 0
    zeros = lambda *s: jnp.zeros(s, F32)
    yp, conv_p, S_p, C_p, n_p, m_p = _trunk(
        x_prompt.reshape(Bp * Tp, D), zeros(depth, Bp, CONV_W - 1, CG), zeros(depth, Bp, Hg, DKg, DVg),
        zeros(depth, Bp, Hm, DKm, DVm), zeros(depth, Bp, Hm, DKm), zeros(depth, Bp, Hm), P,
        B=Bp, T_valid=Tp, L=Lp)

    Ls = -(-Ts // SUBLANES) * SUBLANES
    ys, conv_s, S_s, C_s, n_s, m_s = _trunk(
        x_sample.reshape(Bs * Ts, D), state_gdn_conv, state_gdn_S, state_mlstm_C, state_mlstm_n, state_mlstm_m, P,
        B=Bs, T_valid=Ts, L=Ls)
    return (yp.reshape(Bp, Tp, D), ys.reshape(Bs, Ts, D), conv_p, S_p, C_p, n_p, m_p, conv_s, S_s, C_s, n_s, m_s)
```

```python
import functools

import jax
import jax.numpy as jnp
from jax import lax
from jax.experimental import pallas as pl
from jax.experimental.pallas import tpu as pltpu

F32 = jnp.float32
BF16 = jnp.bfloat16
EPS = 1e-6
CONV_W = 4
LANES = 128
SUBLANES = 8
GATE_ROWS = 16
CHUNK = 64
CHAINS = 32
ROW_PARTS = 2
NEG_BIG = -1e30
VMEM_LIMIT = 48 * 1024 * 1024
HIGHEST = lax.Precision.HIGHEST


def _sigmoid(x):
    return 1.0 / (1.0 + jnp.exp(-x))


def _softplus(x):
    return jnp.maximum(x, 0.0) + jnp.log(1.0 + jnp.exp(-jnp.abs(x)))


def _mm(a, b):
    return jnp.dot(a.astype(BF16), b.astype(BF16), preferred_element_type=F32)


def _mm_nt(a, b):
    return lax.dot_general(a.astype(BF16), b.astype(BF16), (((1,), (1,)), ((), ())), preferred_element_type=F32)


def _mm_tn(a, b):
    return lax.dot_general(a.astype(BF16), b.astype(BF16), (((0,), (0,)), ((), ())), preferred_element_type=F32)


def _mm_tn_list(As, Bs):
    K, M = As[0].shape
    if K < 64:
        return [_mm_tn(a, b) for a, b in zip(As, Bs)]
    eye = jnp.where(lax.broadcasted_iota(jnp.int32, (M, M), 0) == lax.broadcasted_iota(jnp.int32, (M, M), 1),
                    1.0, 0.0).astype(BF16)
    a_t = [_mm_nt(eye, a) for a in As]
    return [_mm(a, b) for a, b in zip(a_t, Bs)]


def _split_bf16(a):
    hi = a.astype(BF16)
    return hi, (a - hi.astype(F32)).astype(BF16)


def _tile(n, cap, mult):
    best = None
    for t in range(mult, min(n, cap) + 1, mult):
        if n % t == 0:
            best = t
    return best if best is not None else n


def _tri_masks(L):
    row = lax.broadcasted_iota(jnp.int32, (L, L), 0)
    col = lax.broadcasted_iota(jnp.int32, (L, L), 1)
    return row >= col, row > col


def _cumsum_both(g_col, g_row, incl):
    L = incl.shape[0]
    row = lax.broadcasted_iota(jnp.int32, (L, L), 0)
    col = lax.broadcasted_iota(jnp.int32, (L, L), 1)
    lower = jnp.where(incl, 1.0, 0.0)
    upper = jnp.where(row <= col, 1.0, 0.0)
    c_col = jnp.dot(lower, g_col, precision=HIGHEST, preferred_element_type=F32)
    c_row = jnp.dot(g_row, upper, precision=HIGHEST, preferred_element_type=F32)
    return c_col, c_row


def _tri_inv_unit_lower(As):
    L = As[0].shape[0]
    row = lax.broadcasted_iota(jnp.int32, (L, L), 0)
    col = lax.broadcasted_iota(jnp.int32, (L, L), 1)
    eye = jnp.where(row == col, 1.0, 0.0)
    levels = max(1, (L - 1).bit_length())
    Xs = [eye - A for A in As]
    if levels > 1:
        Ps = [_mm(A, A) for A in As]
        for _ in range(1, levels - 1):
            Rs = [_mm(jnp.concatenate([P, X], axis=0), P) for P, X in zip(Ps, Xs)]
            Ps = [R[:L] for R in Rs]
            Xs = [X + R[L:] for X, R in zip(Xs, Rs)]
        Xs = [X + _mm(X, P) for X, P in zip(Xs, Ps)]
    splits = [(_split_bf16(A), _split_bf16(X)) for A, X in zip(As, Xs)]
    AX1 = [jnp.dot(jnp.concatenate([a_hi, a_lo], axis=0), x_hi, preferred_element_type=F32)
           for (a_hi, a_lo), (x_hi, _) in splits]
    AX2 = [jnp.dot(a_hi, x_lo, preferred_element_type=F32) for (a_hi, _), (_, x_lo) in splits]
    Rs = [(eye - X) - (r1[:L] + (r1[L:] + r2)) for X, r1, r2 in zip(Xs, AX1, AX2)]
    return Xs, Rs


def _layer_spec(layer, *shape):
    return pl.BlockSpec((None,) + shape, lambda *_: (layer,) + (0,) * len(shape))


def _row_parts(tm):
    return [pl.ds(r * (tm // ROW_PARTS), tm // ROW_PARTS) for r in range(ROW_PARTS)]


def _norm_proj_kernel(x_ref, g_ref, w_ref, o_ref, s_ref):
    parts = _row_parts(x_ref.shape[0])
    x = [x_ref[p, :] for p in parts]
    xn = [(v * lax.rsqrt(jnp.mean(v * v, axis=-1, keepdims=True) + EPS) * g_ref[...]).astype(BF16) for v in x]
    res = [jnp.dot(v, w_ref[...], preferred_element_type=F32) for v in xn]
    for p, r in zip(parts, res):
        o_ref[p, :] = r[:, :o_ref.shape[1]]
        s_ref[p, :] = r[:, o_ref.shape[1]:]


def _norm_proj(x, gamma_all, w_all, layer):
    M, D = x.shape
    N = w_all.shape[2]
    tm = _tile(M, 512, SUBLANES * ROW_PARTS)
    return pl.pallas_call(
        _norm_proj_kernel,
        grid=(M // tm,),
        in_specs=[pl.BlockSpec((tm, D), lambda i: (i, 0)),
                  _layer_spec(layer, 1, D),
                  _layer_spec(layer, D, N)],
        out_specs=[pl.BlockSpec((tm, N - LANES), lambda i: (i, 0)),
                   pl.BlockSpec((tm, LANES), lambda i: (i, 0))],
        out_shape=[jax.ShapeDtypeStruct((M, N - LANES), F32),
                   jax.ShapeDtypeStruct((M, LANES), F32)],
        compiler_params=pltpu.CompilerParams(dimension_semantics=("parallel",), vmem_limit_bytes=VMEM_LIMIT),
        name="norm_proj",
    )(x, gamma_all, w_all)


def _gdn_prep_seq(b, qkv_ref, sm_ref, smT_ref, xp_ref, convn_ref, cw_ref, prow_ref, pcol_ref,
                  *, L, NC, T_valid, H, DK, DV):
    K = H * DK
    pad0 = SUBLANES - (CONV_W - 1)
    incl, _ = _tri_masks(L)
    lv = L if NC > 1 else T_valid
    xp_ref[b, SUBLANES:SUBLANES + L, :] = qkv_ref[b]
    xfull = xp_ref[b]
    y = xfull * cw_ref[CONV_W - 1:CONV_W, :]
    for j in range(CONV_W - 1):
        y = y + pltpu.roll(xfull, CONV_W - 1 - j, axis=0) * cw_ref[j:j + 1, :]
    y = y[SUBLANES:, :]
    qkv = y * _sigmoid(y)
    tail = xp_ref[b, SUBLANES + lv - (CONV_W - 1):SUBLANES + lv, :]
    xp_ref[b, pad0:SUBLANES, :] = tail
    convn_ref[b] = tail

    sm = sm_ref[b]
    smT = smT_ref[b, 0]
    beta_c = _sigmoid(sm)
    g_c = -jnp.exp(prow_ref[0:1, :]) * _softplus(sm + prow_ref[1:2, :])
    g_r = -jnp.exp(pcol_ref[:, 0:1]) * _softplus(smT + pcol_ref[:, 1:2])
    if T_valid < NC * L:
        vc = lax.broadcasted_iota(jnp.int32, (L, 1), 0) < T_valid
        vr = lax.broadcasted_iota(jnp.int32, (1, L), 1) < T_valid
        beta_c = jnp.where(vc, beta_c, 0.0)
        g_c = jnp.where(vc, g_c, 0.0)
        g_r = jnp.where(vr, g_r, 0.0)
    gc_c, gc_r = _cumsum_both(g_c, g_r, incl)
    chains = []
    for h in range(H):
        qh = qkv[:, h * DK:(h + 1) * DK]
        kh = qkv[:, K + h * DK:K + (h + 1) * DK]
        vh = qkv[:, 2 * K + h * DV:2 * K + (h + 1) * DV]
        qh = qh * lax.rsqrt(jnp.sum(qh * qh, axis=-1, keepdims=True) + EPS) * (DK ** -0.5)
        kh = kh * lax.rsqrt(jnp.sum(kh * kh, axis=-1, keepdims=True) + EPS)
        b_c = beta_c[:, h:h + 1]
        gcc = gc_c[:, H + h:H + h + 1]
        gcr = gc_r[H + h:H + h + 1, :]
        gl = gcc[L - 1:L, :]
        eg = jnp.exp(gcc)
        kb = kh * b_c
        chains.append(dict(
            kh=kh.astype(BF16), kbq=jnp.concatenate([kb, qh], axis=0).astype(BF16),
            decay=jnp.exp(jnp.where(incl, gcc - gcr, -jnp.inf)),
            rhs=jnp.concatenate([kb * eg, vh * b_c], axis=1), qd=(qh * eg).astype(BF16),
            kd=(kh * jnp.exp(gl - gcc)).astype(BF16), dl=jnp.exp(gl)))
    return chains


def _gdn_chain_phase(chains, S_ref, o_ref, gnorm_ref, *, L, DK, DV):
    _, strict = _tri_masks(L)
    ops = [c[2] for c in chains]
    r1 = [_mm_nt(c["kbq"], c["kh"]) for c in ops]
    A = [jnp.where(strict, r[:L] * c["decay"], 0.0) for r, c in zip(r1, ops)]
    qk = [(r[L:] * c["decay"]).astype(BF16) for r, c in zip(r1, ops)]
    Xs, Rs = _tri_inv_unit_lower(A)
    corr = [_mm(R, c["rhs"]) for R, c in zip(Rs, ops)]
    wu = [_mm(X, c["rhs"] + cr) for X, c, cr in zip(Xs, ops, corr)]
    states = [S_ref[b, h] for b, h, _ in chains]
    wq = [_mm(jnp.concatenate([x[:, :DK].astype(BF16), c["qd"]], axis=0), S) for x, c, S in zip(wu, ops, states)]
    v_new = [(x[:, DK:] - y[:L]).astype(BF16) for x, y in zip(wu, wq)]
    o_intra = [_mm(q, v) for q, v in zip(qk, v_new)]
    s_upd = [_mm_tn(c["kd"], v) for c, v in zip(ops, v_new)]
    for (b, h, c), S, su in zip(chains, states, s_upd):
        S_ref[b, h] = S * c["dl"] + su
    o = [y[L:] + oi for y, oi in zip(wq, o_intra)]
    rs = [lax.rsqrt(jnp.mean(x * x, axis=-1, keepdims=True) + EPS) for x in o]
    for x, r, (b, h, _) in zip(o, rs, chains):
        o_ref[b, :, h * DV:(h + 1) * DV] = x * r * gnorm_ref[:, h * DV:(h + 1) * DV]


def _gdn_kernel(qkv_ref, sm_ref, smT_ref, conv0_ref, S0_ref, cw_ref, prow_ref, pcol_ref, gnorm_ref,
                o_ref, convn_ref, S_ref, xp_ref, *, nb, L, NC, T_valid, H, DK, DV):
    pad0 = SUBLANES - (CONV_W - 1)

    @pl.when(pl.program_id(1) == 0)
    def _():
        xp_ref[:, pad0:SUBLANES, :] = conv0_ref[...]
        S_ref[...] = S0_ref[...]

    chains = []
    for b in range(nb):
        ops = _gdn_prep_seq(b, qkv_ref, sm_ref, smT_ref, xp_ref, convn_ref, cw_ref, prow_ref, pcol_ref,
                            L=L, NC=NC, T_valid=T_valid, H=H, DK=DK, DV=DV)
        chains += [(b, h, c) for h, c in enumerate(ops)]
    _gdn_chain_phase(chains, S_ref, o_ref, gnorm_ref, L=L, DK=DK, DV=DV)


def _skip_ref(kernel_fn, index, count=1):
    def wrapped(*refs, **kw):
        kernel_fn(*refs[:index], *refs[index + count:], **kw)
    return wrapped


def _gdn(proj, small, smT, conv0_all, S0_all, layer, prev, cw, prow, pcol, gnorm, *, NC, L, T_valid, offs):
    depth, B, H, DK, DV = S0_all.shape
    CG = conv0_all.shape[-1]
    V = H * DV
    T_pad = proj.shape[1]
    nb = _tile(B, max(1, CHAINS // H), 1)
    kern = functools.partial(_gdn_kernel, nb=nb, L=L, NC=NC, T_valid=T_valid, H=H, DK=DK, DV=DV)
    in_specs = [pl.BlockSpec((nb, L, CG), lambda i, n: (i, n, offs["qkv"] // CG)),
                pl.BlockSpec((nb, L, LANES), lambda i, n: (i, n, 0)),
                pl.BlockSpec((nb, 1, GATE_ROWS, L), lambda i, n: (i, n, 0, 0)),
                pl.BlockSpec((None, nb, CONV_W - 1, CG), lambda i, n: (layer, i, 0, 0)),
                pl.BlockSpec((None, nb, H, DK, DV), lambda i, n: (layer, i, 0, 0, 0)),
                _layer_spec(layer, CONV_W, CG),
                _layer_spec(layer, 2, LANES),
                _layer_spec(layer, GATE_ROWS, 2),
                _layer_spec(layer, 1, V)]
    operands = [proj, small, smT, conv0_all, S0_all, cw, prow, pcol, gnorm]
    aliases = {}
    if prev is not None:
        kern = _skip_ref(kern, len(operands), 2)
        aliases = {len(operands): 1, len(operands) + 1: 2}
        in_specs += [pl.BlockSpec(memory_space=pl.ANY)] * 2
        operands += list(prev)
    return pl.pallas_call(
        kern,
        grid=(B // nb, NC),
        in_specs=in_specs,
        out_specs=[pl.BlockSpec((nb, L, V), lambda i, n: (i, n, 0)),
                   pl.BlockSpec((None, nb, CONV_W - 1, CG), lambda i, n: (layer, i, 0, 0)),
                   pl.BlockSpec((None, nb, H, DK, DV), lambda i, n: (layer, i, 0, 0, 0))],
        out_shape=[jax.ShapeDtypeStruct((B, T_pad, V), F32),
                   jax.ShapeDtypeStruct(conv0_all.shape, F32),
                   jax.ShapeDtypeStruct(S0_all.shape, F32)],
        scratch_shapes=[pltpu.VMEM((nb, SUBLANES + L, CG), F32)],
        input_output_aliases=aliases,
        compiler_params=pltpu.CompilerParams(dimension_semantics=("parallel", "arbitrary"),
                                             vmem_limit_bytes=VMEM_LIMIT),
        name="gdn_chunk",
    )(*operands)


def _mlstm_kernel(qk_ref, v_ref, sm_ref, smT_ref, C0_ref, n0_ref, m0_ref, prow_ref, pcol_ref, norm_ref,
                  h_ref, C_ref, n_ref, m_ref, *, nb, L, NC, T_valid, H, DK, DV):
    step = pl.program_id(1)
    K = H * DK
    g0 = 3 * H

    @pl.when(step == 0)
    def _():
        C_ref[...] = C0_ref[...]
        n_ref[...] = n0_ref[...]
        m_ref[...] = m0_ref[...]

    incl, _ = _tri_masks(L)
    lane = lax.broadcasted_iota(jnp.int32, (1, K), 1)
    lane1 = lax.broadcasted_iota(jnp.int32, (1, LANES), 1)
    rows = lax.broadcasted_iota(jnp.int32, (L, 1), 0)
    seqs = range(nb)

    sm = [sm_ref[b] for b in seqs]
    smT = [smT_ref[b, 0] for b in seqs]
    li_c = [x + prow_ref[0:1, :] for x in sm]
    li_r = [x + pcol_ref[:, 0:1] for x in smT]
    lf_c = [-_softplus(-(x + prow_ref[1:2, :])) for x in sm]
    lf_r = [-_softplus(-(x + pcol_ref[:, 1:2])) for x in smT]
    if T_valid < NC * L:
        vc = rows < T_valid
        vr = lax.broadcasted_iota(jnp.int32, (1, L), 1) < T_valid
        li_c = [jnp.where(vc, x, NEG_BIG) for x in li_c]
        li_r = [jnp.where(vr, x, NEG_BIG) for x in li_r]
        lf_c = [jnp.where(vc, x, 0.0) for x in lf_c]
        lf_r = [jnp.where(vr, x, 0.0) for x in lf_r]
    FF = [_cumsum_both(c, r, incl) for c, r in zip(lf_c, lf_r)]
    F_c = [f[0] for f in FF]
    F_r = [f[1] for f in FF]
    r_c = [pltpu.roll(x, H, axis=1) - f for x, f in zip(li_c, F_c)]
    cm = r_c
    s = 1
    while s < L:
        cm = [jnp.maximum(x, jnp.where(rows >= s, pltpu.roll(x, s, axis=0), -jnp.inf)) for x in cm]
        s *= 2
    m_old = [m_ref[b] for b in seqs]
    mx = [jnp.maximum(x, m) for x, m in zip(cm, m_old)]
    mx_last = [x[L - 1:L, :] for x in mx]
    dec = [jnp.exp(m - x) for m, x in zip(m_old, mx_last)]
    wC = [jnp.exp(r - x) for r, x in zip(r_c, mx_last)]
    a_all = [jnp.exp(m - x) for m, x in zip(m_old, mx)]
    floor = [jnp.exp(-f - x) for f, x in zip(F_c, mx)]
    for b in seqs:
        m_ref[b] = jnp.where((lane1 >= g0) & (lane1 < g0 + H), F_c[b][L - 1:L, :] + mx_last[b], 0.0)

    q_all = [qk_ref[b, :, :K] * (DK ** -0.5) for b in seqs]
    k_all = [qk_ref[b, :, K:] for b in seqs]
    k_bf = [x.astype(BF16) for x in k_all]
    C_all = [C_ref[b] for b in seqs]
    C_bf = [x.astype(BF16) for x in C_all]
    n_row = [n_ref[b] for b in seqs]

    half = lane1 < DK
    kw_cols, dec_lanes = [], []
    for b in seqs:
        cols = []
        for c in range(K // LANES):
            w_lo = wC[b][:, g0 + 2 * c:g0 + 2 * c + 1]
            w_hi = wC[b][:, g0 + 2 * c + 1:g0 + 2 * c + 2]
            cols.append(k_all[b][:, c * LANES:(c + 1) * LANES] * jnp.where(half, w_lo, w_hi))
        kw_cols.append(cols)
        d = dec[b][:, g0:g0 + 1]
        for h in range(1, H):
            d = jnp.where(lane >= h * DK, dec[b][:, g0 + h:g0 + h + 1], d)
        dec_lanes.append(d)
    for b in seqs:
        n_upd = jnp.concatenate([jnp.sum(x, axis=0, keepdims=True) for x in kw_cols[b]], axis=1)
        n_ref[b] = dec_lanes[b] * n_row[b] + n_upd

    chains = [(b, h) for b in seqs for h in range(H)]
    in_head = [(lane >= h * DK) & (lane < (h + 1) * DK) for h in range(H)]
    q_h = [jnp.where(in_head[h], q_all[b], 0.0) for b, h in chains]
    q_bf = [x.astype(BF16) for x in q_h]
    v_bf = [v_ref[b, :, h * DV:(h + 1) * DV].astype(BF16) for b, h in chains]
    kw = [kw_cols[b][h // 2][:, (h % 2) * DK:(h % 2 + 1) * DK].astype(BF16) for b, h in chains]
    expD = [jnp.exp(jnp.where(incl, (li_r[b][2 * H + h:2 * H + h + 1, :] - F_r[b][g0 + h:g0 + h + 1, :])
                              - mx[b][:, g0 + h:g0 + h + 1], -jnp.inf)) for b, h in chains]
    qn = [jnp.sum(x * n_row[b], axis=1, keepdims=True) for x, (b, h) in zip(q_h, chains)]

    qk = [_mm_nt(x, k_bf[b]) for x, (b, h) in zip(q_bf, chains)]
    Sm = [e * x for e, x in zip(expD, qk)]
    inter_state = [_mm(x, C_bf[b]) for x, (b, h) in zip(q_bf, chains)]
    intra = [_mm(x, v) for x, v in zip(Sm, v_bf)]
    upd = _mm_tn_list(kw, v_bf)
    rowsum = [jnp.sum(x, axis=1, keepdims=True) for x in Sm]
    for i, (b, h) in enumerate(chains):
        C_ref[b, h * DK:(h + 1) * DK, :] = dec[b][:, g0 + h:g0 + h + 1] * C_all[b][h * DK:(h + 1) * DK, :] + upd[i]
    a_h = [a_all[b][:, g0 + h:g0 + h + 1] for b, h in chains]
    den = [a * x + y for a, x, y in zip(a_h, qn, rowsum)]
    scale = [1.0 / jnp.maximum(jnp.abs(d), floor[b][:, g0 + h:g0 + h + 1]) for d, (b, h) in zip(den, chains)]
    hh = [(a * x + y) * sc for a, x, y, sc in zip(a_h, inter_state, intra, scale)]
    ms = [jnp.mean(x * x, axis=-1, keepdims=True) for x in hh]
    rs = [lax.rsqrt(x + EPS) for x in ms]
    for x, r, (b, h) in zip(hh, rs, chains):
        h_ref[b, :, h * DV:(h + 1) * DV] = x * r * norm_ref[:, h * DV:(h + 1) * DV]


def _mlstm(proj, small, smT, C0_all, layer, prev, n0_all, m0_all, prow, pcol, norm, *, NC, L, T_valid, offs):
    depth, B, K, DV = C0_all.shape
    V = norm.shape[-1]
    H = V // DV
    DK = K // H
    T_pad = proj.shape[1]
    nb = _tile(B, max(1, CHAINS // H), 1)
    assert 2 * DK == LANES and 4 * H <= LANES
    kern = functools.partial(_mlstm_kernel, nb=nb, L=L, NC=NC, T_valid=T_valid, H=H, DK=DK, DV=DV)
    in_specs = [pl.BlockSpec((nb, L, 2 * K), lambda i, n: (i, n, offs["qk_m"] // (2 * K))),
                pl.BlockSpec((nb, L, V), lambda i, n: (i, n, offs["v_m"] // V)),
                pl.BlockSpec((nb, L, LANES), lambda i, n: (i, n, 0)),
                pl.BlockSpec((nb, 1, GATE_ROWS, L), lambda i, n: (i, n, 0, 0)),
                pl.BlockSpec((None, nb, K, DV), lambda i, n: (layer, i, 0, 0)),
                pl.BlockSpec((None, nb, 1, K), lambda i, n: (layer, i, 0, 0)),
                pl.BlockSpec((None, nb, 1, LANES), lambda i, n: (layer, i, 0, 0)),
                _layer_spec(layer, 2, LANES),
                _layer_spec(layer, GATE_ROWS, 2),
                _layer_spec(layer, 1, V)]
    operands = [proj, proj, small, smT, C0_all, n0_all, m0_all, prow, pcol, norm]
    aliases = {}
    if prev is not None:
        kern = _skip_ref(kern, len(operands), 3)
        aliases = {len(operands) + k: 1 + k for k in range(3)}
        in_specs += [pl.BlockSpec(memory_space=pl.ANY)] * 3
        operands += list(prev)
    return pl.pallas_call(
        kern,
        grid=(B // nb, NC),
        in_specs=in_specs,
        out_specs=[pl.BlockSpec((nb, L, V), lambda i, n: (i, n, 0)),
                   pl.BlockSpec((None, nb, K, DV), lambda i, n: (layer, i, 0, 0)),
                   pl.BlockSpec((None, nb, 1, K), lambda i, n: (layer, i, 0, 0)),
                   pl.BlockSpec((None, nb, 1, LANES), lambda i, n: (layer, i, 0, 0))],
        out_shape=[jax.ShapeDtypeStruct((B, T_pad, V), F32),
                   jax.ShapeDtypeStruct(C0_all.shape, F32),
                   jax.ShapeDtypeStruct(n0_all.shape, F32),
                   jax.ShapeDtypeStruct(m0_all.shape, F32)],
        input_output_aliases=aliases,
        compiler_params=pltpu.CompilerParams(dimension_semantics=("parallel", "arbitrary"),
                                             vmem_limit_bytes=VMEM_LIMIT),
        name="mlstm_chunk",
    )(*operands)


def _merge_kernel(og_ref, hm_ref, x_ref, g_ref, wg_ref, wbg_ref, wbm_ref, wout_ref, o_ref, *, Vg, Vm, D):
    dot = functools.partial(jnp.dot, preferred_element_type=F32)
    parts = _row_parts(x_ref.shape[0])
    x = [x_ref[p, :] for p in parts]
    xn = [(v * lax.rsqrt(jnp.mean(v * v, axis=-1, keepdims=True) + EPS) * g_ref[...]).astype(BF16) for v in x]
    z = [dot(v, wg_ref[:, 0:Vg]) for v in xn]
    og = [(og_ref[p, :] * (v * _sigmoid(v))).astype(BF16) for p, v in zip(parts, z)]
    br_g = [dot(v, wbg_ref[...]) for v in og]
    om = [dot(v, wg_ref[:, Vg:Vg + Vm]) for v in xn]
    hm = [(hm_ref[p, :] * _sigmoid(v)).astype(BF16) for p, v in zip(parts, om)]
    br_m = [dot(v, wbm_ref[...]) for v in hm]
    gg = [dot(v, wg_ref[:, Vg + Vm:Vg + Vm + D]) for v in xn]
    merged = [_sigmoid(g) * b for g, b in zip(gg, br_g)]
    gm = [dot(v, wg_ref[:, Vg + Vm + D:]) for v in xn]
    merged = [m + _sigmoid(g) * b for m, g, b in zip(merged, gm, br_m)]
    for p, v, m in zip(parts, x, merged):
        o_ref[p, :] = v + dot(m.astype(BF16), wout_ref[...])


def _merge(og, hm, x, gamma, wg, wbg, wbm, wout, layer):
    M, D = x.shape
    Vg = og.shape[1]
    Vm = hm.shape[1]
    tm = _tile(M, 512, SUBLANES * ROW_PARTS)
    return pl.pallas_call(
        functools.partial(_merge_kernel, Vg=Vg, Vm=Vm, D=D),
        grid=(M // tm,),
        in_specs=[pl.BlockSpec((tm, Vg), lambda i: (i, 0)),
                  pl.BlockSpec((tm, Vm), lambda i: (i, 0)),
                  pl.BlockSpec((tm, D), lambda i: (i, 0)),
                  _layer_spec(layer, 1, D),
                  _layer_spec(layer, D, Vg + Vm + 2 * D),
                  _layer_spec(layer, Vg, D),
                  _layer_spec(layer, Vm, D),
                  _layer_spec(layer, D, D)],
        out_specs=pl.BlockSpec((tm, D), lambda i: (i, 0)),
        out_shape=jax.ShapeDtypeStruct((M, D), F32),
        compiler_params=pltpu.CompilerParams(dimension_semantics=("parallel",), vmem_limit_bytes=VMEM_LIMIT),
        name="merge_out",
    )(og, hm, x, gamma, wg, wbg, wbm, wout)


def _mlp_kernel(x_ref, g_ref, wup_ref, wdn_ref, gf_ref, o_ref, *, final_norm, tf):
    x = x_ref[...]
    ms = jnp.mean(x * x, axis=-1, keepdims=True)
    xn = (x * lax.rsqrt(ms + EPS) * g_ref[...]).astype(BF16)
    y = x
    for c in range(wup_ref.shape[1] // tf):
        hcol = jnp.maximum(jnp.dot(xn, wup_ref[:, c * tf:(c + 1) * tf], preferred_element_type=F32), 0.0)
        y = y + jnp.dot((hcol * hcol).astype(BF16), wdn_ref[c * tf:(c + 1) * tf, :], preferred_element_type=F32)
    if final_norm:
        ms = jnp.mean(y * y, axis=-1, keepdims=True)
        y = y * lax.rsqrt(ms + EPS) * gf_ref[...]
    o_ref[...] = y


def _mlp(x, gamma, wup, wdn, gamma_final, layer, *, final_norm):
    M, D = x.shape
    FF = wup.shape[2]
    tm = _tile(M, 512, SUBLANES)
    tf = _tile(FF, 1024, LANES)
    resident = pl.Buffered(1)
    return pl.pallas_call(
        functools.partial(_mlp_kernel, final_norm=final_norm, tf=tf),
        grid=(M // tm,),
        in_specs=[pl.BlockSpec((tm, D), lambda i: (i, 0)),
                  _layer_spec(layer, 1, D),
                  pl.BlockSpec((None, D, FF), lambda i: (layer, 0, 0), pipeline_mode=resident),
                  pl.BlockSpec((None, FF, D), lambda i: (layer, 0, 0), pipeline_mode=resident),
                  pl.BlockSpec((1, D), lambda i: (0, 0))],
        out_specs=pl.BlockSpec((tm, D), lambda i: (i, 0)),
        out_shape=jax.ShapeDtypeStruct((M, D), F32),
        compiler_params=pltpu.CompilerParams(dimension_semantics=("parallel",), vmem_limit_bytes=VMEM_LIMIT),
        name="mlp",
    )(x, gamma, wup, wdn, gamma_final)


def _repack_kernel(w_ref, rec_ref, gate_ref, *, rec_moves, gate_moves, rec_fill):
    for out_ref, moves in ((rec_ref, rec_moves), (gate_ref, gate_moves)):
        for a, b, d in moves:
            out_ref[:, d:d + (b - a)] = w_ref[:, a:b].astype(BF16)
    rec_ref[:, rec_fill:] = jnp.zeros((rec_ref.shape[0], rec_ref.shape[1] - rec_fill), BF16)


def _pack_w_in(w_in, dims):
    D, Hg, DKg, DVg, Hm, DKm, DVm = dims
    Kg, Vg, Km, Vm = Hg * DKg, Hg * DVg, Hm * DKm, Hm * DVm
    sizes = (Kg, Kg, Vg, Vg, Hg, Hg, Km, Km, Vm, Vm, Hm, Hm, D, D)
    starts = [0]
    for s in sizes:
        starts.append(starts[-1] + s)
    assert 2 * Hg + 2 * Hm <= GATE_ROWS and Hg == Hm
    depth, _, IN = w_in.shape

    def layout(groups):
        moves, offs, off = [], [], 0
        for ranges in groups:
            offs.append(off)
            for a, b in ranges:
                moves.append((starts[a], starts[b], off))
                off += starts[b] - starts[a]
        return moves, offs, off

    rec_moves, rec_offs, rec_fill = layout([[(0, 3)], [(6, 8)], [(8, 9)], [(4, 6), (10, 12)]])
    gate_moves, _, n_gate = layout([[(3, 4)], [(9, 10)], [(12, 14)]])
    offs = dict(zip(("qkv", "qk_m", "v_m", "small"), rec_offs))
    n_rec = offs["small"] + LANES
    for name, width in (("qkv", 2 * Kg + Vg), ("qk_m", 2 * Km), ("v_m", Vm), ("small", LANES)):
        assert offs[name] % width == 0, (name, offs[name], width)
    tk = _tile(D, 256, 2 * SUBLANES)
    in_pad = -(-IN // LANES) * LANES
    w_cast = jnp.pad(w_in.astype(BF16), ((0, 0), (0, 0), (0, in_pad - IN)))
    w_rec, w_gate = pl.pallas_call(
        functools.partial(_repack_kernel, rec_moves=rec_moves, gate_moves=gate_moves, rec_fill=rec_fill),
        grid=(depth, D // tk),
        in_specs=[pl.BlockSpec((None, tk, in_pad), lambda l, i: (l, i, 0))],
        out_specs=[pl.BlockSpec((None, tk, n_rec), lambda l, i: (l, i, 0)),
                   pl.BlockSpec((None, tk, n_gate), lambda l, i: (l, i, 0))],
        out_shape=[jax.ShapeDtypeStruct((depth, D, n_rec), BF16),
                   jax.ShapeDtypeStruct((depth, D, n_gate), BF16)],
        compiler_params=pltpu.CompilerParams(dimension_semantics=("parallel", "parallel"),
                                             vmem_limit_bytes=VMEM_LIMIT),
        name="repack_w_in",
    )(w_cast)
    return w_rec, w_gate, offs


def _gate_params(first, lane_first, second, lane_second):
    depth, H = first.shape
    place = lambda v, lane: jnp.pad(v.astype(F32), ((0, 0), (lane, LANES - lane - H)))
    row = jnp.stack([place(first, lane_first), place(second, lane_second)], axis=1)
    return row, jnp.swapaxes(row[:, :, :GATE_ROWS], 1, 2)


def _head_norm_rows(w, H, DV):
    depth = w.shape[0]
    return jnp.broadcast_to(w.astype(F32).reshape(depth, -1, DV), (depth, H, DV)).reshape(depth, 1, H * DV)


def _trunk(x, conv0, S0, C0, n0, m0, P, *, B, T_valid, L):
    M, D = x.shape
    NC = -(-T_valid // L)
    T_pad = NC * L
    pad_t = lambda a: a if T_pad == T_valid else jnp.pad(a, ((0, 0), (0, T_pad - T_valid), (0, 0)))
    depth = P["w_in"].shape[0]
    offs = P["offs"]
    _, _, H, DK, DV = C0.shape
    C0_all = C0.reshape(depth, B, H * DK, DV)
    n0_all = n0.reshape(depth, B, 1, H * DK)
    m0_all = jnp.pad(m0.reshape(depth, B, 1, H), ((0, 0), (0, 0), (0, 0), (3 * H, LANES - 4 * H)))
    gdn_new, ml_new = None, None
    for l in range(depth):
        proj, small = _norm_proj(x, P["norm_mix"], P["w_in"], l)
        proj3 = pad_t(proj.reshape(B, T_valid, proj.shape[1]))
        small3 = pad_t(small.reshape(B, T_valid, LANES))
        smT = jnp.swapaxes(small3[:, :, :GATE_ROWS].reshape(B, NC, L, GATE_ROWS), 2, 3)
        og, *gdn_new = _gdn(proj3, small3, smT, conv0, S0, l, gdn_new, P["conv_w"], P["gdn_prow"],
                            P["gdn_pcol"], P["gdn_norm"], NC=NC, L=L, T_valid=T_valid, offs=offs)
        hm, *ml_new = _mlstm(proj3, small3, smT, C0_all, l, ml_new, n0_all, m0_all, P["ml_prow"],
                             P["ml_pcol"], P["ml_norm"], NC=NC, L=L, T_valid=T_valid, offs=offs)
        x = _merge(og[:, :T_valid].reshape(M, -1), hm[:, :T_valid].reshape(M, -1), x, P["norm_mix"], P["w_gate"],
                   P["w_bg"], P["w_bm"], P["w_out"], l)
        x = _mlp(x, P["norm_mlp"], P["w_up"], P["w_down"], P["norm_final"], l, final_norm=(l == depth - 1))
    conv_all, S_all = gdn_new
    C_all, n_all, m_all = ml_new
    return x, conv_all, S_all, C_all.reshape(C0.shape), n_all.reshape(n0.shape), m_all[:, :, 0, 3 * H:4 * H]


def kernel(x_prompt, x_sample, state_gdn_conv, state_gdn_S, state_mlstm_C, state_mlstm_n, state_mlstm_m, norm_mix, w_in, gdn_conv_w, gdn_A_log, gdn_dt_bias, gdn_norm, ml_i_bias, ml_f_bias, ml_norm, w_branch_gdn, w_branch_ml, w_out, norm_mlp, w_up, w_down, norm_final):
    Bp, Tp, D = x_prompt.shape
    Bs, Ts, _ = x_sample.shape
    depth = w_in.shape[0]
    _, _, Hg, DKg, DVg = state_gdn_S.shape
    _, _, Hm, DKm, DVm = state_mlstm_C.shape
    CG = state_gdn_conv.shape[-1]
    dims = (D, Hg, DKg, DVg, Hm, DKm, DVm)
    assert Ts >= CONV_W - 1 and Tp >= CONV_W - 1

    w_rec, w_gate, offs = _pack_w_in(w_in, dims)
    P = {
        "offs": offs,
        "w_in": w_rec,
        "w_gate": w_gate,
        "norm_mix": norm_mix.reshape(depth, 1, D),
        "norm_mlp": norm_mlp.reshape(depth, 1, D),
        "norm_final": norm_final.reshape(1, D),
        "conv_w": gdn_conv_w,
        "w_bg": w_branch_gdn.astype(BF16),
        "w_bm": w_branch_ml.astype(BF16),
        "w_out": w_out.astype(BF16),
        "w_up": w_up.astype(BF16),
        "w_down": w_down.astype(BF16),
        "gdn_norm": _head_norm_rows(gdn_norm, Hg, DVg),
        "ml_norm": _head_norm_rows(ml_norm, Hm, DVm),
    }
    P["gdn_prow"], P["gdn_pcol"] = _gate_params(gdn_A_log, Hg, gdn_dt_bias, Hg)
    P["ml_prow"], P["ml_pcol"] = _gate_params(ml_i_bias, 2 * Hg, ml_f_bias, 2 * Hg + Hm)

    Lp = CHUNK if Tp % CHUNK == 0 else Tp
    assert Lp % SUBLANES == 0
    zeros = lambda *s: jnp.zeros(s, F32)
    yp, conv_p, S_p, C_p, n_p, m_p = _trunk(
        x_prompt.reshape(Bp * Tp, D), zeros(depth, Bp, CONV_W - 1, CG), zeros(depth, Bp, Hg, DKg, DVg),
        zeros(depth, Bp, Hm, DKm, DVm), zeros(depth, Bp, Hm, DKm), zeros(depth, Bp, Hm), P,
        B=Bp, T_valid=Tp, L=Lp)

    Ls = -(-Ts // SUBLANES) * SUBLANES
    ys, conv_s, S_s, C_s, n_s, m_s = _trunk(
        x_sample.reshape(Bs * Ts, D), state_gdn_conv, state_gdn_S, state_mlstm_C, state_mlstm_n, state_mlstm_m, P,
        B=Bs, T_valid=Ts, L=Ls)
    return (yp.reshape(Bp, Tp, D), ys.reshape(Bs, Ts, D), conv_p, S_p, C_p, n_p, m_p, conv_s, S_s, C_s, n_s, m_s)
```

```python
import functools

import jax
import jax.numpy as jnp
from jax import lax
from jax.experimental import pallas as pl
from jax.experimental.pallas import tpu as pltpu

F32 = jnp.float32
BF16 = jnp.bfloat16
EPS = 1e-6
CONV_W = 4
LANES = 128
SUBLANES = 8
GATE_ROWS = 16
CHUNK = 64
CHAINS = 32
ROW_PARTS = 2
NEG_BIG = -1e30
VMEM_LIMIT = 48 * 1024 * 1024
HIGHEST = lax.Precision.HIGHEST


def _sigmoid(x):
    return 1.0 / (1.0 + jnp.exp(-x))


def _softplus(x):
    return jnp.maximum(x, 0.0) + jnp.log(1.0 + jnp.exp(-jnp.abs(x)))


def _mm(a, b):
    return jnp.dot(a.astype(BF16), b.astype(BF16), preferred_element_type=F32)


def _mm_nt(a, b):
    return lax.dot_general(a.astype(BF16), b.astype(BF16), (((1,), (1,)), ((), ())), preferred_element_type=F32)


def _mm_tn(a, b):
    return lax.dot_general(a.astype(BF16), b.astype(BF16), (((0,), (0,)), ((), ())), preferred_element_type=F32)


def _mm_tn_list(As, Bs):
    K, M = As[0].shape
    if K < 64:
        return [_mm_tn(a, b) for a, b in zip(As, Bs)]
    eye = jnp.where(lax.broadcasted_iota(jnp.int32, (M, M), 0) == lax.broadcasted_iota(jnp.int32, (M, M), 1),
                    1.0, 0.0).astype(BF16)
    a_t = [_mm_nt(eye, a) for a in As]
    return [_mm(a, b) for a, b in zip(a_t, Bs)]


def _split_bf16(a):
    hi = a.astype(BF16)
    return hi, (a - hi.astype(F32)).astype(BF16)


def _tile(n, cap, mult):
    best = None
    for t in range(mult, min(n, cap) + 1, mult):
        if n % t == 0:
            best = t
    return best if best is not None else n


def _tri_masks(L):
    row = lax.broadcasted_iota(jnp.int32, (L, L), 0)
    col = lax.broadcasted_iota(jnp.int32, (L, L), 1)
    return row >= col, row > col


def _cumsum_both(g_col, g_row, incl):
    L = incl.shape[0]
    row = lax.broadcasted_iota(jnp.int32, (L, L), 0)
    col = lax.broadcasted_iota(jnp.int32, (L, L), 1)
    lower = jnp.where(incl, 1.0, 0.0)
    upper = jnp.where(row <= col, 1.0, 0.0)
    c_col = jnp.dot(lower, g_col, precision=HIGHEST, preferred_element_type=F32)
    c_row = jnp.dot(g_row, upper, precision=HIGHEST, preferred_element_type=F32)
    return c_col, c_row


def _interleave(gens, bursts):
    alive = list(zip(gens, bursts))
    while alive:
        for pair in list(alive):
            for _ in range(pair[1]):
                if next(pair[0], StopIteration) is StopIteration:
                    alive.remove(pair)
                    break


def _tri_inv_steps(As):
    L = As[0].shape[0]
    row = lax.broadcasted_iota(jnp.int32, (L, L), 0)
    col = lax.broadcasted_iota(jnp.int32, (L, L), 1)
    eye = jnp.where(row == col, 1.0, 0.0)
    levels = max(1, (L - 1).bit_length())
    Xs = [eye - A for A in As]
    if levels > 1:
        Ps = [_mm(A, A) for A in As]
        yield
        for _ in range(1, levels - 1):
            Rs = [_mm(jnp.concatenate([P, X], axis=0), P) for P, X in zip(Ps, Xs)]
            Ps = [R[:L] for R in Rs]
            Xs = [X + R[L:] for X, R in zip(Xs, Rs)]
            yield
        Xs = [X + _mm(X, P) for X, P in zip(Xs, Ps)]
        yield
    splits = [(_split_bf16(A), _split_bf16(X)) for A, X in zip(As, Xs)]
    AX1 = [jnp.dot(jnp.concatenate([a_hi, a_lo], axis=0), x_hi, preferred_element_type=F32)
           for (a_hi, a_lo), (x_hi, _) in splits]
    AX2 = [jnp.dot(a_hi, x_lo, preferred_element_type=F32) for (a_hi, _), (_, x_lo) in splits]
    Rs = [(eye - X) - (r1[:L] + (r1[L:] + r2)) for X, r1, r2 in zip(Xs, AX1, AX2)]
    return Xs, Rs


def _layer_spec(layer, *shape):
    return pl.BlockSpec((None,) + shape, lambda *_: (layer,) + (0,) * len(shape))


def _row_parts(tm):
    return [pl.ds(r * (tm // ROW_PARTS), tm // ROW_PARTS) for r in range(ROW_PARTS)]


def _norm_proj_kernel(x_ref, g_ref, w_ref, o_ref, s_ref):
    parts = _row_parts(x_ref.shape[0])
    x = [x_ref[p, :] for p in parts]
    xn = [(v * lax.rsqrt(jnp.mean(v * v, axis=-1, keepdims=True) + EPS) * g_ref[...]).astype(BF16) for v in x]
    res = [jnp.dot(v, w_ref[...], preferred_element_type=F32) for v in xn]
    for p, r in zip(parts, res):
        o_ref[p, :] = r[:, :o_ref.shape[1]]
        s_ref[p, :] = r[:, o_ref.shape[1]:]


def _norm_proj(x, gamma_all, w_all, layer):
    M, D = x.shape
    N = w_all.shape[2]
    tm = _tile(M, 512, SUBLANES * ROW_PARTS)
    return pl.pallas_call(
        _norm_proj_kernel,
        grid=(M // tm,),
        in_specs=[pl.BlockSpec((tm, D), lambda i: (i, 0)),
                  _layer_spec(layer, 1, D),
                  _layer_spec(layer, D, N)],
        out_specs=[pl.BlockSpec((tm, N - LANES), lambda i: (i, 0)),
                   pl.BlockSpec((tm, LANES), lambda i: (i, 0))],
        out_shape=[jax.ShapeDtypeStruct((M, N - LANES), F32),
                   jax.ShapeDtypeStruct((M, LANES), F32)],
        compiler_params=pltpu.CompilerParams(dimension_semantics=("parallel",), vmem_limit_bytes=VMEM_LIMIT),
        name="norm_proj",
    )(x, gamma_all, w_all)


def _gdn_prep_seq(b, qkv_ref, sm_ref, smT_ref, xp_ref, convn_ref, cw_ref, prow_ref, pcol_ref,
                  *, L, NC, T_valid, H, DK, DV):
    K = H * DK
    pad0 = SUBLANES - (CONV_W - 1)
    incl, _ = _tri_masks(L)
    lv = L if NC > 1 else T_valid
    xp_ref[b, SUBLANES:SUBLANES + L, :] = qkv_ref[b]
    xfull = xp_ref[b]
    y = xfull * cw_ref[CONV_W - 1:CONV_W, :]
    for j in range(CONV_W - 1):
        y = y + pltpu.roll(xfull, CONV_W - 1 - j, axis=0) * cw_ref[j:j + 1, :]
    y = y[SUBLANES:, :]
    qkv = y * _sigmoid(y)
    tail = xp_ref[b, SUBLANES + lv - (CONV_W - 1):SUBLANES + lv, :]
    xp_ref[b, pad0:SUBLANES, :] = tail
    convn_ref[b] = tail

    sm = sm_ref[b]
    smT = smT_ref[b, 0]
    beta_c = _sigmoid(sm)
    g_c = -jnp.exp(prow_ref[0:1, :]) * _softplus(sm + prow_ref[1:2, :])
    g_r = -jnp.exp(pcol_ref[:, 0:1]) * _softplus(smT + pcol_ref[:, 1:2])
    if T_valid < NC * L:
        vc = lax.broadcasted_iota(jnp.int32, (L, 1), 0) < T_valid
        vr = lax.broadcasted_iota(jnp.int32, (1, L), 1) < T_valid
        beta_c = jnp.where(vc, beta_c, 0.0)
        g_c = jnp.where(vc, g_c, 0.0)
        g_r = jnp.where(vr, g_r, 0.0)
    gc_c, gc_r = _cumsum_both(g_c, g_r, incl)
    chains = []
    for h in range(H):
        qh = qkv[:, h * DK:(h + 1) * DK]
        kh = qkv[:, K + h * DK:K + (h + 1) * DK]
        vh = qkv[:, 2 * K + h * DV:2 * K + (h + 1) * DV]
        qh = qh * lax.rsqrt(jnp.sum(qh * qh, axis=-1, keepdims=True) + EPS) * (DK ** -0.5)
        kh = kh * lax.rsqrt(jnp.sum(kh * kh, axis=-1, keepdims=True) + EPS)
        b_c = beta_c[:, h:h + 1]
        gcc = gc_c[:, H + h:H + h + 1]
        gcr = gc_r[H + h:H + h + 1, :]
        gl = gcc[L - 1:L, :]
        eg = jnp.exp(gcc)
        kb = kh * b_c
        chains.append(dict(
            kh=kh.astype(BF16), kbq=jnp.concatenate([kb, qh], axis=0).astype(BF16),
            decay=jnp.exp(jnp.where(incl, gcc - gcr, -jnp.inf)),
            rhs=jnp.concatenate([kb * eg, vh * b_c], axis=1), qd=(qh * eg).astype(BF16),
            kd=(kh * jnp.exp(gl - gcc)).astype(BF16), dl=jnp.exp(gl)))
    return chains


def _gdn_chain_steps(chains, S_ref, o_ref, gnorm_ref, *, L, DK, DV):
    _, strict = _tri_masks(L)
    ops = [c[2] for c in chains]
    r1 = [_mm_nt(c["kbq"], c["kh"]) for c in ops]
    A = [jnp.where(strict, r[:L] * c["decay"], 0.0) for r, c in zip(r1, ops)]
    qk = [(r[L:] * c["decay"]).astype(BF16) for r, c in zip(r1, ops)]
    yield
    Xs, Rs = yield from _tri_inv_steps(A)
    corr = [_mm(R, c["rhs"]) for R, c in zip(Rs, ops)]
    yield
    wu = [_mm(X, c["rhs"] + cr) for X, c, cr in zip(Xs, ops, corr)]
    yield
    states = [S_ref[b, h] for b, h, _ in chains]
    wq = [_mm(jnp.concatenate([x[:, :DK].astype(BF16), c["qd"]], axis=0), S) for x, c, S in zip(wu, ops, states)]
    yield
    v_new = [(x[:, DK:] - y[:L]).astype(BF16) for x, y in zip(wu, wq)]
    o_intra = [_mm(q, v) for q, v in zip(qk, v_new)]
    s_upd = [_mm_tn(c["kd"], v) for c, v in zip(ops, v_new)]
    yield
    for (b, h, c), S, su in zip(chains, states, s_upd):
        S_ref[b, h] = S * c["dl"] + su
    o = [y[L:] + oi for y, oi in zip(wq, o_intra)]
    rs = [lax.rsqrt(jnp.mean(x * x, axis=-1, keepdims=True) + EPS) for x in o]
    for x, r, (b, h, _) in zip(o, rs, chains):
        o_ref[b, :, h * DV:(h + 1) * DV] = x * r * gnorm_ref[:, h * DV:(h + 1) * DV]


def _gdn_steps(qkv_ref, sm_ref, smT_ref, conv0_ref, S0_ref, cw_ref, prow_ref, pcol_ref, gnorm_ref,
               o_ref, convn_ref, S_ref, xp_ref, *, nb, L, NC, T_valid, H, DK, DV):
    pad0 = SUBLANES - (CONV_W - 1)

    @pl.when(pl.program_id(1) == 0)
    def _():
        xp_ref[:, pad0:SUBLANES, :] = conv0_ref[...]
        S_ref[...] = S0_ref[...]

    chains = []
    for b in range(nb):
        ops = _gdn_prep_seq(b, qkv_ref, sm_ref, smT_ref, xp_ref, convn_ref, cw_ref, prow_ref, pcol_ref,
                            L=L, NC=NC, T_valid=T_valid, H=H, DK=DK, DV=DV)
        chains += [(b, h, c) for h, c in enumerate(ops)]
        yield
    yield from _gdn_chain_steps(chains, S_ref, o_ref, gnorm_ref, L=L, DK=DK, DV=DV)


def _skip_ref(kernel_fn, index, count=1):
    def wrapped(*refs, **kw):
        kernel_fn(*refs[:index], *refs[index + count:], **kw)
    return wrapped


def _mlstm_steps(qk_ref, v_ref, sm_ref, smT_ref, C0_ref, n0_ref, m0_ref, prow_ref, pcol_ref, norm_ref,
                 h_ref, C_ref, n_ref, m_ref, *, nb, L, NC, T_valid, H, DK, DV):
    step = pl.program_id(1)
    K = H * DK
    g0 = 3 * H

    @pl.when(step == 0)
    def _():
        C_ref[...] = C0_ref[...]
        n_ref[...] = n0_ref[...]
        m_ref[...] = m0_ref[...]

    incl, _ = _tri_masks(L)
    lane = lax.broadcasted_iota(jnp.int32, (1, K), 1)
    lane1 = lax.broadcasted_iota(jnp.int32, (1, LANES), 1)
    rows = lax.broadcasted_iota(jnp.int32, (L, 1), 0)
    seqs = range(nb)

    sm = [sm_ref[b] for b in seqs]
    smT = [smT_ref[b, 0] for b in seqs]
    li_c = [x + prow_ref[0:1, :] for x in sm]
    li_r = [x + pcol_ref[:, 0:1] for x in smT]
    lf_c = [-_softplus(-(x + prow_ref[1:2, :])) for x in sm]
    lf_r = [-_softplus(-(x + pcol_ref[:, 1:2])) for x in smT]
    if T_valid < NC * L:
        vc = rows < T_valid
        vr = lax.broadcasted_iota(jnp.int32, (1, L), 1) < T_valid
        li_c = [jnp.where(vc, x, NEG_BIG) for x in li_c]
        li_r = [jnp.where(vr, x, NEG_BIG) for x in li_r]
        lf_c = [jnp.where(vc, x, 0.0) for x in lf_c]
        lf_r = [jnp.where(vr, x, 0.0) for x in lf_r]
    yield
    FF = [_cumsum_both(c, r, incl) for c, r in zip(lf_c, lf_r)]
    F_c = [f[0] for f in FF]
    F_r = [f[1] for f in FF]
    yield
    r_c = [pltpu.roll(x, H, axis=1) - f for x, f in zip(li_c, F_c)]
    cm = r_c
    s = 1
    while s < L:
        cm = [jnp.maximum(x, jnp.where(rows >= s, pltpu.roll(x, s, axis=0), -jnp.inf)) for x in cm]
        s *= 2
    m_old = [m_ref[b] for b in seqs]
    mx = [jnp.maximum(x, m) for x, m in zip(cm, m_old)]
    mx_last = [x[L - 1:L, :] for x in mx]
    dec = [jnp.exp(m - x) for m, x in zip(m_old, mx_last)]
    wC = [jnp.exp(r - x) for r, x in zip(r_c, mx_last)]
    a_all = [jnp.exp(m - x) for m, x in zip(m_old, mx)]
    floor = [jnp.exp(-f - x) for f, x in zip(F_c, mx)]
    for b in seqs:
        m_ref[b] = jnp.where((lane1 >= g0) & (lane1 < g0 + H), F_c[b][L - 1:L, :] + mx_last[b], 0.0)
    yield

    q_all = [qk_ref[b, :, :K] * (DK ** -0.5) for b in seqs]
    k_all = [qk_ref[b, :, K:] for b in seqs]
    k_bf = [x.astype(BF16) for x in k_all]
    C_all = [C_ref[b] for b in seqs]
    C_bf = [x.astype(BF16) for x in C_all]
    n_row = [n_ref[b] for b in seqs]

    half = lane1 < DK
    kw_cols, dec_lanes = [], []
    for b in seqs:
        cols = []
        for c in range(K // LANES):
            w_lo = wC[b][:, g0 + 2 * c:g0 + 2 * c + 1]
            w_hi = wC[b][:, g0 + 2 * c + 1:g0 + 2 * c + 2]
            cols.append(k_all[b][:, c * LANES:(c + 1) * LANES] * jnp.where(half, w_lo, w_hi))
        kw_cols.append(cols)
        d = dec[b][:, g0:g0 + 1]
        for h in range(1, H):
            d = jnp.where(lane >= h * DK, dec[b][:, g0 + h:g0 + h + 1], d)
        dec_lanes.append(d)
    for b in seqs:
        n_upd = jnp.concatenate([jnp.sum(x, axis=0, keepdims=True) for x in kw_cols[b]], axis=1)
        n_ref[b] = dec_lanes[b] * n_row[b] + n_upd
    yield

    chains = [(b, h) for b in seqs for h in range(H)]
    in_head = [(lane >= h * DK) & (lane < (h + 1) * DK) for h in range(H)]
    q_h = [jnp.where(in_head[h], q_all[b], 0.0) for b, h in chains]
    q_bf = [x.astype(BF16) for x in q_h]
    v_bf = [v_ref[b, :, h * DV:(h + 1) * DV].astype(BF16) for b, h in chains]
    kw = [kw_cols[b][h // 2][:, (h % 2) * DK:(h % 2 + 1) * DK].astype(BF16) for b, h in chains]
    expD = [jnp.exp(jnp.where(incl, (li_r[b][2 * H + h:2 * H + h + 1, :] - F_r[b][g0 + h:g0 + h + 1, :])
                              - mx[b][:, g0 + h:g0 + h + 1], -jnp.inf)) for b, h in chains]
    qn = [jnp.sum(x * n_row[b], axis=1, keepdims=True) for x, (b, h) in zip(q_h, chains)]
    yield

    qk = [_mm_nt(x, k_bf[b]) for x, (b, h) in zip(q_bf, chains)]
    yield
    Sm = [e * x for e, x in zip(expD, qk)]
    inter_state = [_mm(x, C_bf[b]) for x, (b, h) in zip(q_bf, chains)]
    yield
    intra = [_mm(x, v) for x, v in zip(Sm, v_bf)]
    yield
    upd = _mm_tn_list(kw, v_bf)
    rowsum = [jnp.sum(x, axis=1, keepdims=True) for x in Sm]
    yield
    for i, (b, h) in enumerate(chains):
        C_ref[b, h * DK:(h + 1) * DK, :] = dec[b][:, g0 + h:g0 + h + 1] * C_all[b][h * DK:(h + 1) * DK, :] + upd[i]
    a_h = [a_all[b][:, g0 + h:g0 + h + 1] for b, h in chains]
    den = [a * x + y for a, x, y in zip(a_h, qn, rowsum)]
    scale = [1.0 / jnp.maximum(jnp.abs(d), floor[b][:, g0 + h:g0 + h + 1]) for d, (b, h) in zip(den, chains)]
    hh = [(a * x + y) * sc for a, x, y, sc in zip(a_h, inter_state, intra, scale)]
    ms = [jnp.mean(x * x, axis=-1, keepdims=True) for x in hh]
    rs = [lax.rsqrt(x + EPS) for x in ms]
    for x, r, (b, h) in zip(hh, rs, chains):
        h_ref[b, :, h * DV:(h + 1) * DV] = x * r * norm_ref[:, h * DV:(h + 1) * DV]


N_GDN_IN, N_ML_IN = 9, 8


def _mixers_kernel(*refs, nb, L, NC, T_valid, gdn_dims, ml_dims):
    gdn_in, ml_in = refs[:N_GDN_IN], refs[N_GDN_IN:N_GDN_IN + N_ML_IN]
    o_ref, convn_ref, S_ref, h_ref, C_ref, n_ref, m_ref, xp_ref = refs[N_GDN_IN + N_ML_IN:]
    sm_ref, smT_ref = gdn_in[1], gdn_in[2]
    common = dict(nb=nb, L=L, NC=NC, T_valid=T_valid)
    gdn = _gdn_steps(*gdn_in, o_ref, convn_ref, S_ref, xp_ref, **common, **gdn_dims)
    ml = _mlstm_steps(ml_in[0], ml_in[1], sm_ref, smT_ref, *ml_in[2:], h_ref, C_ref, n_ref, m_ref,
                      **common, **ml_dims)
    _interleave([gdn, ml], [1, 1])


def _mixers(proj, small, smT, conv0_all, S0_all, C0_all, n0_all, m0_all, layer, prev, P, *, NC, L, T_valid, offs):
    depth, B, Hg, DKg, DVg = S0_all.shape
    CG = conv0_all.shape[-1]
    Vg = Hg * DVg
    _, _, Km, DVm = C0_all.shape
    Vm = P["ml_norm"].shape[-1]
    Hm = Vm // DVm
    DKm = Km // Hm
    T_pad = proj.shape[1]
    nb = _tile(B, max(1, CHAINS // Hg), 1)
    assert 2 * DKm == LANES and 4 * Hm <= LANES and Hm == Hg
    kern = functools.partial(_mixers_kernel, nb=nb, L=L, NC=NC, T_valid=T_valid,
                             gdn_dims=dict(H=Hg, DK=DKg, DV=DVg), ml_dims=dict(H=Hm, DK=DKm, DV=DVm))
    seq_blk = lambda *shape: pl.BlockSpec((None, nb) + shape, lambda i, n: (layer, i) + (0,) * len(shape))
    in_specs = [pl.BlockSpec((nb, L, CG), lambda i, n: (i, n, offs["qkv"] // CG)),
                pl.BlockSpec((nb, L, LANES), lambda i, n: (i, n, 0)),
                pl.BlockSpec((nb, 1, GATE_ROWS, L), lambda i, n: (i, n, 0, 0)),
                seq_blk(CONV_W - 1, CG),
                seq_blk(Hg, DKg, DVg),
                _layer_spec(layer, CONV_W, CG),
                _layer_spec(layer, 2, LANES),
                _layer_spec(layer, GATE_ROWS, 2),
                _layer_spec(layer, 1, Vg),
                pl.BlockSpec((nb, L, 2 * Km), lambda i, n: (i, n, offs["qk_m"] // (2 * Km))),
                pl.BlockSpec((nb, L, Vm), lambda i, n: (i, n, offs["v_m"] // Vm)),
                seq_blk(Km, DVm),
                seq_blk(1, Km),
                seq_blk(1, LANES),
                _layer_spec(layer, 2, LANES),
                _layer_spec(layer, GATE_ROWS, 2),
                _layer_spec(layer, 1, Vm)]
    operands = [proj, small, smT, conv0_all, S0_all, P["conv_w"], P["gdn_prow"], P["gdn_pcol"], P["gdn_norm"],
                proj, proj, C0_all, n0_all, m0_all, P["ml_prow"], P["ml_pcol"], P["ml_norm"]]
    assert len(operands) == N_GDN_IN + N_ML_IN
    state_out = {1: conv0_all, 2: S0_all, 4: C0_all, 5: n0_all, 6: m0_all}
    aliases = {}
    if prev is not None:
        kern = _skip_ref(kern, len(operands), len(prev))
        aliases = {len(operands) + k: o for k, o in enumerate(sorted(state_out))}
        in_specs += [pl.BlockSpec(memory_space=pl.ANY)] * len(prev)
        operands += list(prev)
    outs = pl.pallas_call(
        kern,
        grid=(B // nb, NC),
        in_specs=in_specs,
        out_specs=[pl.BlockSpec((nb, L, Vg), lambda i, n: (i, n, 0)),
                   seq_blk(CONV_W - 1, CG),
                   seq_blk(Hg, DKg, DVg),
                   pl.BlockSpec((nb, L, Vm), lambda i, n: (i, n, 0)),
                   seq_blk(Km, DVm),
                   seq_blk(1, Km),
                   seq_blk(1, LANES)],
        out_shape=[jax.ShapeDtypeStruct((B, T_pad, Vg), F32) if k == 0 else
                   jax.ShapeDtypeStruct((B, T_pad, Vm), F32) if k == 3 else
                   jax.ShapeDtypeStruct(state_out[k].shape, F32) for k in range(7)],
        scratch_shapes=[pltpu.VMEM((nb, SUBLANES + L, CG), F32)],
        input_output_aliases=aliases,
        compiler_params=pltpu.CompilerParams(dimension_semantics=("parallel", "arbitrary"),
                                             vmem_limit_bytes=VMEM_LIMIT),
        name="mixers_chunk",
    )(*operands)
    og, hm = outs[0], outs[3]
    return og, hm, [outs[k] for k in sorted(state_out)]


def _merge_kernel(og_ref, hm_ref, x_ref, g_ref, wg_ref, wbg_ref, wbm_ref, wout_ref, o_ref, *, Vg, Vm, D):
    dot = functools.partial(jnp.dot, preferred_element_type=F32)
    parts = _row_parts(x_ref.shape[0])
    x = [x_ref[p, :] for p in parts]
    xn = [(v * lax.rsqrt(jnp.mean(v * v, axis=-1, keepdims=True) + EPS) * g_ref[...]).astype(BF16) for v in x]
    z = [dot(v, wg_ref[:, 0:Vg]) for v in xn]
    og = [(og_ref[p, :] * (v * _sigmoid(v))).astype(BF16) for p, v in zip(parts, z)]
    br_g = [dot(v, wbg_ref[...]) for v in og]
    om = [dot(v, wg_ref[:, Vg:Vg + Vm]) for v in xn]
    hm = [(hm_ref[p, :] * _sigmoid(v)).astype(BF16) for p, v in zip(parts, om)]
    br_m = [dot(v, wbm_ref[...]) for v in hm]
    gg = [dot(v, wg_ref[:, Vg + Vm:Vg + Vm + D]) for v in xn]
    merged = [_sigmoid(g) * b for g, b in zip(gg, br_g)]
    gm = [dot(v, wg_ref[:, Vg + Vm + D:]) for v in xn]
    merged = [m + _sigmoid(g) * b for m, g, b in zip(merged, gm, br_m)]
    for p, v, m in zip(parts, x, merged):
        o_ref[p, :] = v + dot(m.astype(BF16), wout_ref[...])


def _merge(og, hm, x, gamma, wg, wbg, wbm, wout, layer):
    M, D = x.shape
    Vg = og.shape[1]
    Vm = hm.shape[1]
    tm = _tile(M, 512, SUBLANES * ROW_PARTS)
    return pl.pallas_call(
        functools.partial(_merge_kernel, Vg=Vg, Vm=Vm, D=D),
        grid=(M // tm,),
        in_specs=[pl.BlockSpec((tm, Vg), lambda i: (i, 0)),
                  pl.BlockSpec((tm, Vm), lambda i: (i, 0)),
                  pl.BlockSpec((tm, D), lambda i: (i, 0)),
                  _layer_spec(layer, 1, D),
                  _layer_spec(layer, D, Vg + Vm + 2 * D),
                  _layer_spec(layer, Vg, D),
                  _layer_spec(layer, Vm, D),
                  _layer_spec(layer, D, D)],
        out_specs=pl.BlockSpec((tm, D), lambda i: (i, 0)),
        out_shape=jax.ShapeDtypeStruct((M, D), F32),
        compiler_params=pltpu.CompilerParams(dimension_semantics=("parallel",), vmem_limit_bytes=VMEM_LIMIT),
        name="merge_out",
    )(og, hm, x, gamma, wg, wbg, wbm, wout)


def _mlp_kernel(x_ref, g_ref, wup_ref, wdn_ref, gf_ref, o_ref, *, final_norm, tf):
    x = x_ref[...]
    ms = jnp.mean(x * x, axis=-1, keepdims=True)
    xn = (x * lax.rsqrt(ms + EPS) * g_ref[...]).astype(BF16)
    y = x
    for c in range(wup_ref.shape[1] // tf):
        hcol = jnp.maximum(jnp.dot(xn, wup_ref[:, c * tf:(c + 1) * tf], preferred_element_type=F32), 0.0)
        y = y + jnp.dot((hcol * hcol).astype(BF16), wdn_ref[c * tf:(c + 1) * tf, :], preferred_element_type=F32)
    if final_norm:
        ms = jnp.mean(y * y, axis=-1, keepdims=True)
        y = y * lax.rsqrt(ms + EPS) * gf_ref[...]
    o_ref[...] = y


def _mlp(x, gamma, wup, wdn, gamma_final, layer, *, final_norm):
    M, D = x.shape
    FF = wup.shape[2]
    tm = _tile(M, 512, SUBLANES)
    tf = _tile(FF, 1024, LANES)
    resident = pl.Buffered(1)
    return pl.pallas_call(
        functools.partial(_mlp_kernel, final_norm=final_norm, tf=tf),
        grid=(M // tm,),
        in_specs=[pl.BlockSpec((tm, D), lambda i: (i, 0)),
                  _layer_spec(layer, 1, D),
                  pl.BlockSpec((None, D, FF), lambda i: (layer, 0, 0), pipeline_mode=resident),
                  pl.BlockSpec((None, FF, D), lambda i: (layer, 0, 0), pipeline_mode=resident),
                  pl.BlockSpec((1, D), lambda i: (0, 0))],
        out_specs=pl.BlockSpec((tm, D), lambda i: (i, 0)),
        out_shape=jax.ShapeDtypeStruct((M, D), F32),
        compiler_params=pltpu.CompilerParams(dimension_semantics=("parallel",), vmem_limit_bytes=VMEM_LIMIT),
        name="mlp",
    )(x, gamma, wup, wdn, gamma_final)


def _repack_kernel(w_ref, rec_ref, gate_ref, *, rec_moves, gate_moves, rec_fill):
    for out_ref, moves in ((rec_ref, rec_moves), (gate_ref, gate_moves)):
        for a, b, d in moves:
            out_ref[:, d:d + (b - a)] = w_ref[:, a:b].astype(BF16)
    rec_ref[:, rec_fill:] = jnp.zeros((rec_ref.shape[0], rec_ref.shape[1] - rec_fill), BF16)


def _pack_w_in(w_in, dims):
    D, Hg, DKg, DVg, Hm, DKm, DVm = dims
    Kg, Vg, Km, Vm = Hg * DKg, Hg * DVg, Hm * DKm, Hm * DVm
    sizes = (Kg, Kg, Vg, Vg, Hg, Hg, Km, Km, Vm, Vm, Hm, Hm, D, D)
    starts = [0]
    for s in sizes:
        starts.append(starts[-1] + s)
    assert 2 * Hg + 2 * Hm <= GATE_ROWS and Hg == Hm
    depth, _, IN = w_in.shape

    def layout(groups):
        moves, offs, off = [], [], 0
        for ranges in groups:
            offs.append(off)
            for a, b in ranges:
                moves.append((starts[a], starts[b], off))
                off += starts[b] - starts[a]
        return moves, offs, off

    rec_moves, rec_offs, rec_fill = layout([[(0, 3)], [(6, 8)], [(8, 9)], [(4, 6), (10, 12)]])
    gate_moves, _, n_gate = layout([[(3, 4)], [(9, 10)], [(12, 14)]])
    offs = dict(zip(("qkv", "qk_m", "v_m", "small"), rec_offs))
    n_rec = offs["small"] + LANES
    for name, width in (("qkv", 2 * Kg + Vg), ("qk_m", 2 * Km), ("v_m", Vm), ("small", LANES)):
        assert offs[name] % width == 0, (name, offs[name], width)
    tk = _tile(D, 256, 2 * SUBLANES)
    in_pad = -(-IN // LANES) * LANES
    w_cast = jnp.pad(w_in.astype(BF16), ((0, 0), (0, 0), (0, in_pad - IN)))
    w_rec, w_gate = pl.pallas_call(
        functools.partial(_repack_kernel, rec_moves=rec_moves, gate_moves=gate_moves, rec_fill=rec_fill),
        grid=(depth, D // tk),
        in_specs=[pl.BlockSpec((None, tk, in_pad), lambda l, i: (l, i, 0))],
        out_specs=[pl.BlockSpec((None, tk, n_rec), lambda l, i: (l, i, 0)),
                   pl.BlockSpec((None, tk, n_gate), lambda l, i: (l, i, 0))],
        out_shape=[jax.ShapeDtypeStruct((depth, D, n_rec), BF16),
                   jax.ShapeDtypeStruct((depth, D, n_gate), BF16)],
        compiler_params=pltpu.CompilerParams(dimension_semantics=("parallel", "parallel"),
                                             vmem_limit_bytes=VMEM_LIMIT),
        name="repack_w_in",
    )(w_cast)
    return w_rec, w_gate, offs


def _gate_params(first, lane_first, second, lane_second):
    depth, H = first.shape
    place = lambda v, lane: jnp.pad(v.astype(F32), ((0, 0), (lane, LANES - lane - H)))
    row = jnp.stack([place(first, lane_first), place(second, lane_second)], axis=1)
    return row, jnp.swapaxes(row[:, :, :GATE_ROWS], 1, 2)


def _head_norm_rows(w, H, DV):
    depth = w.shape[0]
    return jnp.broadcast_to(w.astype(F32).reshape(depth, -1, DV), (depth, H, DV)).reshape(depth, 1, H * DV)


def _trunk(x, conv0, S0, C0, n0, m0, P, *, B, T_valid, L):
    M, D = x.shape
    NC = -(-T_valid // L)
    T_pad = NC * L
    pad_t = lambda a: a if T_pad == T_valid else jnp.pad(a, ((0, 0), (0, T_pad - T_valid), (0, 0)))
    depth = P["w_in"].shape[0]
    offs = P["offs"]
    _, _, H, DK, DV = C0.shape
    C0_all = C0.reshape(depth, B, H * DK, DV)
    n0_all = n0.reshape(depth, B, 1, H * DK)
    m0_all = jnp.pad(m0.reshape(depth, B, 1, H), ((0, 0), (0, 0), (0, 0), (3 * H, LANES - 4 * H)))
    new = None
    for l in range(depth):
        proj, small = _norm_proj(x, P["norm_mix"], P["w_in"], l)
        proj3 = pad_t(proj.reshape(B, T_valid, proj.shape[1]))
        small3 = pad_t(small.reshape(B, T_valid, LANES))
        smT = jnp.swapaxes(small3[:, :, :GATE_ROWS].reshape(B, NC, L, GATE_ROWS), 2, 3)
        og, hm, new = _mixers(proj3, small3, smT, conv0, S0, C0_all, n0_all, m0_all, l, new, P,
                              NC=NC, L=L, T_valid=T_valid, offs=offs)
        x = _merge(og[:, :T_valid].reshape(M, -1), hm[:, :T_valid].reshape(M, -1), x, P["norm_mix"], P["w_gate"],
                   P["w_bg"], P["w_bm"], P["w_out"], l)
        x = _mlp(x, P["norm_mlp"], P["w_up"], P["w_down"], P["norm_final"], l, final_norm=(l == depth - 1))
    conv_all, S_all, C_all, n_all, m_all = new
    return x, conv_all, S_all, C_all.reshape(C0.shape), n_all.reshape(n0.shape), m_all[:, :, 0, 3 * H:4 * H]


def kernel(x_prompt, x_sample, state_gdn_conv, state_gdn_S, state_mlstm_C, state_mlstm_n, state_mlstm_m, norm_mix, w_in, gdn_conv_w, gdn_A_log, gdn_dt_bias, gdn_norm, ml_i_bias, ml_f_bias, ml_norm, w_branch_gdn, w_branch_ml, w_out, norm_mlp, w_up, w_down, norm_final):
    Bp, Tp, D = x_prompt.shape
    Bs, Ts, _ = x_sample.shape
    depth = w_in.shape[0]
    _, _, Hg, DKg, DVg = state_gdn_S.shape
    _, _, Hm, DKm, DVm = state_mlstm_C.shape
    CG = state_gdn_conv.shape[-1]
    dims = (D, Hg, DKg, DVg, Hm, DKm, DVm)
    assert Ts >= CONV_W - 1 and Tp >= CONV_W - 1

    w_rec, w_gate, offs = _pack_w_in(w_in, dims)
    P = {
        "offs": offs,
        "w_in": w_rec,
        "w_gate": w_gate,
        "norm_mix": norm_mix.reshape(depth, 1, D),
        "norm_mlp": norm_mlp.reshape(depth, 1, D),
        "norm_final": norm_final.reshape(1, D),
        "conv_w": gdn_conv_w,
        "w_bg": w_branch_gdn.astype(BF16),
        "w_bm": w_branch_ml.astype(BF16),
        "w_out": w_out.astype(BF16),
        "w_up": w_up.astype(BF16),
        "w_down": w_down.astype(BF16),
        "gdn_norm": _head_norm_rows(gdn_norm, Hg, DVg),
        "ml_norm": _head_norm_rows(ml_norm, Hm, DVm),
    }
    P["gdn_prow"], P["gdn_pcol"] = _gate_params(gdn_A_log, Hg, gdn_dt_bias, Hg)
    P["ml_prow"], P["ml_pcol"] = _gate_params(ml_i_bias, 2 * Hg, ml_f_bias, 2 * Hg + Hm)

    Lp = CHUNK if Tp % CHUNK == 0 else Tp
    assert Lp % SUBLANES == 0
    zeros = lambda *s: jnp.zeros(s, F32)
    yp, conv_p, S_p, C_p, n_p, m_p = _trunk(
        x_prompt.reshape(Bp * Tp, D), zeros(depth, Bp, CONV_W - 1, CG), zeros(depth, Bp, Hg, DKg, DVg),
        zeros(depth, Bp, Hm, DKm, DVm), zeros(depth, Bp, Hm, DKm), zeros(depth, Bp, Hm), P,
        B=Bp, T_valid=Tp, L=Lp)

    Ls = -(-Ts // SUBLANES) * SUBLANES
    ys, conv_s, S_s, C_s, n_s, m_s = _trunk(
        x_sample.reshape(Bs * Ts, D), state_gdn_conv, state_gdn_S, state_mlstm_C, state_mlstm_n, state_mlstm_m, P,
        B=Bs, T_valid=Ts, L=Ls)
    return (yp.reshape(Bp, Tp, D), ys.reshape(Bs, Ts, D), conv_p, S_p, C_p, n_p, m_p, conv_s, S_s, C_s, n_s, m_s)
```
